```python
import jax, jax.numpy as jnp
from jax import lax
import numpy as np

D_MODEL = 1024
BATCH = 8
SEQ = 2048
DEPTH = 2
DEC_BATCH = 128
DEC_SEQ = 1
PAST_LEN = 16384
PAGE_SIZE = 128

D_MIX = D_MODEL
D_A = D_MIX // 4
W_A = 3
D_B = D_MIX // 4
W_B = 31
D_C = D_MIX // 2
H_C = 4
DK = D_C // H_C
DV = D_C // H_C
W_C = 4
CHUNK = 64
D_IN = 3 * D_A + 2 * D_B + 3 * D_C + D_C + 2 * H_C
N_GROUPS = 4
EXP_PER_GROUP = 8
N_EXPERTS = N_GROUPS * EXP_PER_GROUP
TOP_K = 2
D_EXPERT = D_MODEL // 4
EPS = 1e-6

kernel_name = 'hybrid_conv_conformer_gdn_hmoe_step'


def rms_norm(x, g):
    xf = x.astype(jnp.float32)
    y = xf * lax.rsqrt(jnp.mean(xf * xf, axis=-1, keepdims=True) + EPS)
    return (y * g.astype(jnp.float32)).astype(x.dtype)


def layer_norm(x, g, b):
    xf = x.astype(jnp.float32)
    mu = jnp.mean(xf, axis=-1, keepdims=True)
    xc = xf - mu
    y = xc * lax.rsqrt(jnp.mean(xc * xc, axis=-1, keepdims=True) + EPS)
    return (y * g.astype(jnp.float32) + b.astype(jnp.float32)).astype(x.dtype)


def l2_normalize(x):
    xf = x.astype(jnp.float32)
    return xf * lax.rsqrt(jnp.sum(xf * xf, axis=-1, keepdims=True) + 1e-6)


def causal_depthwise_conv(x, prev, w):
    width = w.shape[0]
    xx = jnp.concatenate([prev.astype(x.dtype), x], axis=1)
    y = lax.conv_general_dilated(xx, w[:, None, :].astype(x.dtype), window_strides=(1,), padding='VALID',
                                 dimension_numbers=('NWC', 'WIO', 'NWC'), feature_group_count=x.shape[-1])
    return y, xx[:, xx.shape[1] - (width - 1):]


def gated_delta_rule(q, k, v, g, beta, s0):
    n_b, t_len = q.shape[0], q.shape[1]
    c = min(CHUNK, t_len)
    n_c = -(-t_len // c)
    pad = n_c * c - t_len

    def blk(a):
        a = jnp.pad(a, [(0, 0), (0, pad), (0, 0), (0, 0)])
        return a.reshape(n_b, n_c, c, a.shape[2], a.shape[3]).transpose(1, 0, 3, 2, 4)

    def blk_s(a):
        a = jnp.pad(a, [(0, 0), (0, pad), (0, 0)])
        return a.reshape(n_b, n_c, c, a.shape[2]).transpose(1, 0, 3, 2)

    qb, kb, vb = blk(q), blk(k), blk(v)
    gb, bb = blk_s(g), blk_s(beta)
    G = jnp.cumsum(gb, axis=-1)
    diff = G[..., :, None] - G[..., None, :]
    idx = jnp.arange(c)
    strict = idx[:, None] > idx[None, :]
    incl = idx[:, None] >= idx[None, :]
    kk = jnp.einsum('nbhid,nbhjd->nbhij', kb, kb)
    a_mat = bb[..., :, None] * kk * jnp.exp(jnp.where(strict, diff, -jnp.inf))
    rhs = jnp.concatenate([vb * bb[..., None], kb * (bb * jnp.exp(G))[..., None]], axis=-1)
    sol = lax.linalg.triangular_solve(a_mat, rhs, left_side=True, lower=True, unit_diagonal=True)
    u, w = sol[..., :DV], sol[..., DV:]
    qk = jnp.einsum('nbhid,nbhjd->nbhij', qb, kb) * jnp.exp(jnp.where(incl, diff, -jnp.inf))
    q_dec = qb * jnp.exp(G)[..., None]
    k_dec = kb * jnp.exp(G[..., -1:] - G)[..., None]
    g_last = jnp.exp(G[..., -1])

    def step(s, xs):
        u_c, w_c, qd_c, qk_c, kd_c, gl_c = xs
        v_new = u_c - jnp.einsum('bhck,bhkv->bhcv', w_c, s)
        o_c = jnp.einsum('bhck,bhkv->bhcv', qd_c, s) + jnp.einsum('bhij,bhjv->bhiv', qk_c, v_new)
        s = s * gl_c[..., None, None] + jnp.einsum('bhck,bhcv->bhkv', kd_c, v_new)
        return s, o_c

    s_fin, o = lax.scan(step, s0, (u, w, q_dec, qk, k_dec, g_last))
    o = o.transpose(1, 0, 3, 2, 4).reshape(n_b, n_c * c, H_C, DV)[:, :t_len]
    return o, s_fin


def hier_moe(h, w_group, w_router, w_gate, w_up, w_down):
    g_logits = jnp.einsum('btd,dg->btg', h, w_group).astype(jnp.float32)
    g_prob = jax.nn.softmax(g_logits, axis=-1)
    g_idx = jnp.argmax(g_logits, axis=-1)
    g_p = jnp.max(g_prob, axis=-1)
    e_logits = jnp.einsum('btd,gde->btge', h, w_router).astype(jnp.float32)
    e_logits = jnp.take_along_axis(e_logits, g_idx[..., None, None], axis=2)[..., 0, :]
    e_prob = jax.nn.softmax(e_logits, axis=-1)
    top_p, top_i = lax.top_k(e_prob, TOP_K)
    top_p = top_p / jnp.sum(top_p, axis=-1, keepdims=True)
    e_idx = g_idx[..., None] * EXP_PER_GROUP + top_i
    combine = jnp.sum(jax.nn.one_hot(e_idx, N_EXPERTS, dtype=jnp.float32)
                      * (g_p[..., None] * top_p)[..., None], axis=-2).astype(h.dtype)
    gate = jnp.einsum('btd,edf->btef', h, w_gate)
    up = jnp.einsum('btd,edf->btef', h, w_up)
    act = jax.nn.silu(gate) * up * combine[..., None]
    return jnp.einsum('btef,efd->btd', act, w_down)


def decoder_layer(x, conv_a_prev, conv_b_prev, conv_qkv_prev, delta_prev, norm1_g, w_in, conv_a_w, conv_b_w,
                  ln_b_g, ln_b_b, conv_c_w, a_log, dt_bias, o_norm_g, w_out, norm2_g, w_group, w_router,
                  w_gate, w_up, w_down):
    n_b, t_len, _ = x.shape
    h = rms_norm(x, norm1_g)
    z = jnp.einsum('btd,de->bte', h, w_in)
    offs = np.cumsum([D_A, D_A, D_A, D_B, D_B, 3 * D_C, D_C, H_C]).tolist()
    a_b, a_c, a_h, b_val, b_gate, qkv, c_gate, c_beta, c_alpha = jnp.split(z, offs, axis=-1)
    conv_a_out, conv_a_new = causal_depthwise_conv(a_c * a_h, conv_a_prev, conv_a_w)
    y_a = a_b * conv_a_out
    conv_b_out, conv_b_new = causal_depthwise_conv(b_val * jax.nn.sigmoid(b_gate), conv_b_prev, conv_b_w)
    y_b = jax.nn.silu(layer_norm(conv_b_out, ln_b_g, ln_b_b))
    qkv_conv, conv_qkv_new = causal_depthwise_conv(qkv, conv_qkv_prev, conv_c_w)
    qkv_conv = jax.nn.silu(qkv_conv)
    q, k, v = jnp.split(qkv_conv, 3, axis=-1)
    q = l2_normalize(q.reshape(n_b, t_len, H_C, DK)) * (DK ** -0.5)
    k = l2_normalize(k.reshape(n_b, t_len, H_C, DK))
    v = v.reshape(n_b, t_len, H_C, DV).astype(jnp.float32)
    beta = jax.nn.sigmoid(c_beta.astype(jnp.float32))
    g = -jnp.exp(a_log.astype(jnp.float32)) * jax.nn.softplus(c_alpha.astype(jnp.float32) + dt_bias.astype(jnp.float32))
    o, delta_new = gated_delta_rule(q, k, v, g, beta, delta_prev.astype(jnp.float32))
    o = rms_norm(o, o_norm_g) * jax.nn.silu(c_gate.reshape(n_b, t_len, H_C, DV).astype(jnp.float32))
    y_c = o.reshape(n_b, t_len, D_C).astype(x.dtype)
    x = x + jnp.einsum('bte,ed->btd', jnp.concatenate([y_a, y_b, y_c], axis=-1), w_out)
    x = x + hier_moe(rms_norm(x, norm2_g), w_group, w_router, w_gate, w_up, w_down)
    return x, conv_a_new, conv_b_new, conv_qkv_new, delta_new.astype(delta_prev.dtype)


def setup_inputs(seed: int = 0) -> dict:
    key = jax.random.key(seed)
    ks = jax.random.split(key, 26)

    def nrm(k, shape, scale):
        return jax.random.normal(k, shape, jnp.float32) * scale

    dt = jnp.exp(jax.random.uniform(ks[15], (DEPTH, H_C), jnp.float32, float(np.log(1e-3)), float(np.log(1e-1))))
    return {
        'x_prompt': nrm(ks[0], (BATCH, SEQ, D_MODEL), 1.0),
        'x_sample': nrm(ks[1], (DEC_BATCH, DEC_SEQ, D_MODEL), 1.0),
        'state_conv_a': nrm(ks[2], (DEPTH, DEC_BATCH, W_A - 1, D_A), 1.0),
        'state_conv_b': nrm(ks[3], (DEPTH, DEC_BATCH, W_B - 1, D_B), 0.5),
        'state_conv_qkv': nrm(ks[4], (DEPTH, DEC_BATCH, W_C - 1, 3 * D_C), 1.0),
        'state_delta': nrm(ks[5], (DEPTH, DEC_BATCH, H_C, DK, DV), 0.3),
        'norm1_g': 1.0 + nrm(ks[6], (DEPTH, D_MODEL), 0.02),
        'w_in': nrm(ks[7], (DEPTH, D_MODEL, D_IN), D_MODEL ** -0.5),
        'conv_a_w': nrm(ks[8], (DEPTH, W_A, D_A), W_A ** -0.5),
        'conv_b_w': nrm(ks[9], (DEPTH, W_B, D_B), W_B ** -0.5),
        'ln_b_g': 1.0 + nrm(ks[10], (DEPTH, D_B), 0.02),
        'ln_b_b': nrm(ks[11], (DEPTH, D_B), 0.02),
        'conv_c_w': nrm(ks[12], (DEPTH, W_C, 3 * D_C), W_C ** -0.5),
        'a_log': jnp.log(jax.random.uniform(ks[13], (DEPTH, H_C), jnp.float32, 1.0, 16.0)),
        'dt_bias': jnp.log(jnp.expm1(dt)),
        'o_norm_g': 1.0 + nrm(ks[14], (DEPTH, DV), 0.02),
        'w_out': nrm(ks[16], (DEPTH, D_MIX, D_MODEL), D_MIX ** -0.5),
        'norm2_g': 1.0 + nrm(ks[17], (DEPTH, D_MODEL), 0.02),
        'w_group': nrm(ks[18], (DEPTH, D_MODEL, N_GROUPS), D_MODEL ** -0.5),
        'w_router': nrm(ks[19], (DEPTH, N_GROUPS, D_MODEL, EXP_PER_GROUP), D_MODEL ** -0.5),
        'w_gate': nrm(ks[20], (DEPTH, N_EXPERTS, D_MODEL, D_EXPERT), D_MODEL ** -0.5),
        'w_up': nrm(ks[21], (DEPTH, N_EXPERTS, D_MODEL, D_EXPERT), D_MODEL ** -0.5),
        'w_down': nrm(ks[22], (DEPTH, N_EXPERTS, D_EXPERT, D_MODEL), D_EXPERT ** -0.5),
        'final_g': 1.0 + nrm(ks[23], (D_MODEL,), 0.02),
    }


def reference(x_prompt, x_sample, state_conv_a, state_conv_b, state_conv_qkv, state_delta, norm1_g, w_in,
              conv_a_w, conv_b_w, ln_b_g, ln_b_b, conv_c_w, a_log, dt_bias, o_norm_g, w_out, norm2_g, w_group,
              w_router, w_gate, w_up, w_down, final_g):
    n_p = x_prompt.shape[0]
    xp, xs = x_prompt, x_sample
    pa, pb, pq, pd, sa, sb, sq, sd = [], [], [], [], [], [], [], []
    for l in range(DEPTH):
        params = (norm1_g[l], w_in[l], conv_a_w[l], conv_b_w[l], ln_b_g[l], ln_b_b[l], conv_c_w[l], a_log[l],
                  dt_bias[l], o_norm_g[l], w_out[l], norm2_g[l], w_group[l], w_router[l], w_gate[l], w_up[l], w_down[l])
        xp, ca, cb, cq, dd = decoder_layer(
            xp, jnp.zeros((n_p, W_A - 1, D_A), xp.dtype), jnp.zeros((n_p, W_B - 1, D_B), xp.dtype),
            jnp.zeros((n_p, W_C - 1, 3 * D_C), xp.dtype), jnp.zeros((n_p, H_C, DK, DV), state_delta.dtype), *params)
        pa.append(ca); pb.append(cb); pq.append(cq); pd.append(dd)
        xs, ca, cb, cq, dd = decoder_layer(xs, state_conv_a[l], state_conv_b[l], state_conv_qkv[l], state_delta[l], *params)
        sa.append(ca); sb.append(cb); sq.append(cq); sd.append(dd)
    y_prompt = rms_norm(xp, final_g)
    y_sample = rms_norm(xs, final_g)
    new_conv_a_prompt = jnp.stack(pa)
    new_conv_b_prompt = jnp.stack(pb)
    new_conv_qkv_prompt = jnp.stack(pq)
    new_delta_prompt = jnp.stack(pd)
    new_conv_a_sample = jnp.stack(sa)
    new_conv_b_sample = jnp.stack(sb)
    new_conv_qkv_sample = jnp.stack(sq)
    new_delta_sample = jnp.stack(sd)
    return (y_prompt, y_sample, new_conv_a_prompt, new_conv_b_prompt, new_conv_qkv_prompt, new_delta_prompt,
            new_conv_a_sample, new_conv_b_sample, new_conv_qkv_sample, new_delta_sample)
```

```python
import functools

import jax
import jax.numpy as jnp
from jax import lax
from jax.experimental import pallas as pl
from jax.experimental.pallas import tpu as pltpu

F32 = jnp.float32
BF16 = jnp.bfloat16
HIGHEST = lax.Precision.HIGHEST

EPS = 1e-6
H_C = 4
DK = 128
DV = 128
D_A = 256
D_B = 256
D_C = 512
W_A = 3
W_B = 31
W_C = 4
CHUNK = 64
N_GROUPS = 4
EXP_PER_GROUP = 8
N_EXPERTS = N_GROUPS * EXP_PER_GROUP
LANES = 128
PAD_A = 8
PAD_B = 32
PAD_C = 8
EXPERT_LANE0 = N_GROUPS
VMEM_LIMIT = 56 * 1024 * 1024


def _silu(x):
    return x * (1.0 / (1.0 + jnp.exp(-x)))


def _sigmoid(x):
    return 1.0 / (1.0 + jnp.exp(-x))


def _softplus(x):
    return jnp.maximum(x, 0.0) + jnp.log1p(jnp.exp(-jnp.abs(x)))


def _rms(x, g):
    return x * lax.rsqrt(jnp.mean(x * x, axis=-1, keepdims=True) + EPS) * g


def _conv_taps(buf, w_ref, width, pad, n_rows, n_cols, row_blk, col_blk):
    row_parts = []
    for r0 in range(0, n_rows, row_blk):
        col_parts = []
        for c0 in range(0, n_cols, col_blk):
            acc = None
            for j in range(width):
                start = pad - (width - 1) + j + r0
                term = buf[start:start + row_blk, c0:c0 + col_blk] * w_ref[j:j + 1, c0:c0 + col_blk]
                acc = term if acc is None else acc + term
            col_parts.append(acc)
        row_parts.append(col_parts[0] if len(col_parts) == 1 else jnp.concatenate(col_parts, axis=1))
    return row_parts[0] if len(row_parts) == 1 else jnp.concatenate(row_parts, axis=0)


def _l2norm_heads(x, scale):
    parts = []
    for h in range(H_C):
        xh = x[:, h * DK:(h + 1) * DK]
        parts.append(xh * (lax.rsqrt(jnp.sum(xh * xh, axis=-1, keepdims=True) + 1e-6) * scale))
    return jnp.concatenate(parts, axis=1)


def _in_projection(x, g1_ref, wmain_ref, wsmall_ref):
    h = _rms(x, g1_ref[...])
    hb = h.astype(BF16)
    z_a = jnp.dot(hb, wmain_ref[:, 0:768], preferred_element_type=F32)
    z_b = jnp.dot(hb, wmain_ref[:, 768:1280], preferred_element_type=F32)
    z_qkv = jnp.dot(hb, wmain_ref[:, 1280:2816], preferred_element_type=F32)
    z_gate = jnp.dot(hb, wmain_ref[:, 2816:3328], preferred_element_type=F32)
    z_s = jnp.dot(h, wsmall_ref[...], preferred_element_type=F32, precision=HIGHEST)
    return z_a, z_b, z_qkv, z_gate, z_s


def _layer_norm_silu(x, g, b):
    mu = jnp.mean(x, axis=-1, keepdims=True)
    xc = x - mu
    y = xc * lax.rsqrt(jnp.mean(xc * xc, axis=-1, keepdims=True) + EPS)
    return _silu(y * g + b)


def _decay_and_beta(z_s, alog_ref, dtb_ref):
    lane = lax.broadcasted_iota(jnp.int32, z_s.shape, 1)
    g = -jnp.exp(alog_ref[...]) * _softplus(z_s + dtb_ref[...])
    g = jnp.where(lane < H_C, g, 0.0)
    beta = jnp.where((lane >= H_C) & (lane < 2 * H_C), _sigmoid(z_s), 0.0)
    return g, beta


def _inproj_prompt_kernel(x_ref, g1_ref, wmain_ref, wsmall_ref, caw_ref, cbw_ref, lng_ref, lnb_ref,
                          ccw_ref, alog_ref, dtb_ref,
                          yab_ref, q_ref, k_ref, v_ref, gate_ref, gb_ref, sta_ref, stb_ref, stq_ref,
                          abuf, bbuf, cbuf, *, tt, nt):
    t = pl.program_id(1)
    z_a, z_b, z_qkv, z_gate, z_s = _in_projection(x_ref[...], g1_ref, wmain_ref, wsmall_ref)

    @pl.when(t == 0)
    def _():
        abuf[0:PAD_A, :] = jnp.zeros((PAD_A, D_A), F32)
        bbuf[0:PAD_B, :] = jnp.zeros((PAD_B, D_B), F32)
        cbuf[0:PAD_C, :] = jnp.zeros((PAD_C, 3 * D_C), F32)

    abuf[PAD_A:PAD_A + tt, :] = z_a[:, 256:512] * z_a[:, 512:768]
    conv_a = _conv_taps(abuf, caw_ref, W_A, PAD_A, tt, D_A, 64, 256)
    yab_ref[:, 0:D_A] = z_a[:, 0:256] * conv_a

    bbuf[PAD_B:PAD_B + tt, :] = z_b[:, 0:256] * _sigmoid(z_b[:, 256:512])
    conv_b = _conv_taps(bbuf, cbw_ref, W_B, PAD_B, tt, D_B, 64, 256)
    yab_ref[:, D_A:D_A + D_B] = _layer_norm_silu(conv_b, lng_ref[...], lnb_ref[...])

    cbuf[PAD_C:PAD_C + tt, :] = z_qkv
    qkv = _silu(_conv_taps(cbuf, ccw_ref, W_C, PAD_C, tt, 3 * D_C, 64, 512))
    q_ref[...] = _l2norm_heads(qkv[:, 0:D_C], DK ** -0.5)
    k_ref[...] = _l2norm_heads(qkv[:, D_C:2 * D_C], 1.0)
    v_ref[...] = qkv[:, 2 * D_C:3 * D_C]
    gate_ref[...] = z_gate

    g, beta = _decay_and_beta(z_s, alog_ref, dtb_ref)
    row_in_chunk = lax.broadcasted_iota(jnp.int32, g.shape, 0) & (CHUNK - 1)
    shift = 1
    while shift < CHUNK:
        g = g + jnp.where(row_in_chunk >= shift, pltpu.roll(g, shift, axis=0), 0.0)
        shift *= 2
    gb_ref[...] = g + beta

    @pl.when(t == nt - 1)
    def _():
        sta_ref[0] = abuf[PAD_A + tt - (W_A - 1):PAD_A + tt, :]
        stb_ref[0] = bbuf[PAD_B + tt - (W_B - 1):PAD_B + tt, :]
        stq_ref[0] = cbuf[PAD_C + tt - (W_C - 1):PAD_C + tt, :]

    abuf[0:PAD_A, :] = abuf[tt:tt + PAD_A, :]
    bbuf[0:PAD_B, :] = bbuf[tt:tt + PAD_B, :]
    cbuf[0:PAD_C, :] = cbuf[tt:tt + PAD_C, :]


def _const_spec(shape):
    nd = len(shape)
    return pl.BlockSpec(shape, lambda *_: (0,) * nd)


def _inproj_prompt(x2d, n_b, t_len, g1, wmain, wsmall, caw, cbw, lng, lnb, ccw, alog, dtb):
    tt = min(512, t_len)
    nt = t_len // tt
    n_tok = n_b * t_len
    d = x2d.shape[1]
    row = lambda w: pl.BlockSpec((tt, w), lambda b, t: (b * nt + t, 0))
    st = lambda r, w: pl.BlockSpec((1, r, w), lambda b, t: (b, 0, 0))
    out_shape = (
        jax.ShapeDtypeStruct((n_tok, D_A + D_B), F32),
        jax.ShapeDtypeStruct((n_tok, D_C), F32),
        jax.ShapeDtypeStruct((n_tok, D_C), F32),
        jax.ShapeDtypeStruct((n_tok, D_C), F32),
        jax.ShapeDtypeStruct((n_tok, D_C), F32),
        jax.ShapeDtypeStruct((n_tok, LANES), F32),
        jax.ShapeDtypeStruct((n_b, W_A - 1, D_A), F32),
        jax.ShapeDtypeStruct((n_b, W_B - 1, D_B), F32),
        jax.ShapeDtypeStruct((n_b, W_C - 1, 3 * D_C), F32),
    )
    return pl.pallas_call(
        functools.partial(_inproj_prompt_kernel, tt=tt, nt=nt),
        grid=(n_b, nt),
        in_specs=[row(d), _const_spec(g1.shape), _const_spec(wmain.shape), _const_spec(wsmall.shape),
                  _const_spec(caw.shape), _const_spec(cbw.shape), _const_spec(lng.shape), _const_spec(lnb.shape),
                  _const_spec(ccw.shape), _const_spec(alog.shape), _const_spec(dtb.shape)],
        out_specs=(row(D_A + D_B), row(D_C), row(D_C), row(D_C), row(D_C), row(LANES),
                   st(W_A - 1, D_A), st(W_B - 1, D_B), st(W_C - 1, 3 * D_C)),
        out_shape=out_shape,
        scratch_shapes=[pltpu.VMEM((PAD_A + tt, D_A), F32), pltpu.VMEM((PAD_B + tt, D_B), F32),
                        pltpu.VMEM((PAD_C + tt, 3 * D_C), F32)],
        compiler_params=pltpu.CompilerParams(dimension_semantics=("arbitrary", "arbitrary"),
                                             vmem_limit_bytes=VMEM_LIMIT),
        name="inproj_prompt",
    )(x2d, g1, wmain, wsmall, caw, cbw, lng, lnb, ccw, alog, dtb)


def _dot_nt(a, b):
    return lax.dot_general(a, b, (((1,), (1,)), ((), ())), preferred_element_type=F32)


def _dot_tn(a, b):
    return lax.dot_general(a, b, (((0,), (0,)), ((), ())), preferred_element_type=F32)


def _delta_chunk_head(q, k, v, g_col, b_col, s, n_double):
    c = q.shape[0]
    ri = lax.broadcasted_iota(jnp.int32, (c, c), 0)
    ci = lax.broadcasted_iota(jnp.int32, (c, c), 1)
    g_row = jnp.sum(jnp.where(ri == ci, g_col, 0.0), axis=0, keepdims=True)
    diff = g_col - g_row
    dec_strict = jnp.exp(jnp.where(ri > ci, diff, -jnp.inf))
    dec_incl = jnp.exp(jnp.where(ri >= ci, diff, -jnp.inf))
    e_g = jnp.exp(g_col)
    kb = k.astype(BF16)
    kk = _dot_nt(kb, kb)
    qk = _dot_nt(q.astype(BF16), kb) * dec_incl
    x = -(b_col * kk * dec_strict)
    t_inv = jnp.where(ri == ci, 1.0, 0.0) + x
    p = x
    for _ in range(n_double):
        pb = p.astype(BF16)
        p = jnp.dot(pb, pb, preferred_element_type=F32)
        t_inv = t_inv + jnp.dot(t_inv.astype(BF16), p.astype(BF16), preferred_element_type=F32)
    rhs = jnp.concatenate([v * b_col, k * (b_col * e_g)], axis=1).astype(BF16)
    sol = jnp.dot(t_inv.astype(BF16), rhs, preferred_element_type=F32)
    u, w = sol[:, :DV], sol[:, DV:]
    sb = s.astype(BF16)
    ws_qs = jnp.dot(jnp.concatenate([w, q * e_g], axis=0).astype(BF16), sb, preferred_element_type=F32)
    v_new = u - ws_qs[:c]
    o = ws_qs[c:] + jnp.dot(qk.astype(BF16), v_new.astype(BF16), preferred_element_type=F32)
    g_last = g_col[c - 1:c, :]
    k_dec = k * jnp.exp(g_last - g_col)
    s_new = s * jnp.exp(g_last) + _dot_tn(k_dec.astype(BF16), v_new.astype(BF16))
    return o, s_new


def _gated_norm(o, on_g, gate):
    return o * lax.rsqrt(jnp.mean(o * o, axis=-1, keepdims=True) + EPS) * on_g * _silu(gate)


def _delta_prompt_kernel(q_ref, k_ref, v_ref, gb_ref, gate_ref, ong_ref, yc_ref, sfin_ref, s_scr, *, tq, nt):
    t = pl.program_id(1)

    @pl.when(t == 0)
    def _():
        s_scr[...] = jnp.zeros(s_scr.shape, F32)

    n_double = CHUNK.bit_length() - 2

    def chunk_body(ci, carry):
        r0 = pl.multiple_of(ci * CHUNK, CHUNK)
        gb = gb_ref[pl.ds(r0, CHUNK), :]
        for h in range(H_C):
            cols = slice(h * DK, (h + 1) * DK)
            o, s_new = _delta_chunk_head(q_ref[pl.ds(r0, CHUNK), cols], k_ref[pl.ds(r0, CHUNK), cols],
                                         v_ref[pl.ds(r0, CHUNK), cols], gb[:, h:h + 1],
                                         gb[:, H_C + h:H_C + h + 1], s_scr[h], n_double)
            s_scr[h] = s_new
            yc_ref[pl.ds(r0, CHUNK), cols] = _gated_norm(o, ong_ref[...], gate_ref[pl.ds(r0, CHUNK), cols])
        return carry

    lax.fori_loop(0, tq // CHUNK, chunk_body, 0)

    @pl.when(t == nt - 1)
    def _():
        sfin_ref[0] = s_scr[...]


def _delta_prompt(q, k, v, gb, gate, ong, n_b, t_len):
    tq = min(512, t_len)
    nt = t_len // tq
    row = lambda w: pl.BlockSpec((tq, w), lambda b, t: (b * nt + t, 0))
    return pl.pallas_call(
        functools.partial(_delta_prompt_kernel, tq=tq, nt=nt),
        grid=(n_b, nt),
        in_specs=[row(D_C), row(D_C), row(D_C), row(LANES), row(D_C), _const_spec(ong.shape)],
        out_specs=(row(D_C), pl.BlockSpec((1, H_C, DK, DV), lambda b, t: (b, 0, 0, 0))),
        out_shape=(jax.ShapeDtypeStruct((n_b * t_len, D_C), F32),
                   jax.ShapeDtypeStruct((n_b, H_C, DK, DV), F32)),
        scratch_shapes=[pltpu.VMEM((H_C, DK, DV), F32)],
        compiler_params=pltpu.CompilerParams(dimension_semantics=("arbitrary", "arbitrary"),
                                             vmem_limit_bytes=VMEM_LIMIT),
        name="delta_prompt",
    )(q, k, v, gb, gate, ong)


def _inproj_sample_kernel(x_ref, g1_ref, wmain_ref, wsmall_ref, caw_ref, cbw_ref, lng_ref, lnb_ref,
                          ccw_ref, alog_ref, dtb_ref, sta_ref, stb_ref, stq_ref,
                          yab_ref, q_ref, k_ref, v_ref, gate_ref, gb_ref, una_ref, unb_ref, unq_ref):
    z_a, z_b, z_qkv, z_gate, z_s = _in_projection(x_ref[...], g1_ref, wmain_ref, wsmall_ref)

    def conv_step(state_ref, new, w_ref, width):
        acc = new * w_ref[width - 1:width, :]
        for j in range(width - 1):
            acc = acc + state_ref[j] * w_ref[j:j + 1, :]
        return acc

    u_a = z_a[:, 256:512] * z_a[:, 512:768]
    una_ref[...] = u_a
    yab_ref[:, 0:D_A] = z_a[:, 0:256] * conv_step(sta_ref, u_a, caw_ref, W_A)

    u_b = z_b[:, 0:256] * _sigmoid(z_b[:, 256:512])
    unb_ref[...] = u_b
    yab_ref[:, D_A:D_A + D_B] = _layer_norm_silu(conv_step(stb_ref, u_b, cbw_ref, W_B), lng_ref[...], lnb_ref[...])

    unq_ref[...] = z_qkv
    qkv = _silu(conv_step(stq_ref, z_qkv, ccw_ref, W_C))
    q_ref[...] = _l2norm_heads(qkv[:, 0:D_C], DK ** -0.5)
    k_ref[...] = _l2norm_heads(qkv[:, D_C:2 * D_C], 1.0)
    v_ref[...] = qkv[:, 2 * D_C:3 * D_C]
    gate_ref[...] = z_gate
    g, beta = _decay_and_beta(z_s, alog_ref, dtb_ref)
    gb_ref[...] = g + beta


def _inproj_sample(x2d, g1, wmain, wsmall, caw, cbw, lng, lnb, ccw, alog, dtb, st_a, st_b, st_q):
    n = x2d.shape[0]
    args = (x2d, g1, wmain, wsmall, caw, cbw, lng, lnb, ccw, alog, dtb, st_a, st_b, st_q)
    out_widths = (D_A + D_B, D_C, D_C, D_C, D_C, LANES, D_A, D_B, 3 * D_C)
    return pl.pallas_call(
        _inproj_sample_kernel,
        grid=(1,),
        in_specs=[_const_spec(a.shape) for a in args],
        out_specs=tuple(_const_spec((n, w)) for w in out_widths),
        out_shape=tuple(jax.ShapeDtypeStruct((n, w), F32) for w in out_widths),
        compiler_params=pltpu.CompilerParams(dimension_semantics=("arbitrary",), vmem_limit_bytes=VMEM_LIMIT),
        name="inproj_sample",
    )(*args)


def _delta_sample_kernel(q_ref, k_ref, v_ref, gb_ref, gate_ref, ong_ref, s_ref, yc_ref, snew_ref, *, nb):
    q = q_ref[...]
    k = k_ref[...]
    rows = [k[:, h * DK:(h + 1) * DK] for h in range(H_C)] + [q[:, h * DK:(h + 1) * DK] for h in range(H_C)]
    rows.append(jnp.zeros((LANES - 2 * H_C * nb, DK), F32))
    kq_t = jnp.concatenate(rows, axis=0).T
    gb = gb_ref[...]
    for i in range(nb):
        for h in range(H_C):
            cols = slice(h * DK, (h + 1) * DK)
            s = s_ref[i, h]
            k_col = kq_t[:, h * nb + i:h * nb + i + 1]
            q_col = kq_t[:, (H_C + h) * nb + i:(H_C + h) * nb + i + 1]
            k_s = jnp.sum(s * k_col, axis=0, keepdims=True)
            q_s = jnp.sum(s * q_col, axis=0, keepdims=True)
            e_g = jnp.exp(gb[i:i + 1, h:h + 1])
            beta = gb[i:i + 1, H_C + h:H_C + h + 1]
            v_new = beta * (v_ref[i:i + 1, cols] - e_g * k_s)
            qk = jnp.sum(q[i:i + 1, cols] * k[i:i + 1, cols], axis=-1, keepdims=True)
            o = e_g * q_s + qk * v_new
            snew_ref[i, h] = e_g * s + k_col * v_new
            yc_ref[i:i + 1, cols] = _gated_norm(o, ong_ref[...], gate_ref[i:i + 1, cols])


def _delta_sample(q, k, v, gb, gate, ong, state):
    n = q.shape[0]
    nb = 8
    row = lambda w: pl.BlockSpec((nb, w), lambda i: (i, 0))
    st = pl.BlockSpec((nb, H_C, DK, DV), lambda i: (i, 0, 0, 0))
    return pl.pallas_call(
        functools.partial(_delta_sample_kernel, nb=nb),
        grid=(n // nb,),
        in_specs=[row(D_C), row(D_C), row(D_C), row(LANES), row(D_C), _const_spec(ong.shape), st],
        out_specs=(row(D_C), st),
        out_shape=(jax.ShapeDtypeStruct((n, D_C), F32), jax.ShapeDtypeStruct(state.shape, F32)),
        compiler_params=pltpu.CompilerParams(dimension_semantics=("arbitrary",), vmem_limit_bytes=VMEM_LIMIT),
        name="delta_sample",
    )(q, k, v, gb, gate, ong, state)


def _outproj_router_kernel(x_ref, yab_ref, yc_ref, woa_ref, woc_ref, g2_ref, wrt_ref, x1_ref, h2_ref, comb_ref):
    x1 = (x_ref[...]
          + jnp.dot(yab_ref[...].astype(BF16), woa_ref[...], preferred_element_type=F32)
          + jnp.dot(yc_ref[...].astype(BF16), woc_ref[...], preferred_element_type=F32))
    x1_ref[...] = x1
    h2 = _rms(x1, g2_ref[...])
    h2_ref[...] = h2.astype(BF16)
    logits = jnp.dot(h2, wrt_ref[...], preferred_element_type=F32, precision=HIGHEST)
    lane = lax.broadcasted_iota(jnp.int32, logits.shape, 1)
    neg = -jnp.inf
    gl = jnp.where(lane < N_GROUPS, logits, neg)
    g_max = jnp.max(gl, axis=-1, keepdims=True)
    g_idx = jnp.min(jnp.where(gl == g_max, lane, LANES), axis=-1, keepdims=True)
    g_p = 1.0 / jnp.sum(jnp.exp(gl - g_max), axis=-1, keepdims=True)
    lo = EXPERT_LANE0 + g_idx * EXP_PER_GROUP
    in_group = (lane >= lo) & (lane < lo + EXP_PER_GROUP)
    el = jnp.where(in_group, logits, neg)
    e_max = jnp.max(el, axis=-1, keepdims=True)
    pe = jnp.exp(el - e_max)
    e_prob = pe / jnp.sum(pe, axis=-1, keepdims=True)
    p1 = jnp.max(e_prob, axis=-1, keepdims=True)
    i1 = jnp.min(jnp.where(e_prob == p1, lane, LANES), axis=-1, keepdims=True)
    rest = jnp.where(in_group & (lane != i1), e_prob, -1.0)
    p2 = jnp.max(rest, axis=-1, keepdims=True)
    i2 = jnp.min(jnp.where(rest == p2, lane, LANES), axis=-1, keepdims=True)
    denom = p1 + p2
    comb_ref[...] = (jnp.where(lane == i1, g_p * (p1 / denom), 0.0)
                     + jnp.where(lane == i2, g_p * (p2 / denom), 0.0))


def _outproj_router(x2d, yab, yc, woa, woc, g2, wrt):
    n, d = x2d.shape
    tm = min(512, n)
    row = lambda w: pl.BlockSpec((tm, w), lambda i: (i, 0))
    return pl.pallas_call(
        _outproj_router_kernel,
        grid=(n // tm,),
        in_specs=[row(d), row(D_A + D_B), row(D_C), _const_spec(woa.shape), _const_spec(woc.shape),
                  _const_spec(g2.shape), _const_spec(wrt.shape)],
        out_specs=(row(d), row(d), row(LANES)),
        out_shape=(jax.ShapeDtypeStruct((n, d), F32), jax.ShapeDtypeStruct((n, d), BF16),
                   jax.ShapeDtypeStruct((n, LANES), F32)),
        compiler_params=pltpu.CompilerParams(dimension_semantics=("arbitrary",), vmem_limit_bytes=VMEM_LIMIT),
        name="outproj_router",
    )(x2d, yab, yc, woa, woc, g2, wrt)


def _moe_kernel(x1_ref, h2_ref, comb_ref, wg_ref, wu_ref, wd_ref, fg_ref, out_ref, *, final_norm):
    e = pl.program_id(1)

    @pl.when(e == 0)
    def _():
        out_ref[...] = x1_ref[...]

    h2 = h2_ref[...]
    gate = jnp.dot(h2, wg_ref[0], preferred_element_type=F32)
    up = jnp.dot(h2, wu_ref[0], preferred_element_type=F32)
    comb = comb_ref[...]
    lane = lax.broadcasted_iota(jnp.int32, comb.shape, 1)
    c_col = jnp.sum(jnp.where(lane == EXPERT_LANE0 + e, comb, 0.0), axis=-1, keepdims=True)
    act = (_silu(gate) * up * c_col).astype(BF16)
    out_ref[...] += jnp.dot(act, wd_ref[0], preferred_element_type=F32)

    if final_norm:
        @pl.when(e == pl.num_programs(1) - 1)
        def _():
            out_ref[...] = _rms(out_ref[...], fg_ref[...])


def _moe(x1, h2, comb, wg, wu, wd, fg, final_norm):
    n, d = x1.shape
    tm = min(1024, n)
    n_e, _, d_e = wg.shape
    row = lambda w: pl.BlockSpec((tm, w), lambda i, e: (i, 0))
    return pl.pallas_call(
        functools.partial(_moe_kernel, final_norm=final_norm),
        grid=(n // tm, n_e),
        in_specs=[row(d), row(d), row(LANES),
                  pl.BlockSpec((1, d, d_e), lambda i, e: (e, 0, 0)),
                  pl.BlockSpec((1, d, d_e), lambda i, e: (e, 0, 0)),
                  pl.BlockSpec((1, d_e, d), lambda i, e: (e, 0, 0)),
                  pl.BlockSpec(fg.shape, lambda i, e: (0, 0))],
        out_specs=row(d),
        out_shape=jax.ShapeDtypeStruct((n, d), F32),
        compiler_params=pltpu.CompilerParams(dimension_semantics=("arbitrary", "arbitrary"),
                                             vmem_limit_bytes=VMEM_LIMIT),
        name="moe",
    )(x1, h2, comb, wg, wu, wd, fg)


def _pad_lanes(a, lane0=0):
    a = a.reshape((1, -1)) if a.ndim == 1 else a
    return jnp.pad(a, ((0, 0), (lane0, LANES - lane0 - a.shape[1])))


def kernel(x_prompt, x_sample, state_conv_a, state_conv_b, state_conv_qkv, state_delta, norm1_g, w_in, conv_a_w,
           conv_b_w, ln_b_g, ln_b_b, conv_c_w, a_log, dt_bias, o_norm_g, w_out, norm2_g, w_group, w_router,
           w_gate, w_up, w_down, final_g):
    n_b, t_len, d = x_prompt.shape
    n_s = x_sample.shape[0]
    depth = w_in.shape[0]
    n_main = 3 * D_A + 2 * D_B + 3 * D_C + D_C
    xp = x_prompt.reshape(n_b * t_len, d)
    xs = x_sample.reshape(n_s, d)
    fg = final_g.reshape(1, d)
    outs = {k: [] for k in ("pa", "pb", "pq", "pd", "sa", "sb", "sq", "sd")}
    for l in range(depth):
        g1 = norm1_g[l].reshape(1, d)
        wmain = w_in[l][:, :n_main].astype(BF16)
        wsmall = _pad_lanes(jnp.concatenate([w_in[l][:, n_main + H_C:], w_in[l][:, n_main:n_main + H_C]], axis=1))
        caw, cbw, ccw = conv_a_w[l], conv_b_w[l], conv_c_w[l]
        lng, lnb = ln_b_g[l].reshape(1, D_B), ln_b_b[l].reshape(1, D_B)
        alog, dtb = _pad_lanes(a_log[l]), _pad_lanes(dt_bias[l])
        ong = o_norm_g[l].reshape(1, DV)
        woa = w_out[l][:D_A + D_B].astype(BF16)
        woc = w_out[l][D_A + D_B:].astype(BF16)
        g2 = norm2_g[l].reshape(1, d)
        wrt = _pad_lanes(jnp.concatenate(
            [w_group[l], w_router[l].transpose(1, 0, 2).reshape(d, N_EXPERTS)], axis=1))
        wg, wu, wd = w_gate[l].astype(BF16), w_up[l].astype(BF16), w_down[l].astype(BF16)
        last = l == depth - 1

        yab, q, k, v, gate, gb, st_a, st_b, st_q = _inproj_prompt(
            xp, n_b, t_len, g1, wmain, wsmall, caw, cbw, lng, lnb, ccw, alog, dtb)
        yc, s_fin = _delta_prompt(q, k, v, gb, gate, ong, n_b, t_len)
        x1, h2, comb = _outproj_router(xp, yab, yc, woa, woc, g2, wrt)
        xp = _moe(x1, h2, comb, wg, wu, wd, fg, last)
        outs["pa"].append(st_a); outs["pb"].append(st_b); outs["pq"].append(st_q); outs["pd"].append(s_fin)

        hist_a = state_conv_a[l].transpose(1, 0, 2)
        hist_b = state_conv_b[l].transpose(1, 0, 2)
        hist_q = state_conv_qkv[l].transpose(1, 0, 2)
        yab, q, k, v, gate, gb, un_a, un_b, un_q = _inproj_sample(
            xs, g1, wmain, wsmall, caw, cbw, lng, lnb, ccw, alog, dtb, hist_a, hist_b, hist_q)
        yc, s_new = _delta_sample(q, k, v, gb, gate, ong, state_delta[l])
        x1, h2, comb = _outproj_router(xs, yab, yc, woa, woc, g2, wrt)
        xs = _moe(x1, h2, comb, wg, wu, wd, fg, last)
        outs["sa"].append(jnp.concatenate([state_conv_a[l][:, 1:], un_a[:, None]], axis=1))
        outs["sb"].append(jnp.concatenate([state_conv_b[l][:, 1:], un_b[:, None]], axis=1))
        outs["sq"].append(jnp.concatenate([state_conv_qkv[l][:, 1:], un_q[:, None]], axis=1))
        outs["sd"].append(s_new)

    stack = lambda key: jnp.stack(outs[key])
    return (xp.reshape(n_b, t_len, d), xs.reshape(n_s, 1, d),
            stack("pa"), stack("pb"), stack("pq"), stack("pd"),
            stack("sa"), stack("sb"), stack("sq"), stack("sd"))
```

```python
import functools

import jax
import jax.numpy as jnp
from jax import lax
from jax.experimental import pallas as pl
from jax.experimental.pallas import tpu as pltpu

F32 = jnp.float32
BF16 = jnp.bfloat16
HIGHEST = lax.Precision.HIGHEST

EPS = 1e-6
H_C = 4
DK = 128
DV = 128
D_A = 256
D_B = 256
D_C = 512
W_A = 3
W_B = 31
W_C = 4
CHUNK = 64
N_GROUPS = 4
EXP_PER_GROUP = 8
N_EXPERTS = N_GROUPS * EXP_PER_GROUP
LANES = 128
PAD_A = 8
PAD_B = 32
PAD_C = 8
EXPERT_LANE0 = N_GROUPS
VMEM_LIMIT = 56 * 1024 * 1024


def _silu(x):
    return x * (1.0 / (1.0 + jnp.exp(-x)))


def _sigmoid(x):
    return 1.0 / (1.0 + jnp.exp(-x))


def _softplus(x):
    return jnp.maximum(x, 0.0) + jnp.log1p(jnp.exp(-jnp.abs(x)))


def _rms(x, g):
    return x * lax.rsqrt(jnp.mean(x * x, axis=-1, keepdims=True) + EPS) * g


def _conv_taps(buf, w_ref, width, pad, n_rows, n_cols, row_blk, col_blk):
    row_parts = []
    for r0 in range(0, n_rows, row_blk):
        col_parts = []
        for c0 in range(0, n_cols, col_blk):
            acc = None
            for j in range(width):
                start = pad - (width - 1) + j + r0
                term = buf[start:start + row_blk, c0:c0 + col_blk] * w_ref[j:j + 1, c0:c0 + col_blk]
                acc = term if acc is None else acc + term
            col_parts.append(acc)
        row_parts.append(col_parts[0] if len(col_parts) == 1 else jnp.concatenate(col_parts, axis=1))
    return row_parts[0] if len(row_parts) == 1 else jnp.concatenate(row_parts, axis=0)


def _l2norm_heads(x, scale):
    parts = []
    for h in range(H_C):
        xh = x[:, h * DK:(h + 1) * DK]
        parts.append(xh * (lax.rsqrt(jnp.sum(xh * xh, axis=-1, keepdims=True) + 1e-6) * scale))
    return jnp.concatenate(parts, axis=1)


def _in_projection(x, g1_ref, wmain_ref, wsmall_ref):
    h = _rms(x, g1_ref[...])
    hb = h.astype(BF16)
    z_a = jnp.dot(hb, wmain_ref[:, 0:768], preferred_element_type=F32)
    z_b = jnp.dot(hb, wmain_ref[:, 768:1280], preferred_element_type=F32)
    z_qkv = jnp.dot(hb, wmain_ref[:, 1280:2816], preferred_element_type=F32)
    z_gate = jnp.dot(hb, wmain_ref[:, 2816:3328], preferred_element_type=F32)
    z_s = jnp.dot(h, wsmall_ref[...], preferred_element_type=F32, precision=HIGHEST)
    return z_a, z_b, z_qkv, z_gate, z_s


def _layer_norm_silu(x, g, b):
    mu = jnp.mean(x, axis=-1, keepdims=True)
    xc = x - mu
    y = xc * lax.rsqrt(jnp.mean(xc * xc, axis=-1, keepdims=True) + EPS)
    return _silu(y * g + b)


def _decay_and_beta(z_s, alog_ref, dtb_ref):
    lane = lax.broadcasted_iota(jnp.int32, z_s.shape, 1)
    g = -jnp.exp(alog_ref[...]) * _softplus(z_s + dtb_ref[...])
    g = jnp.where(lane < H_C, g, 0.0)
    beta = jnp.where((lane >= H_C) & (lane < 2 * H_C), _sigmoid(z_s), 0.0)
    return g, beta


def _inproj_prompt_kernel(x_ref, g1_ref, wmain_ref, wsmall_ref, caw_ref, cbw_ref, lng_ref, lnb_ref,
                          ccw_ref, alog_ref, dtb_ref,
                          yab_ref, q_ref, k_ref, v_ref, gate_ref, gb_ref, sta_ref, stb_ref, stq_ref,
                          abuf, bbuf, cbuf, *, tt, nt):
    t = pl.program_id(1)
    z_a, z_b, z_qkv, z_gate, z_s = _in_projection(x_ref[...], g1_ref, wmain_ref, wsmall_ref)

    @pl.when(t == 0)
    def _():
        abuf[0:PAD_A, :] = jnp.zeros((PAD_A, D_A), F32)
        bbuf[0:PAD_B, :] = jnp.zeros((PAD_B, D_B), F32)
        cbuf[0:PAD_C, :] = jnp.zeros((PAD_C, 3 * D_C), F32)

    abuf[PAD_A:PAD_A + tt, :] = z_a[:, 256:512] * z_a[:, 512:768]
    conv_a = _conv_taps(abuf, caw_ref, W_A, PAD_A, tt, D_A, 64, 256)
    yab_ref[:, 0:D_A] = z_a[:, 0:256] * conv_a

    bbuf[PAD_B:PAD_B + tt, :] = z_b[:, 0:256] * _sigmoid(z_b[:, 256:512])
    conv_b = _conv_taps(bbuf, cbw_ref, W_B, PAD_B, tt, D_B, 64, 256)
    yab_ref[:, D_A:D_A + D_B] = _layer_norm_silu(conv_b, lng_ref[...], lnb_ref[...])

    cbuf[PAD_C:PAD_C + tt, :] = z_qkv
    qkv = _silu(_conv_taps(cbuf, ccw_ref, W_C, PAD_C, tt, 3 * D_C, 64, 512))
    q_ref[...] = _l2norm_heads(qkv[:, 0:D_C], DK ** -0.5)
    k_ref[...] = _l2norm_heads(qkv[:, D_C:2 * D_C], 1.0)
    v_ref[...] = qkv[:, 2 * D_C:3 * D_C]
    gate_ref[...] = z_gate

    g, beta = _decay_and_beta(z_s, alog_ref, dtb_ref)
    row_in_chunk = lax.broadcasted_iota(jnp.int32, g.shape, 0) & (CHUNK - 1)
    shift = 1
    while shift < CHUNK:
        g = g + jnp.where(row_in_chunk >= shift, pltpu.roll(g, shift, axis=0), 0.0)
        shift *= 2
    gb_ref[...] = g + beta

    @pl.when(t == nt - 1)
    def _():
        sta_ref[0] = abuf[PAD_A + tt - (W_A - 1):PAD_A + tt, :]
        stb_ref[0] = bbuf[PAD_B + tt - (W_B - 1):PAD_B + tt, :]
        stq_ref[0] = cbuf[PAD_C + tt - (W_C - 1):PAD_C + tt, :]

    abuf[0:PAD_A, :] = abuf[tt:tt + PAD_A, :]
    bbuf[0:PAD_B, :] = bbuf[tt:tt + PAD_B, :]
    cbuf[0:PAD_C, :] = cbuf[tt:tt + PAD_C, :]


def _const_spec(shape):
    nd = len(shape)
    return pl.BlockSpec(shape, lambda *_: (0,) * nd)


def _inproj_prompt(x2d, n_b, t_len, g1, wmain, wsmall, caw, cbw, lng, lnb, ccw, alog, dtb):
    tt = min(512, t_len)
    nt = t_len // tt
    n_tok = n_b * t_len
    d = x2d.shape[1]
    row = lambda w: pl.BlockSpec((tt, w), lambda b, t: (b * nt + t, 0))
    st = lambda r, w: pl.BlockSpec((1, r, w), lambda b, t: (b, 0, 0))
    out_shape = (
        jax.ShapeDtypeStruct((n_tok, D_A + D_B), F32),
        jax.ShapeDtypeStruct((n_tok, D_C), F32),
        jax.ShapeDtypeStruct((n_tok, D_C), F32),
        jax.ShapeDtypeStruct((n_tok, D_C), F32),
        jax.ShapeDtypeStruct((n_tok, D_C), F32),
        jax.ShapeDtypeStruct((n_tok, LANES), F32),
        jax.ShapeDtypeStruct((n_b, W_A - 1, D_A), F32),
        jax.ShapeDtypeStruct((n_b, W_B - 1, D_B), F32),
        jax.ShapeDtypeStruct((n_b, W_C - 1, 3 * D_C), F32),
    )
    return pl.pallas_call(
        functools.partial(_inproj_prompt_kernel, tt=tt, nt=nt),
        grid=(n_b, nt),
        in_specs=[row(d), _const_spec(g1.shape), _const_spec(wmain.shape), _const_spec(wsmall.shape),
                  _const_spec(caw.shape), _const_spec(cbw.shape), _const_spec(lng.shape), _const_spec(lnb.shape),
                  _const_spec(ccw.shape), _const_spec(alog.shape), _const_spec(dtb.shape)],
        out_specs=(row(D_A + D_B), row(D_C), row(D_C), row(D_C), row(D_C), row(LANES),
                   st(W_A - 1, D_A), st(W_B - 1, D_B), st(W_C - 1, 3 * D_C)),
        out_shape=out_shape,
        scratch_shapes=[pltpu.VMEM((PAD_A + tt, D_A), F32), pltpu.VMEM((PAD_B + tt, D_B), F32),
                        pltpu.VMEM((PAD_C + tt, 3 * D_C), F32)],
        compiler_params=pltpu.CompilerParams(dimension_semantics=("arbitrary", "arbitrary"),
                                             vmem_limit_bytes=VMEM_LIMIT),
        name="inproj_prompt",
    )(x2d, g1, wmain, wsmall, caw, cbw, lng, lnb, ccw, alog, dtb)


def _dot_nt(a, b):
    return lax.dot_general(a, b, (((1,), (1,)), ((), ())), preferred_element_type=F32)


def _dot_tn(a, b):
    return lax.dot_general(a, b, (((0,), (0,)), ((), ())), preferred_element_type=F32)


def _bdot(a, b):
    return jnp.dot(a.astype(BF16), b.astype(BF16), preferred_element_type=F32)


def _gated_norm(o, on_g, gate):
    return o * lax.rsqrt(jnp.mean(o * o, axis=-1, keepdims=True) + EPS) * on_g * _silu(gate)


def _delta_prompt_kernel(q_ref, k_ref, v_ref, gb_ref, gate_ref, ong_ref, yc_ref, sfin_ref, s_scr, *, tq, nt):
    t = pl.program_id(1)

    @pl.when(t == 0)
    def _():
        s_scr[...] = jnp.zeros(s_scr.shape, F32)

    c = CHUNK
    n_chunks = tq // c
    n_double = c.bit_length() - 2
    inst = [(ch, h) for ch in range(n_chunks) for h in range(H_C)]
    rows = lambda ch: slice(ch * c, (ch + 1) * c)
    cols = lambda h: slice(h * DK, (h + 1) * DK)
    ri = lax.broadcasted_iota(jnp.int32, (c, c), 0)
    ci = lax.broadcasted_iota(jnp.int32, (c, c), 1)
    gb = gb_ref[...]

    q = [q_ref[rows(ch), cols(h)] for ch, h in inst]
    k = [k_ref[rows(ch), cols(h)] for ch, h in inst]
    v = [v_ref[rows(ch), cols(h)] for ch, h in inst]
    g_col = [gb[rows(ch), h:h + 1] for ch, h in inst]
    b_col = [gb[rows(ch), H_C + h:H_C + h + 1] for ch, h in inst]
    g_row = [jnp.sum(jnp.where(ri == ci, g, 0.0), axis=0, keepdims=True) for g in g_col]
    diff = [gc - gr for gc, gr in zip(g_col, g_row)]
    dec_strict = [jnp.exp(jnp.where(ri > ci, d, -jnp.inf)) for d in diff]
    dec_incl = [jnp.exp(jnp.where(ri >= ci, d, -jnp.inf)) for d in diff]
    e_g = [jnp.exp(g) for g in g_col]
    g_last = [g[c - 1:c, :] for g in g_col]

    qkk = [_dot_nt(jnp.concatenate([qi, ki], axis=0).astype(BF16), ki.astype(BF16)) for qi, ki in zip(q, k)]
    qk = [m[:c] * d for m, d in zip(qkk, dec_incl)]
    p = [-(b * m[c:] * d) for b, m, d in zip(b_col, qkk, dec_strict)]
    sol = [jnp.concatenate([vi * b, ki * (b * eg)], axis=1) for vi, ki, b, eg in zip(v, k, b_col, e_g)]
    for level in range(n_double + 1):
        sol = [s + _bdot(pi, s) for pi, s in zip(p, sol)]
        if level < n_double:
            p = [_bdot(pi, pi) for pi in p]
    kd = [ki * jnp.exp(gl - g) for ki, gl, g in zip(k, g_last, g_col)]
    qk_uw = [_bdot(m, s) for m, s in zip(qk, sol)]
    kd_uw = [_dot_tn(x.astype(BF16), s.astype(BF16)) for x, s in zip(kd, sol)]
    lhs = [jnp.concatenate([qi * eg - a[:, DV:], -b[:, DV:]], axis=0).astype(BF16)
           for qi, eg, a, b in zip(q, e_g, qk_uw, kd_uw)]
    decay = [jnp.exp(gl) for gl in g_last]

    state = [s_scr[h] for h in range(H_C)]
    for ch in range(n_chunks):
        base = ch * H_C
        r = [jnp.dot(lhs[base + h], state[h].astype(BF16), preferred_element_type=F32) for h in range(H_C)]
        for h in range(H_C):
            o = r[h][:c] + qk_uw[base + h][:, :DV]
            yc_ref[rows(ch), cols(h)] = _gated_norm(o, ong_ref[...], gate_ref[rows(ch), cols(h)])
        state = [decay[base + h] * state[h] + r[h][c:] + kd_uw[base + h][:, :DV] for h in range(H_C)]
    for h in range(H_C):
        s_scr[h] = state[h]

    @pl.when(t == nt - 1)
    def _():
        sfin_ref[0] = s_scr[...]


def _delta_prompt(q, k, v, gb, gate, ong, n_b, t_len):
    tq = min(256, t_len)
    nt = t_len // tq
    row = lambda w: pl.BlockSpec((tq, w), lambda b, t: (b * nt + t, 0))
    return pl.pallas_call(
        functools.partial(_delta_prompt_kernel, tq=tq, nt=nt),
        grid=(n_b, nt),
        in_specs=[row(D_C), row(D_C), row(D_C), row(LANES), row(D_C), _const_spec(ong.shape)],
        out_specs=(row(D_C), pl.BlockSpec((1, H_C, DK, DV), lambda b, t: (b, 0, 0, 0))),
        out_shape=(jax.ShapeDtypeStruct((n_b * t_len, D_C), F32),
                   jax.ShapeDtypeStruct((n_b, H_C, DK, DV), F32)),
        scratch_shapes=[pltpu.VMEM((H_C, DK, DV), F32)],
        compiler_params=pltpu.CompilerParams(dimension_semantics=("arbitrary", "arbitrary"),
                                             vmem_limit_bytes=VMEM_LIMIT),
        name="delta_prompt",
    )(q, k, v, gb, gate, ong)


def _inproj_sample_kernel(x_ref, g1_ref, wmain_ref, wsmall_ref, caw_ref, cbw_ref, lng_ref, lnb_ref,
                          ccw_ref, alog_ref, dtb_ref, sta_ref, stb_ref, stq_ref,
                          yab_ref, q_ref, k_ref, v_ref, gate_ref, gb_ref, una_ref, unb_ref, unq_ref):
    z_a, z_b, z_qkv, z_gate, z_s = _in_projection(x_ref[...], g1_ref, wmain_ref, wsmall_ref)

    def conv_step(state_ref, new, w_ref, width):
        acc = new * w_ref[width - 1:width, :]
        for j in range(width - 1):
            acc = acc + state_ref[j] * w_ref[j:j + 1, :]
        return acc

    u_a = z_a[:, 256:512] * z_a[:, 512:768]
    una_ref[...] = u_a
    yab_ref[:, 0:D_A] = z_a[:, 0:256] * conv_step(sta_ref, u_a, caw_ref, W_A)

    u_b = z_b[:, 0:256] * _sigmoid(z_b[:, 256:512])
    unb_ref[...] = u_b
    yab_ref[:, D_A:D_A + D_B] = _layer_norm_silu(conv_step(stb_ref, u_b, cbw_ref, W_B), lng_ref[...], lnb_ref[...])

    unq_ref[...] = z_qkv
    qkv = _silu(conv_step(stq_ref, z_qkv, ccw_ref, W_C))
    q_ref[...] = _l2norm_heads(qkv[:, 0:D_C], DK ** -0.5)
    k_ref[...] = _l2norm_heads(qkv[:, D_C:2 * D_C], 1.0)
    v_ref[...] = qkv[:, 2 * D_C:3 * D_C]
    gate_ref[...] = z_gate
    g, beta = _decay_and_beta(z_s, alog_ref, dtb_ref)
    gb_ref[...] = g + beta


def _inproj_sample(x2d, g1, wmain, wsmall, caw, cbw, lng, lnb, ccw, alog, dtb, st_a, st_b, st_q):
    n = x2d.shape[0]
    args = (x2d, g1, wmain, wsmall, caw, cbw, lng, lnb, ccw, alog, dtb, st_a, st_b, st_q)
    out_widths = (D_A + D_B, D_C, D_C, D_C, D_C, LANES, D_A, D_B, 3 * D_C)
    return pl.pallas_call(
        _inproj_sample_kernel,
        grid=(1,),
        in_specs=[_const_spec(a.shape) for a in args],
        out_specs=tuple(_const_spec((n, w)) for w in out_widths),
        out_shape=tuple(jax.ShapeDtypeStruct((n, w), F32) for w in out_widths),
        compiler_params=pltpu.CompilerParams(dimension_semantics=("arbitrary",), vmem_limit_bytes=VMEM_LIMIT),
        name="inproj_sample",
    )(*args)


def _delta_sample_kernel(q_ref, k_ref, v_ref, gb_ref, gate_ref, ong_ref, s_ref, yc_ref, snew_ref, *, nb):
    q = q_ref[...]
    k = k_ref[...]
    rows = [k[:, h * DK:(h + 1) * DK] for h in range(H_C)] + [q[:, h * DK:(h + 1) * DK] for h in range(H_C)]
    rows.append(jnp.zeros((LANES - 2 * H_C * nb, DK), F32))
    kq_t = jnp.concatenate(rows, axis=0).T
    gb = gb_ref[...]
    for i in range(nb):
        for h in range(H_C):
            cols = slice(h * DK, (h + 1) * DK)
            s = s_ref[i, h]
            k_col = kq_t[:, h * nb + i:h * nb + i + 1]
            q_col = kq_t[:, (H_C + h) * nb + i:(H_C + h) * nb + i + 1]
            k_s = jnp.sum(s * k_col, axis=0, keepdims=True)
            q_s = jnp.sum(s * q_col, axis=0, keepdims=True)
            e_g = jnp.exp(gb[i:i + 1, h:h + 1])
            beta = gb[i:i + 1, H_C + h:H_C + h + 1]
            v_new = beta * (v_ref[i:i + 1, cols] - e_g * k_s)
            qk = jnp.sum(q[i:i + 1, cols] * k[i:i + 1, cols], axis=-1, keepdims=True)
            o = e_g * q_s + qk * v_new
            snew_ref[i, h] = e_g * s + k_col * v_new
            yc_ref[i:i + 1, cols] = _gated_norm(o, ong_ref[...], gate_ref[i:i + 1, cols])


def _delta_sample(q, k, v, gb, gate, ong, state):
    n = q.shape[0]
    nb = 8
    row = lambda w: pl.BlockSpec((nb, w), lambda i: (i, 0))
    st = pl.BlockSpec((nb, H_C, DK, DV), lambda i: (i, 0, 0, 0))
    return pl.pallas_call(
        functools.partial(_delta_sample_kernel, nb=nb),
        grid=(n // nb,),
        in_specs=[row(D_C), row(D_C), row(D_C), row(LANES), row(D_C), _const_spec(ong.shape), st],
        out_specs=(row(D_C), st),
        out_shape=(jax.ShapeDtypeStruct((n, D_C), F32), jax.ShapeDtypeStruct(state.shape, F32)),
        compiler_params=pltpu.CompilerParams(dimension_semantics=("arbitrary",), vmem_limit_bytes=VMEM_LIMIT),
        name="delta_sample",
    )(q, k, v, gb, gate, ong, state)


def _outproj_router_kernel(x_ref, yab_ref, yc_ref, woa_ref, woc_ref, g2_ref, wrt_ref, x1_ref, h2_ref, comb_ref):
    x1 = (x_ref[...]
          + jnp.dot(yab_ref[...].astype(BF16), woa_ref[...], preferred_element_type=F32)
          + jnp.dot(yc_ref[...].astype(BF16), woc_ref[...], preferred_element_type=F32))
    x1_ref[...] = x1
    h2 = _rms(x1, g2_ref[...])
    h2_ref[...] = h2.astype(BF16)
    logits = jnp.dot(h2, wrt_ref[...], preferred_element_type=F32, precision=HIGHEST)
    lane = lax.broadcasted_iota(jnp.int32, logits.shape, 1)
    neg = -jnp.inf
    gl = jnp.where(lane < N_GROUPS, logits, neg)
    g_max = jnp.max(gl, axis=-1, keepdims=True)
    g_idx = jnp.min(jnp.where(gl == g_max, lane, LANES), axis=-1, keepdims=True)
    g_p = 1.0 / jnp.sum(jnp.exp(gl - g_max), axis=-1, keepdims=True)
    lo = EXPERT_LANE0 + g_idx * EXP_PER_GROUP
    in_group = (lane >= lo) & (lane < lo + EXP_PER_GROUP)
    el = jnp.where(in_group, logits, neg)
    e_max = jnp.max(el, axis=-1, keepdims=True)
    pe = jnp.exp(el - e_max)
    e_prob = pe / jnp.sum(pe, axis=-1, keepdims=True)
    p1 = jnp.max(e_prob, axis=-1, keepdims=True)
    i1 = jnp.min(jnp.where(e_prob == p1, lane, LANES), axis=-1, keepdims=True)
    rest = jnp.where(in_group & (lane != i1), e_prob, -1.0)
    p2 = jnp.max(rest, axis=-1, keepdims=True)
    i2 = jnp.min(jnp.where(rest == p2, lane, LANES), axis=-1, keepdims=True)
    denom = p1 + p2
    comb_ref[...] = (jnp.where(lane == i1, g_p * (p1 / denom), 0.0)
                     + jnp.where(lane == i2, g_p * (p2 / denom), 0.0))


def _outproj_router(x2d, yab, yc, woa, woc, g2, wrt):
    n, d = x2d.shape
    tm = min(512, n)
    row = lambda w: pl.BlockSpec((tm, w), lambda i: (i, 0))
    return pl.pallas_call(
        _outproj_router_kernel,
        grid=(n // tm,),
        in_specs=[row(d), row(D_A + D_B), row(D_C), _const_spec(woa.shape), _const_spec(woc.shape),
                  _const_spec(g2.shape), _const_spec(wrt.shape)],
        out_specs=(row(d), row(d), row(LANES)),
        out_shape=(jax.ShapeDtypeStruct((n, d), F32), jax.ShapeDtypeStruct((n, d), BF16),
                   jax.ShapeDtypeStruct((n, LANES), F32)),
        compiler_params=pltpu.CompilerParams(dimension_semantics=("arbitrary",), vmem_limit_bytes=VMEM_LIMIT),
        name="outproj_router",
    )(x2d, yab, yc, woa, woc, g2, wrt)


def _moe_kernel(x1_ref, h2_ref, comb_ref, wg_ref, wu_ref, wd_ref, fg_ref, out_ref, *, final_norm):
    e = pl.program_id(1)

    @pl.when(e == 0)
    def _():
        out_ref[...] = x1_ref[...]

    h2 = h2_ref[...]
    gate = jnp.dot(h2, wg_ref[0], preferred_element_type=F32)
    up = jnp.dot(h2, wu_ref[0], preferred_element_type=F32)
    comb = comb_ref[...]
    lane = lax.broadcasted_iota(jnp.int32, comb.shape, 1)
    c_col = jnp.sum(jnp.where(lane == EXPERT_LANE0 + e, comb, 0.0), axis=-1, keepdims=True)
    act = (_silu(gate) * up * c_col).astype(BF16)
    out_ref[...] += jnp.dot(act, wd_ref[0], preferred_element_type=F32)

    if final_norm:
        @pl.when(e == pl.num_programs(1) - 1)
        def _():
            out_ref[...] = _rms(out_ref[...], fg_ref[...])


def _moe(x1, h2, comb, wg, wu, wd, fg, final_norm):
    n, d = x1.shape
    tm = min(1024, n)
    n_e, _, d_e = wg.shape
    row = lambda w: pl.BlockSpec((tm, w), lambda i, e: (i, 0))
    return pl.pallas_call(
        functools.partial(_moe_kernel, final_norm=final_norm),
        grid=(n // tm, n_e),
        in_specs=[row(d), row(d), row(LANES),
                  pl.BlockSpec((1, d, d_e), lambda i, e: (e, 0, 0)),
                  pl.BlockSpec((1, d, d_e), lambda i, e: (e, 0, 0)),
                  pl.BlockSpec((1, d_e, d), lambda i, e: (e, 0, 0)),
                  pl.BlockSpec(fg.shape, lambda i, e: (0, 0))],
        out_specs=row(d),
        out_shape=jax.ShapeDtypeStruct((n, d), F32),
        compiler_params=pltpu.CompilerParams(dimension_semantics=("arbitrary", "arbitrary"),
                                             vmem_limit_bytes=VMEM_LIMIT),
        name="moe",
    )(x1, h2, comb, wg, wu, wd, fg)


def _pad_lanes(a, lane0=0):
    a = a.reshape((1, -1)) if a.ndim == 1 else a
    return jnp.pad(a, ((0, 0), (lane0, LANES - lane0 - a.shape[1])))


def kernel(x_prompt, x_sample, state_conv_a, state_conv_b, state_conv_qkv, state_delta, norm1_g, w_in, conv_a_w,
           conv_b_w, ln_b_g, ln_b_b, conv_c_w, a_log, dt_bias, o_norm_g, w_out, norm2_g, w_group, w_router,
           w_gate, w_up, w_down, final_g):
    n_b, t_len, d = x_prompt.shape
    n_s = x_sample.shape[0]
    depth = w_in.shape[0]
    n_main = 3 * D_A + 2 * D_B + 3 * D_C + D_C
    xp = x_prompt.reshape(n_b * t_len, d)
    xs = x_sample.reshape(n_s, d)
    fg = final_g.reshape(1, d)
    outs = {k: [] for k in ("pa", "pb", "pq", "pd", "sa", "sb", "sq", "sd")}
    for l in range(depth):
        g1 = norm1_g[l].reshape(1, d)
        wmain = w_in[l][:, :n_main].astype(BF16)
        wsmall = _pad_lanes(jnp.concatenate([w_in[l][:, n_main + H_C:], w_in[l][:, n_main:n_main + H_C]], axis=1))
        caw, cbw, ccw = conv_a_w[l], conv_b_w[l], conv_c_w[l]
        lng, lnb = ln_b_g[l].reshape(1, D_B), ln_b_b[l].reshape(1, D_B)
        alog, dtb = _pad_lanes(a_log[l]), _pad_lanes(dt_bias[l])
        ong = o_norm_g[l].reshape(1, DV)
        woa = w_out[l][:D_A + D_B].astype(BF16)
        woc = w_out[l][D_A + D_B:].astype(BF16)
        g2 = norm2_g[l].reshape(1, d)
        wrt = _pad_lanes(jnp.concatenate(
            [w_group[l], w_router[l].transpose(1, 0, 2).reshape(d, N_EXPERTS)], axis=1))
        wg, wu, wd = w_gate[l].astype(BF16), w_up[l].astype(BF16), w_down[l].astype(BF16)
        last = l == depth - 1

        yab, q, k, v, gate, gb, st_a, st_b, st_q = _inproj_prompt(
            xp, n_b, t_len, g1, wmain, wsmall, caw, cbw, lng, lnb, ccw, alog, dtb)
        yc, s_fin = _delta_prompt(q, k, v, gb, gate, ong, n_b, t_len)
        x1, h2, comb = _outproj_router(xp, yab, yc, woa, woc, g2, wrt)
        xp = _moe(x1, h2, comb, wg, wu, wd, fg, last)
        outs["pa"].append(st_a); outs["pb"].append(st_b); outs["pq"].append(st_q); outs["pd"].append(s_fin)

        hist_a = state_conv_a[l].transpose(1, 0, 2)
        hist_b = state_conv_b[l].transpose(1, 0, 2)
        hist_q = state_conv_qkv[l].transpose(1, 0, 2)
        yab, q, k, v, gate, gb, un_a, un_b, un_q = _inproj_sample(
            xs, g1, wmain, wsmall, caw, cbw, lng, lnb, ccw, alog, dtb, hist_a, hist_b, hist_q)
        yc, s_new = _delta_sample(q, k, v, gb, gate, ong, state_delta[l])
        x1, h2, comb = _outproj_router(xs, yab, yc, woa, woc, g2, wrt)
        xs = _moe(x1, h2, comb, wg, wu, wd, fg, last)
        outs["sa"].append(jnp.concatenate([state_conv_a[l][:, 1:], un_a[:, None]], axis=1))
        outs["sb"].append(jnp.concatenate([state_conv_b[l][:, 1:], un_b[:, None]], axis=1))
        outs["sq"].append(jnp.concatenate([state_conv_qkv[l][:, 1:], un_q[:, None]], axis=1))
        outs["sd"].append(s_new)

    stack = lambda key: jnp.stack(outs[key])
    return (xp.reshape(n_b, t_len, d), xs.reshape(n_s, 1, d),
            stack("pa"), stack("pb"), stack("pq"), stack("pd"),
            stack("sa"), stack("sb"), stack("sq"), stack("sd"))
```

```python
import functools

import jax
import jax.numpy as jnp
from jax import lax
from jax.experimental import pallas as pl
from jax.experimental.pallas import tpu as pltpu

F32 = jnp.float32
BF16 = jnp.bfloat16
HIGHEST = lax.Precision.HIGHEST

EPS = 1e-6
H_C = 4
DK = 128
DV = 128
D_A = 256
D_B = 256
D_C = 512
W_A = 3
W_B = 31
W_C = 4
CHUNK = 64
N_GROUPS = 4
EXP_PER_GROUP = 8
N_EXPERTS = N_GROUPS * EXP_PER_GROUP
LANES = 128
PAD_A = 8
PAD_B = 32
PAD_C = 8
EXPERT_LANE0 = N_GROUPS
VMEM_LIMIT = 56 * 1024 * 1024
TOKEN_TILE = 512
PIECE = 16
ROW_TILE = 256


def _silu(x):
    return x * (1.0 / (1.0 + jnp.exp(-x)))


def _sigmoid(x):
    return 1.0 / (1.0 + jnp.exp(-x))


def _softplus(x):
    return jnp.maximum(x, 0.0) + jnp.log1p(jnp.exp(-jnp.abs(x)))


def _rms(x, g):
    return x * lax.rsqrt(jnp.mean(x * x, axis=-1, keepdims=True) + EPS) * g


def _conv_taps(buf, w_ref, width, pad, n_rows, n_cols, row_blk, col_blk):
    row_parts = []
    for r0 in range(0, n_rows, row_blk):
        col_parts = []
        for c0 in range(0, n_cols, col_blk):
            acc = None
            for j in range(width):
                start = pad - (width - 1) + j + r0
                term = buf[start:start + row_blk, c0:c0 + col_blk] * w_ref[j:j + 1, c0:c0 + col_blk]
                acc = term if acc is None else acc + term
            col_parts.append(acc)
        row_parts.append(col_parts[0] if len(col_parts) == 1 else jnp.concatenate(col_parts, axis=1))
    return row_parts[0] if len(row_parts) == 1 else jnp.concatenate(row_parts, axis=0)


def _l2norm_heads(x, scale):
    parts = []
    for h in range(H_C):
        xh = x[:, h * DK:(h + 1) * DK]
        parts.append(xh * (lax.rsqrt(jnp.sum(xh * xh, axis=-1, keepdims=True) + 1e-6) * scale))
    return jnp.concatenate(parts, axis=1)


def _in_projection(x, g1_ref, wmain_ref, wsmall_ref):
    h = _rms(x, g1_ref[...])
    hb = h.astype(BF16)
    z_a = jnp.dot(hb, wmain_ref[:, 0:768], preferred_element_type=F32)
    z_b = jnp.dot(hb, wmain_ref[:, 768:1280], preferred_element_type=F32)
    z_qkv = jnp.dot(hb, wmain_ref[:, 1280:2816], preferred_element_type=F32)
    z_gate = jnp.dot(hb, wmain_ref[:, 2816:3328], preferred_element_type=F32)
    z_s = jnp.dot(h, wsmall_ref[...], preferred_element_type=F32, precision=HIGHEST)
    return z_a, z_b, z_qkv, z_gate, z_s


def _layer_norm_silu(x, g, b):
    mu = jnp.mean(x, axis=-1, keepdims=True)
    xc = x - mu
    y = xc * lax.rsqrt(jnp.mean(xc * xc, axis=-1, keepdims=True) + EPS)
    return _silu(y * g + b)


def _decay_and_beta(z_s, alog_ref, dtb_ref):
    lane = lax.broadcasted_iota(jnp.int32, z_s.shape, 1)
    g = -jnp.exp(alog_ref[...]) * _softplus(z_s + dtb_ref[...])
    g = jnp.where(lane < H_C, g, 0.0)
    beta = jnp.where((lane >= H_C) & (lane < 2 * H_C), _sigmoid(z_s), 0.0)
    return g, beta


def _inproj_prompt_kernel(x_ref, g1_ref, wmain_ref, wsmall_ref, caw_ref, cbw_ref, lng_ref, lnb_ref,
                          ccw_ref, alog_ref, dtb_ref,
                          yab_ref, q_ref, k_ref, v_ref, gate_ref, gb_ref, sta_ref, stb_ref, stq_ref,
                          abuf, bbuf, cbuf, *, tt, nt):
    t = pl.program_id(1)
    z_a, z_b, z_qkv, z_gate, z_s = _in_projection(x_ref[...], g1_ref, wmain_ref, wsmall_ref)

    @pl.when(t == 0)
    def _():
        abuf[0:PAD_A, :] = jnp.zeros((PAD_A, D_A), F32)
        bbuf[0:PAD_B, :] = jnp.zeros((PAD_B, D_B), F32)
        cbuf[0:PAD_C, :] = jnp.zeros((PAD_C, 3 * D_C), F32)

    abuf[PAD_A:PAD_A + tt, :] = z_a[:, 256:512] * z_a[:, 512:768]
    conv_a = _conv_taps(abuf, caw_ref, W_A, PAD_A, tt, D_A, 64, 256)
    yab_ref[:, 0:D_A] = z_a[:, 0:256] * conv_a

    bbuf[PAD_B:PAD_B + tt, :] = z_b[:, 0:256] * _sigmoid(z_b[:, 256:512])
    conv_b = _conv_taps(bbuf, cbw_ref, W_B, PAD_B, tt, D_B, 64, 256)
    yab_ref[:, D_A:D_A + D_B] = _layer_norm_silu(conv_b, lng_ref[...], lnb_ref[...])

    cbuf[PAD_C:PAD_C + tt, :] = z_qkv
    qkv = _silu(_conv_taps(cbuf, ccw_ref, W_C, PAD_C, tt, 3 * D_C, 64, 512))
    q_ref[...] = _l2norm_heads(qkv[:, 0:D_C], DK ** -0.5)
    k_ref[...] = _l2norm_heads(qkv[:, D_C:2 * D_C], 1.0)
    v_ref[...] = qkv[:, 2 * D_C:3 * D_C]
    gate_ref[...] = z_gate

    g, beta = _decay_and_beta(z_s, alog_ref, dtb_ref)
    row_in_chunk = lax.broadcasted_iota(jnp.int32, g.shape, 0) & (CHUNK - 1)
    shift = 1
    while shift < CHUNK:
        g = g + jnp.where(row_in_chunk >= shift, pltpu.roll(g, shift, axis=0), 0.0)
        shift *= 2
    gb_ref[...] = g + beta

    @pl.when(t == nt - 1)
    def _():
        sta_ref[0] = abuf[PAD_A + tt - (W_A - 1):PAD_A + tt, :]
        stb_ref[0] = bbuf[PAD_B + tt - (W_B - 1):PAD_B + tt, :]
        stq_ref[0] = cbuf[PAD_C + tt - (W_C - 1):PAD_C + tt, :]

    abuf[0:PAD_A, :] = abuf[tt:tt + PAD_A, :]
    bbuf[0:PAD_B, :] = bbuf[tt:tt + PAD_B, :]
    cbuf[0:PAD_C, :] = cbuf[tt:tt + PAD_C, :]


def _const_spec(shape):
    nd = len(shape)
    return pl.BlockSpec(shape, lambda *_: (0,) * nd)


def _inproj_prompt(x2d, n_b, t_len, g1, wmain, wsmall, caw, cbw, lng, lnb, ccw, alog, dtb):
    tt = min(512, t_len)
    nt = t_len // tt
    n_tok = n_b * t_len
    d = x2d.shape[1]
    row = lambda w: pl.BlockSpec((tt, w), lambda b, t: (b * nt + t, 0))
    st = lambda r, w: pl.BlockSpec((1, r, w), lambda b, t: (b, 0, 0))
    out_shape = (
        jax.ShapeDtypeStruct((n_tok, D_A + D_B), F32),
        jax.ShapeDtypeStruct((n_tok, D_C), F32),
        jax.ShapeDtypeStruct((n_tok, D_C), F32),
        jax.ShapeDtypeStruct((n_tok, D_C), F32),
        jax.ShapeDtypeStruct((n_tok, D_C), F32),
        jax.ShapeDtypeStruct((n_tok, LANES), F32),
        jax.ShapeDtypeStruct((n_b, W_A - 1, D_A), F32),
        jax.ShapeDtypeStruct((n_b, W_B - 1, D_B), F32),
        jax.ShapeDtypeStruct((n_b, W_C - 1, 3 * D_C), F32),
    )
    return pl.pallas_call(
        functools.partial(_inproj_prompt_kernel, tt=tt, nt=nt),
        grid=(n_b, nt),
        in_specs=[row(d), _const_spec(g1.shape), _const_spec(wmain.shape), _const_spec(wsmall.shape),
                  _const_spec(caw.shape), _const_spec(cbw.shape), _const_spec(lng.shape), _const_spec(lnb.shape),
                  _const_spec(ccw.shape), _const_spec(alog.shape), _const_spec(dtb.shape)],
        out_specs=(row(D_A + D_B), row(D_C), row(D_C), row(D_C), row(D_C), row(LANES),
                   st(W_A - 1, D_A), st(W_B - 1, D_B), st(W_C - 1, 3 * D_C)),
        out_shape=out_shape,
        scratch_shapes=[pltpu.VMEM((PAD_A + tt, D_A), F32), pltpu.VMEM((PAD_B + tt, D_B), F32),
                        pltpu.VMEM((PAD_C + tt, 3 * D_C), F32)],
        compiler_params=pltpu.CompilerParams(dimension_semantics=("arbitrary", "arbitrary"),
                                             vmem_limit_bytes=VMEM_LIMIT),
        name="inproj_prompt",
    )(x2d, g1, wmain, wsmall, caw, cbw, lng, lnb, ccw, alog, dtb)


def _dot_nt(a, b):
    return lax.dot_general(a, b, (((1,), (1,)), ((), ())), preferred_element_type=F32)


def _dot_tn(a, b):
    return lax.dot_general(a, b, (((0,), (0,)), ((), ())), preferred_element_type=F32)


def _bdot(a, b):
    return jnp.dot(a.astype(BF16), b.astype(BF16), preferred_element_type=F32)


def _gated_norm(o, on_g, gate):
    return o * lax.rsqrt(jnp.mean(o * o, axis=-1, keepdims=True) + EPS) * on_g * _silu(gate)


def _delta_prompt_kernel(q_ref, k_ref, v_ref, gb_ref, gate_ref, ong_ref, yc_ref, sfin_ref, s_scr, *, tq, nt):
    t = pl.program_id(1)

    @pl.when(t == 0)
    def _():
        s_scr[...] = jnp.zeros(s_scr.shape, F32)

    c = CHUNK
    n_chunks = tq // c
    n_double = c.bit_length() - 2
    inst = [(ch, h) for ch in range(n_chunks) for h in range(H_C)]
    rows = lambda ch: slice(ch * c, (ch + 1) * c)
    cols = lambda h: slice(h * DK, (h + 1) * DK)
    ri = lax.broadcasted_iota(jnp.int32, (c, c), 0)
    ci = lax.broadcasted_iota(jnp.int32, (c, c), 1)
    gb = gb_ref[...]

    q = [q_ref[rows(ch), cols(h)] for ch, h in inst]
    k = [k_ref[rows(ch), cols(h)] for ch, h in inst]
    v = [v_ref[rows(ch), cols(h)] for ch, h in inst]
    g_col = [gb[rows(ch), h:h + 1] for ch, h in inst]
    b_col = [gb[rows(ch), H_C + h:H_C + h + 1] for ch, h in inst]
    g_row = [jnp.sum(jnp.where(ri == ci, g, 0.0), axis=0, keepdims=True) for g in g_col]
    diff = [gc - gr for gc, gr in zip(g_col, g_row)]
    dec_strict = [jnp.exp(jnp.where(ri > ci, d, -jnp.inf)) for d in diff]
    dec_incl = [jnp.exp(jnp.where(ri >= ci, d, -jnp.inf)) for d in diff]
    e_g = [jnp.exp(g) for g in g_col]
    g_last = [g[c - 1:c, :] for g in g_col]

    qkk = [_dot_nt(jnp.concatenate([qi, ki], axis=0).astype(BF16), ki.astype(BF16)) for qi, ki in zip(q, k)]
    qk = [m[:c] * d for m, d in zip(qkk, dec_incl)]
    p = [-(b * m[c:] * d) for b, m, d in zip(b_col, qkk, dec_strict)]
    sol = [jnp.concatenate([vi * b, ki * (b * eg)], axis=1) for vi, ki, b, eg in zip(v, k, b_col, e_g)]
    for level in range(n_double + 1):
        sol = [s + _bdot(pi, s) for pi, s in zip(p, sol)]
        if level < n_double:
            p = [_bdot(pi, pi) for pi in p]
    kd = [ki * jnp.exp(gl - g) for ki, gl, g in zip(k, g_last, g_col)]
    qk_uw = [_bdot(m, s) for m, s in zip(qk, sol)]
    kd_uw = [_dot_tn(x.astype(BF16), s.astype(BF16)) for x, s in zip(kd, sol)]
    lhs = [jnp.concatenate([qi * eg - a[:, DV:], -b[:, DV:]], axis=0).astype(BF16)
           for qi, eg, a, b in zip(q, e_g, qk_uw, kd_uw)]
    decay = [jnp.exp(gl) for gl in g_last]

    state = [s_scr[h] for h in range(H_C)]
    for ch in range(n_chunks):
        base = ch * H_C
        r = [jnp.dot(lhs[base + h], state[h].astype(BF16), preferred_element_type=F32) for h in range(H_C)]
        for h in range(H_C):
            o = r[h][:c] + qk_uw[base + h][:, :DV]
            yc_ref[rows(ch), cols(h)] = _gated_norm(o, ong_ref[...], gate_ref[rows(ch), cols(h)])
        state = [decay[base + h] * state[h] + r[h][c:] + kd_uw[base + h][:, :DV] for h in range(H_C)]
    for h in range(H_C):
        s_scr[h] = state[h]

    @pl.when(t == nt - 1)
    def _():
        sfin_ref[0] = s_scr[...]


def _delta_prompt(q, k, v, gb, gate, ong, n_b, t_len):
    tq = min(256, t_len)
    nt = t_len // tq
    row = lambda w: pl.BlockSpec((tq, w), lambda b, t: (b * nt + t, 0))
    return pl.pallas_call(
        functools.partial(_delta_prompt_kernel, tq=tq, nt=nt),
        grid=(n_b, nt),
        in_specs=[row(D_C), row(D_C), row(D_C), row(LANES), row(D_C), _const_spec(ong.shape)],
        out_specs=(row(D_C), pl.BlockSpec((1, H_C, DK, DV), lambda b, t: (b, 0, 0, 0))),
        out_shape=(jax.ShapeDtypeStruct((n_b * t_len, D_C), F32),
                   jax.ShapeDtypeStruct((n_b, H_C, DK, DV), F32)),
        scratch_shapes=[pltpu.VMEM((H_C, DK, DV), F32)],
        compiler_params=pltpu.CompilerParams(dimension_semantics=("arbitrary", "arbitrary"),
                                             vmem_limit_bytes=VMEM_LIMIT),
        name="delta_prompt",
    )(q, k, v, gb, gate, ong)


def _inproj_sample_kernel(x_ref, g1_ref, wmain_ref, wsmall_ref, caw_ref, cbw_ref, lng_ref, lnb_ref,
                          ccw_ref, alog_ref, dtb_ref, sta_ref, stb_ref, stq_ref,
                          yab_ref, q_ref, k_ref, v_ref, gate_ref, gb_ref, una_ref, unb_ref, unq_ref):
    z_a, z_b, z_qkv, z_gate, z_s = _in_projection(x_ref[...], g1_ref, wmain_ref, wsmall_ref)

    def conv_step(state_ref, new, w_ref, width):
        acc = new * w_ref[width - 1:width, :]
        for j in range(width - 1):
            acc = acc + state_ref[j] * w_ref[j:j + 1, :]
        return acc

    u_a = z_a[:, 256:512] * z_a[:, 512:768]
    una_ref[...] = u_a
    yab_ref[:, 0:D_A] = z_a[:, 0:256] * conv_step(sta_ref, u_a, caw_ref, W_A)

    u_b = z_b[:, 0:256] * _sigmoid(z_b[:, 256:512])
    unb_ref[...] = u_b
    yab_ref[:, D_A:D_A + D_B] = _layer_norm_silu(conv_step(stb_ref, u_b, cbw_ref, W_B), lng_ref[...], lnb_ref[...])

    unq_ref[...] = z_qkv
    qkv = _silu(conv_step(stq_ref, z_qkv, ccw_ref, W_C))
    q_ref[...] = _l2norm_heads(qkv[:, 0:D_C], DK ** -0.5)
    k_ref[...] = _l2norm_heads(qkv[:, D_C:2 * D_C], 1.0)
    v_ref[...] = qkv[:, 2 * D_C:3 * D_C]
    gate_ref[...] = z_gate
    g, beta = _decay_and_beta(z_s, alog_ref, dtb_ref)
    gb_ref[...] = g + beta


def _inproj_sample(x2d, g1, wmain, wsmall, caw, cbw, lng, lnb, ccw, alog, dtb, st_a, st_b, st_q):
    n = x2d.shape[0]
    args = (x2d, g1, wmain, wsmall, caw, cbw, lng, lnb, ccw, alog, dtb, st_a, st_b, st_q)
    out_widths = (D_A + D_B, D_C, D_C, D_C, D_C, LANES, D_A, D_B, 3 * D_C)
    return pl.pallas_call(
        _inproj_sample_kernel,
        grid=(1,),
        in_specs=[_const_spec(a.shape) for a in args],
        out_specs=tuple(_const_spec((n, w)) for w in out_widths),
        out_shape=tuple(jax.ShapeDtypeStruct((n, w), F32) for w in out_widths),
        compiler_params=pltpu.CompilerParams(dimension_semantics=("arbitrary",), vmem_limit_bytes=VMEM_LIMIT),
        name="inproj_sample",
    )(*args)


def _delta_sample_kernel(q_ref, k_ref, v_ref, gb_ref, gate_ref, ong_ref, s_ref, yc_ref, snew_ref, *, nb):
    q = q_ref[...]
    k = k_ref[...]
    rows = [k[:, h * DK:(h + 1) * DK] for h in range(H_C)] + [q[:, h * DK:(h + 1) * DK] for h in range(H_C)]
    rows.append(jnp.zeros((LANES - 2 * H_C * nb, DK), F32))
    kq_t = jnp.concatenate(rows, axis=0).T
    gb = gb_ref[...]
    for i in range(nb):
        for h in range(H_C):
            cols = slice(h * DK, (h + 1) * DK)
            s = s_ref[i, h]
            k_col = kq_t[:, h * nb + i:h * nb + i + 1]
            q_col = kq_t[:, (H_C + h) * nb + i:(H_C + h) * nb + i + 1]
            k_s = jnp.sum(s * k_col, axis=0, keepdims=True)
            q_s = jnp.sum(s * q_col, axis=0, keepdims=True)
            e_g = jnp.exp(gb[i:i + 1, h:h + 1])
            beta = gb[i:i + 1, H_C + h:H_C + h + 1]
            v_new = beta * (v_ref[i:i + 1, cols] - e_g * k_s)
            qk = jnp.sum(q[i:i + 1, cols] * k[i:i + 1, cols], axis=-1, keepdims=True)
            o = e_g * q_s + qk * v_new
            snew_ref[i, h] = e_g * s + k_col * v_new
            yc_ref[i:i + 1, cols] = _gated_norm(o, ong_ref[...], gate_ref[i:i + 1, cols])


def _delta_sample(q, k, v, gb, gate, ong, state):
    n = q.shape[0]
    nb = 8
    row = lambda w: pl.BlockSpec((nb, w), lambda i: (i, 0))
    st = pl.BlockSpec((nb, H_C, DK, DV), lambda i: (i, 0, 0, 0))
    return pl.pallas_call(
        functools.partial(_delta_sample_kernel, nb=nb),
        grid=(n // nb,),
        in_specs=[row(D_C), row(D_C), row(D_C), row(LANES), row(D_C), _const_spec(ong.shape), st],
        out_specs=(row(D_C), st),
        out_shape=(jax.ShapeDtypeStruct((n, D_C), F32), jax.ShapeDtypeStruct(state.shape, F32)),
        compiler_params=pltpu.CompilerParams(dimension_semantics=("arbitrary",), vmem_limit_bytes=VMEM_LIMIT),
        name="delta_sample",
    )(q, k, v, gb, gate, ong, state)


def _outproj_router_kernel(x_ref, yab_ref, yc_ref, woa_ref, woc_ref, g2_ref, wrt_ref,
                           x1_ref, h2_ref, comb_ref, route_ref, cnt_ref):
    x1 = (x_ref[...]
          + jnp.dot(yab_ref[...].astype(BF16), woa_ref[...], preferred_element_type=F32)
          + jnp.dot(yc_ref[...].astype(BF16), woc_ref[...], preferred_element_type=F32))
    x1_ref[...] = x1
    h2 = _rms(x1, g2_ref[...])
    h2_ref[...] = h2.astype(BF16)
    logits = jnp.dot(h2, wrt_ref[...], preferred_element_type=F32, precision=HIGHEST)
    lane = lax.broadcasted_iota(jnp.int32, logits.shape, 1)
    neg = -jnp.inf
    gl = jnp.where(lane < N_GROUPS, logits, neg)
    g_max = jnp.max(gl, axis=-1, keepdims=True)
    g_idx = jnp.min(jnp.where(gl == g_max, lane, LANES), axis=-1, keepdims=True)
    g_p = 1.0 / jnp.sum(jnp.exp(gl - g_max), axis=-1, keepdims=True)
    lo = EXPERT_LANE0 + g_idx * EXP_PER_GROUP
    in_group = (lane >= lo) & (lane < lo + EXP_PER_GROUP)
    el = jnp.where(in_group, logits, neg)
    e_max = jnp.max(el, axis=-1, keepdims=True)
    pe = jnp.exp(el - e_max)
    e_prob = pe / jnp.sum(pe, axis=-1, keepdims=True)
    p1 = jnp.max(e_prob, axis=-1, keepdims=True)
    i1 = jnp.min(jnp.where(e_prob == p1, lane, LANES), axis=-1, keepdims=True)
    rest = jnp.where(in_group & (lane != i1), e_prob, -1.0)
    p2 = jnp.max(rest, axis=-1, keepdims=True)
    i2 = jnp.min(jnp.where(rest == p2, lane, LANES), axis=-1, keepdims=True)
    denom = p1 + p2
    w1 = g_p * (p1 / denom)
    w2 = g_p * (p2 / denom)
    comb_ref[...] = jnp.where(lane == i1, w1, 0.0) + jnp.where(lane == i2, w2, 0.0)
    e1 = i1 - EXPERT_LANE0
    e2 = i2 - EXPERT_LANE0
    route_ref[...] = jnp.where(lane == 0, e1.astype(F32),
                               jnp.where(lane == 1, e2.astype(F32),
                                         jnp.where(lane == 2, w1, jnp.where(lane == 3, w2, 0.0))))
    picked = jnp.where((lane == e1) | (lane == e2), 1.0, 0.0)
    cnt_ref[0] = jnp.broadcast_to(jnp.sum(picked, axis=0, keepdims=True), (8, LANES))


def _outproj_router(x2d, yab, yc, woa, woc, g2, wrt):
    n, d = x2d.shape
    tm = min(TOKEN_TILE, n)
    row = lambda w: pl.BlockSpec((tm, w), lambda i: (i, 0))
    return pl.pallas_call(
        _outproj_router_kernel,
        grid=(n // tm,),
        in_specs=[row(d), row(D_A + D_B), row(D_C), _const_spec(woa.shape), _const_spec(woc.shape),
                  _const_spec(g2.shape), _const_spec(wrt.shape)],
        out_specs=(row(d), row(d), row(LANES), row(LANES), pl.BlockSpec((1, 8, LANES), lambda i: (i, 0, 0))),
        out_shape=(jax.ShapeDtypeStruct((n, d), F32), jax.ShapeDtypeStruct((n, d), BF16),
                   jax.ShapeDtypeStruct((n, LANES), F32), jax.ShapeDtypeStruct((n, LANES), F32),
                   jax.ShapeDtypeStruct((n // tm, 8, LANES), F32)),
        compiler_params=pltpu.CompilerParams(dimension_semantics=("arbitrary",), vmem_limit_bytes=VMEM_LIMIT),
        name="outproj_router",
    )(x2d, yab, yc, woa, woc, g2, wrt)


def _route_plan(cnt, n_tok, tt):
    n_tiles = n_tok // tt
    cnt = cnt[:, 0, :N_EXPERTS].astype(jnp.int32)
    pc = (cnt + PIECE - 1) // PIECE * PIECE
    local_off = jnp.cumsum(pc, axis=1) - pc
    seg_len = jnp.sum(pc, axis=0)
    seg_pad = (seg_len + ROW_TILE - 1) // ROW_TILE * ROW_TILE
    seg_end = jnp.cumsum(seg_pad)
    seg_start = seg_end - seg_pad
    run_off = seg_start[None, :] + jnp.cumsum(pc, axis=0) - pc
    n_row_tiles = _max_sorted_rows(n_tok, tt) // ROW_TILE
    tile_row0 = jnp.arange(n_row_tiles, dtype=jnp.int32) * ROW_TILE
    tile_expert = jnp.minimum(jnp.sum((seg_end[None, :] <= tile_row0[:, None]).astype(jnp.int32), axis=1),
                              N_EXPERTS - 1)
    n_active = (seg_end[-1] // ROW_TILE).reshape(1)
    return dict(
        run_off=run_off.reshape(-1), local_off=local_off.reshape(-1), n_pieces=(pc // PIECE).reshape(-1),
        tile_pieces=jnp.sum(pc, axis=1) // PIECE,
        fill_off=seg_start + seg_len, fill_pieces=(seg_pad - seg_len) // PIECE,
        tile_expert=tile_expert, n_active=n_active,
        local_off_lanes=jnp.broadcast_to(
            jnp.pad(local_off.astype(F32), ((0, 0), (0, LANES - N_EXPERTS)))[:, None, :], (n_tiles, 8, LANES)),
    )


def _max_sorted_rows(n_tok, tt):
    n_tiles = n_tok // tt
    rows = 2 * n_tok + n_tiles * N_EXPERTS * (PIECE - 1) + N_EXPERTS * (ROW_TILE - 1)
    return (rows + ROW_TILE - 1) // ROW_TILE * ROW_TILE


def _local_rows(tt):
    return 2 * tt + N_EXPERTS * PIECE


def _local_positions(route, loff_lanes, tt):
    lane = lax.broadcasted_iota(jnp.int32, (tt, LANES), 1).astype(F32)
    oh1 = lane == route[:, 0:1]
    oh2 = lane == route[:, 1:2]
    picked = jnp.where(oh1 | oh2, 1.0, 0.0).astype(BF16)
    ri = lax.broadcasted_iota(jnp.int32, (tt, tt), 0)
    ci = lax.broadcasted_iota(jnp.int32, (tt, tt), 1)
    earlier = jnp.where(ri > ci, 1.0, 0.0).astype(BF16)
    base = jnp.dot(earlier, picked, preferred_element_type=F32) + loff_lanes
    pos1 = jnp.sum(jnp.where(oh1, base, 0.0), axis=-1, keepdims=True)
    pos2 = jnp.sum(jnp.where(oh2, base, 0.0), axis=-1, keepdims=True)
    return pos1, pos2


def _piece_copy(src, dst, sem):
    return pltpu.make_async_copy(src, dst, sem)


def _dispatch_kernel(run_off, local_off, n_pieces, tile_pieces, fill_off, fill_pieces, n_active,
                     h2_ref, route_ref, loff_ref, xs_hbm, lbuf, zbuf, sem, *, tt, n_tiles, n_row_tiles):
    i = pl.program_id(0)
    slot = i % 2
    lrows = _local_rows(tt)

    def wait_tile(tile, s):
        def body(_, c):
            _piece_copy(lbuf.at[s, pl.ds(0, PIECE)], xs_hbm.at[pl.ds(0, PIECE)], sem.at[s]).wait()
            return c
        lax.fori_loop(0, tile_pieces[tile], body, 0)

    @pl.when(i >= 2)
    def _():
        wait_tile(i - 2, slot)

    route = route_ref[...]
    pos1, pos2 = _local_positions(route, loff_ref[0, 0:1, :], tt)
    lane = lax.broadcasted_iota(jnp.int32, (tt, LANES), 1)
    packed = jnp.where(lane == 0, pos1, jnp.where(lane == 1, pos2, 0.0)).T
    row = lax.broadcasted_iota(jnp.int32, (lrows, tt), 0).astype(F32)
    perm = jnp.where((row == packed[0:1, :]) | (row == packed[1:2, :]), 1.0, 0.0).astype(BF16)
    lbuf[slot] = jnp.dot(perm, h2_ref[...], preferred_element_type=F32).astype(BF16)

    for e in range(N_EXPERTS):
        src0 = local_off[i * N_EXPERTS + e]
        dst0 = run_off[i * N_EXPERTS + e]

        def body(p, c, src0=src0, dst0=dst0):
            _piece_copy(lbuf.at[slot, pl.ds(pl.multiple_of(src0 + p * PIECE, PIECE), PIECE)],
                        xs_hbm.at[pl.ds(pl.multiple_of(dst0 + p * PIECE, PIECE), PIECE)], sem.at[slot]).start()
            return c
        lax.fori_loop(0, n_pieces[i * N_EXPERTS + e], body, 0)

    @pl.when(i == n_tiles - 1)
    def _():
        zbuf[...] = jnp.zeros(zbuf.shape, BF16)
        zpiece = zbuf.at[pl.ds(0, PIECE)]
        for e in range(N_EXPERTS):
            def body(p, c, e=e):
                _piece_copy(zpiece, xs_hbm.at[pl.ds(pl.multiple_of(fill_off[e] + p * PIECE, PIECE), PIECE)],
                            sem.at[2]).start()
                return c
            lax.fori_loop(0, fill_pieces[e], body, 0)

        def tail_body(j, c):
            _piece_copy(zbuf, xs_hbm.at[pl.ds(pl.multiple_of(j * ROW_TILE, ROW_TILE), ROW_TILE)], sem.at[3]).start()
            return c
        lax.fori_loop(n_active[0], n_row_tiles, tail_body, 0)
        for e in range(N_EXPERTS):
            def body(p, c):
                _piece_copy(zpiece, xs_hbm.at[pl.ds(0, PIECE)], sem.at[2]).wait()
                return c
            lax.fori_loop(0, fill_pieces[e], body, 0)

        def tail_wait(j, c):
            _piece_copy(zbuf, xs_hbm.at[pl.ds(0, ROW_TILE)], sem.at[3]).wait()
            return c
        lax.fori_loop(n_active[0], n_row_tiles, tail_wait, 0)
        if n_tiles > 1:
            wait_tile(i - 1, 1 - slot)
        wait_tile(i, slot)


def _dispatch(h2, route, plan, tt):
    n, d = h2.shape
    n_tiles = n // tt
    n_rows = _max_sorted_rows(n, tt)
    kernel_fn = functools.partial(_dispatch_kernel, tt=tt, n_tiles=n_tiles, n_row_tiles=n_rows // ROW_TILE)
    grid_spec = pltpu.PrefetchScalarGridSpec(
        num_scalar_prefetch=7,
        grid=(n_tiles,),
        in_specs=[pl.BlockSpec((tt, d), lambda i, *_: (i, 0)),
                  pl.BlockSpec((tt, LANES), lambda i, *_: (i, 0)),
                  pl.BlockSpec((1, 8, LANES), lambda i, *_: (i, 0, 0))],
        out_specs=pl.BlockSpec(memory_space=pl.ANY),
        scratch_shapes=[pltpu.VMEM((2, _local_rows(tt), d), BF16), pltpu.VMEM((ROW_TILE, d), BF16),
                        pltpu.SemaphoreType.DMA((4,))],
    )
    return pl.pallas_call(
        kernel_fn,
        grid_spec=grid_spec,
        out_shape=jax.ShapeDtypeStruct((n_rows, d), BF16),
        compiler_params=pltpu.CompilerParams(dimension_semantics=("arbitrary",), vmem_limit_bytes=VMEM_LIMIT),
        name="moe_dispatch",
    )(plan["run_off"], plan["local_off"], plan["n_pieces"], plan["tile_pieces"], plan["fill_off"],
      plan["fill_pieces"], plan["n_active"], h2, route, plan["local_off_lanes"])


def _experts_kernel(tile_expert, n_active, xs_ref, wg_ref, wu_ref, wd_ref, ys_ref, wg_b, wu_b, wd_b):
    j = pl.program_id(0)
    expert = tile_expert[j]
    previous = tile_expert[jnp.maximum(j - 1, 0)]

    @pl.when((j == 0) | (expert != previous))
    def _():
        wg_b[...] = wg_ref[0].astype(BF16)
        wu_b[...] = wu_ref[0].astype(BF16)
        wd_b[...] = wd_ref[0].astype(BF16)

    @pl.when(j < n_active[0])
    def _():
        x = xs_ref[...]
        gate = jnp.dot(x, wg_b[...], preferred_element_type=F32)
        up = jnp.dot(x, wu_b[...], preferred_element_type=F32)
        act = (_silu(gate) * up).astype(BF16)
        ys_ref[...] = jnp.dot(act, wd_b[...], preferred_element_type=F32).astype(BF16)

    @pl.when(j >= n_active[0])
    def _():
        ys_ref[...] = jnp.zeros(ys_ref.shape, BF16)


def _experts(xs, wg, wu, wd, plan):
    rows, d = xs.shape
    n_e, _, d_e = wg.shape
    live = lambda j, n_active: jnp.minimum(j, n_active[0] - 1)
    grid_spec = pltpu.PrefetchScalarGridSpec(
        num_scalar_prefetch=2,
        grid=(rows // ROW_TILE,),
        in_specs=[pl.BlockSpec((ROW_TILE, d), lambda j, te, na: (live(j, na), 0)),
                  pl.BlockSpec((1, d, d_e), lambda j, te, na: (te[live(j, na)], 0, 0)),
                  pl.BlockSpec((1, d, d_e), lambda j, te, na: (te[live(j, na)], 0, 0)),
                  pl.BlockSpec((1, d_e, d), lambda j, te, na: (te[live(j, na)], 0, 0))],
        out_specs=pl.BlockSpec((ROW_TILE, d), lambda j, te, na: (j, 0)),
        scratch_shapes=[pltpu.VMEM((d, d_e), BF16), pltpu.VMEM((d, d_e), BF16), pltpu.VMEM((d_e, d), BF16)],
    )
    return pl.pallas_call(
        _experts_kernel,
        grid_spec=grid_spec,
        out_shape=jax.ShapeDtypeStruct((rows, d), BF16),
        compiler_params=pltpu.CompilerParams(dimension_semantics=("arbitrary",), vmem_limit_bytes=VMEM_LIMIT),
        name="moe_experts",
    )(plan["tile_expert"], plan["n_active"], xs, wg, wu, wd)


def _combine_kernel(run_off, local_off, n_pieces, tile_pieces,
                    x1_ref, route_ref, loff_ref, fg_ref, ys_hbm, out_ref, lbuf, sem, *, tt, n_tiles, final_norm):
    i = pl.program_id(0)
    slot = i % 2
    lrows = _local_rows(tt)

    def fetch_tile(tile, s):
        for e in range(N_EXPERTS):
            src0 = run_off[tile * N_EXPERTS + e]
            dst0 = local_off[tile * N_EXPERTS + e]

            def body(p, c, src0=src0, dst0=dst0):
                _piece_copy(ys_hbm.at[pl.ds(pl.multiple_of(src0 + p * PIECE, PIECE), PIECE)],
                            lbuf.at[s, pl.ds(pl.multiple_of(dst0 + p * PIECE, PIECE), PIECE)], sem.at[s]).start()
                return c
            lax.fori_loop(0, n_pieces[tile * N_EXPERTS + e], body, 0)

    @pl.when(i == 0)
    def _():
        lbuf[...] = jnp.zeros(lbuf.shape, BF16)
        fetch_tile(0, 0)

    @pl.when(i + 1 < n_tiles)
    def _():
        fetch_tile(i + 1, 1 - slot)

    def wait_body(_, c):
        _piece_copy(ys_hbm.at[pl.ds(0, PIECE)], lbuf.at[slot, pl.ds(0, PIECE)], sem.at[slot]).wait()
        return c
    lax.fori_loop(0, tile_pieces[i], wait_body, 0)

    route = route_ref[...]
    pos1, pos2 = _local_positions(route, loff_ref[0, 0:1, :], tt)
    col = lax.broadcasted_iota(jnp.int32, (tt, lrows), 1).astype(F32)
    weights = (jnp.where(col == pos1, route[:, 2:3], 0.0) + jnp.where(col == pos2, route[:, 3:4], 0.0)).astype(BF16)
    out = x1_ref[...] + jnp.dot(weights, lbuf[slot], preferred_element_type=F32)
    out_ref[...] = _rms(out, fg_ref[...]) if final_norm else out


def _combine(x1, route, ys, fg, plan, tt, final_norm):
    n, d = x1.shape
    n_tiles = n // tt
    kernel_fn = functools.partial(_combine_kernel, tt=tt, n_tiles=n_tiles, final_norm=final_norm)
    grid_spec = pltpu.PrefetchScalarGridSpec(
        num_scalar_prefetch=4,
        grid=(n_tiles,),
        in_specs=[pl.BlockSpec((tt, d), lambda i, *_: (i, 0)),
                  pl.BlockSpec((tt, LANES), lambda i, *_: (i, 0)),
                  pl.BlockSpec((1, 8, LANES), lambda i, *_: (i, 0, 0)),
                  pl.BlockSpec(fg.shape, lambda i, *_: (0, 0)),
                  pl.BlockSpec(memory_space=pl.ANY)],
        out_specs=pl.BlockSpec((tt, d), lambda i, *_: (i, 0)),
        scratch_shapes=[pltpu.VMEM((2, _local_rows(tt), d), BF16), pltpu.SemaphoreType.DMA((2,))],
    )
    return pl.pallas_call(
        kernel_fn,
        grid_spec=grid_spec,
        out_shape=jax.ShapeDtypeStruct((n, d), F32),
        compiler_params=pltpu.CompilerParams(dimension_semantics=("arbitrary",), vmem_limit_bytes=VMEM_LIMIT),
        name="moe_combine",
    )(plan["run_off"], plan["local_off"], plan["n_pieces"], plan["tile_pieces"], x1, route,
      plan["local_off_lanes"], fg, ys)


def _routed_moe(x1, h2, route, cnt, wg, wu, wd, fg, final_norm):
    n = x1.shape[0]
    tt = min(TOKEN_TILE, n)
    plan = _route_plan(cnt, n, tt)
    xs = _dispatch(h2, route, plan, tt)
    ys = _experts(xs, wg, wu, wd, plan)
    return _combine(x1, route, ys, fg, plan, tt, final_norm)


def _moe_kernel(x1_ref, h2_ref, comb_ref, wg_ref, wu_ref, wd_ref, fg_ref, out_ref, *, final_norm):
    e = pl.program_id(1)

    @pl.when(e == 0)
    def _():
        out_ref[...] = x1_ref[...]

    h2 = h2_ref[...]
    gate = jnp.dot(h2, wg_ref[0].astype(BF16), preferred_element_type=F32)
    up = jnp.dot(h2, wu_ref[0].astype(BF16), preferred_element_type=F32)
    comb = comb_ref[...]
    lane = lax.broadcasted_iota(jnp.int32, comb.shape, 1)
    c_col = jnp.sum(jnp.where(lane == EXPERT_LANE0 + e, comb, 0.0), axis=-1, keepdims=True)
    act = (_silu(gate) * up * c_col).astype(BF16)
    out_ref[...] += jnp.dot(act, wd_ref[0].astype(BF16), preferred_element_type=F32)

    if final_norm:
        @pl.when(e == pl.num_programs(1) - 1)
        def _():
            out_ref[...] = _rms(out_ref[...], fg_ref[...])


def _moe(x1, h2, comb, wg, wu, wd, fg, final_norm):
    n, d = x1.shape
    tm = min(1024, n)
    n_e, _, d_e = wg.shape
    row = lambda w: pl.BlockSpec((tm, w), lambda i, e: (i, 0))
    return pl.pallas_call(
        functools.partial(_moe_kernel, final_norm=final_norm),
        grid=(n // tm, n_e),
        in_specs=[row(d), row(d), row(LANES),
                  pl.BlockSpec((1, d, d_e), lambda i, e: (e, 0, 0)),
                  pl.BlockSpec((1, d, d_e), lambda i, e: (e, 0, 0)),
                  pl.BlockSpec((1, d_e, d), lambda i, e: (e, 0, 0)),
                  pl.BlockSpec(fg.shape, lambda i, e: (0, 0))],
        out_specs=row(d),
        out_shape=jax.ShapeDtypeStruct((n, d), F32),
        compiler_params=pltpu.CompilerParams(dimension_semantics=("arbitrary", "arbitrary"),
                                             vmem_limit_bytes=VMEM_LIMIT),
        name="moe",
    )(x1, h2, comb, wg, wu, wd, fg)


def _pad_lanes(a, lane0=0):
    a = a.reshape((1, -1)) if a.ndim == 1 else a
    return jnp.pad(a, ((0, 0), (lane0, LANES - lane0 - a.shape[1])))


def kernel(x_prompt, x_sample, state_conv_a, state_conv_b, state_conv_qkv, state_delta, norm1_g, w_in, conv_a_w,
           conv_b_w, ln_b_g, ln_b_b, conv_c_w, a_log, dt_bias, o_norm_g, w_out, norm2_g, w_group, w_router,
           w_gate, w_up, w_down, final_g):
    n_b, t_len, d = x_prompt.shape
    n_s = x_sample.shape[0]
    depth = w_in.shape[0]
    n_main = 3 * D_A + 2 * D_B + 3 * D_C + D_C
    xp = x_prompt.reshape(n_b * t_len, d)
    xs = x_sample.reshape(n_s, d)
    fg = final_g.reshape(1, d)
    outs = {k: [] for k in ("pa", "pb", "pq", "pd", "sa", "sb", "sq", "sd")}
    for l in range(depth):
        g1 = norm1_g[l].reshape(1, d)
        wmain = w_in[l][:, :n_main].astype(BF16)
        wsmall = _pad_lanes(jnp.concatenate([w_in[l][:, n_main + H_C:], w_in[l][:, n_main:n_main + H_C]], axis=1))
        caw, cbw, ccw = conv_a_w[l], conv_b_w[l], conv_c_w[l]
        lng, lnb = ln_b_g[l].reshape(1, D_B), ln_b_b[l].reshape(1, D_B)
        alog, dtb = _pad_lanes(a_log[l]), _pad_lanes(dt_bias[l])
        ong = o_norm_g[l].reshape(1, DV)
        woa = w_out[l][:D_A + D_B].astype(BF16)
        woc = w_out[l][D_A + D_B:].astype(BF16)
        g2 = norm2_g[l].reshape(1, d)
        wrt = _pad_lanes(jnp.concatenate(
            [w_group[l], w_router[l].transpose(1, 0, 2).reshape(d, N_EXPERTS)], axis=1))
        wg, wu, wd = w_gate[l], w_up[l], w_down[l]
        last = l == depth - 1

        yab, q, k, v, gate, gb, st_a, st_b, st_q = _inproj_prompt(
            xp, n_b, t_len, g1, wmain, wsmall, caw, cbw, lng, lnb, ccw, alog, dtb)
        yc, s_fin = _delta_prompt(q, k, v, gb, gate, ong, n_b, t_len)
        x1, h2, _, route, cnt = _outproj_router(xp, yab, yc, woa, woc, g2, wrt)
        xp = _routed_moe(x1, h2, route, cnt, wg, wu, wd, fg, last)
        outs["pa"].append(st_a); outs["pb"].append(st_b); outs["pq"].append(st_q); outs["pd"].append(s_fin)

        hist_a = state_conv_a[l].transpose(1, 0, 2)
        hist_b = state_conv_b[l].transpose(1, 0, 2)
        hist_q = state_conv_qkv[l].transpose(1, 0, 2)
        yab, q, k, v, gate, gb, un_a, un_b, un_q = _inproj_sample(
            xs, g1, wmain, wsmall, caw, cbw, lng, lnb, ccw, alog, dtb, hist_a, hist_b, hist_q)
        yc, s_new = _delta_sample(q, k, v, gb, gate, ong, state_delta[l])
        x1, h2, comb, _, _ = _outproj_router(xs, yab, yc, woa, woc, g2, wrt)
        xs = _moe(x1, h2, comb, wg, wu, wd, fg, last)
        outs["sa"].append(jnp.concatenate([state_conv_a[l][:, 1:], un_a[:, None]], axis=1))
        outs["sb"].append(jnp.concatenate([state_conv_b[l][:, 1:], un_b[:, None]], axis=1))
        outs["sq"].append(jnp.concatenate([state_conv_qkv[l][:, 1:], un_q[:, None]], axis=1))
        outs["sd"].append(s_new)

    stack = lambda key: jnp.stack(outs[key])
    return (xp.reshape(n_b, t_len, d), xs.reshape(n_s, 1, d),
            stack("pa"), stack("pb"), stack("pq"), stack("pd"),
            stack("sa"), stack("sb"), stack("sq"), stack("sd"))
```

```python
import functools

import jax
import jax.numpy as jnp
from jax import lax
from jax.experimental import pallas as pl
from jax.experimental.pallas import tpu as pltpu

F32 = jnp.float32
BF16 = jnp.bfloat16

EPS = 1e-6
H_C = 4
DK = 128
DV = 128
D_A = 256
D_B = 256
D_C = 512
W_A = 3
W_B = 31
W_C = 4
CHUNK = 64
N_GROUPS = 4
EXP_PER_GROUP = 8
N_EXPERTS = N_GROUPS * EXP_PER_GROUP
LANES = 128
PAD_A = 8
PAD_B = 32
PAD_C = 8
EXPERT_LANE0 = N_GROUPS
VMEM_LIMIT = 56 * 1024 * 1024
TOKEN_TILE = 512
PIECE = 16
ROW_TILE = 512


def _silu(x):
    return x * (1.0 / (1.0 + jnp.exp(-x)))


def _sigmoid(x):
    return 1.0 / (1.0 + jnp.exp(-x))


def _softplus(x):
    return jnp.maximum(x, 0.0) + jnp.log1p(jnp.exp(-jnp.abs(x)))


def _rms(x, g):
    return x * lax.rsqrt(jnp.mean(x * x, axis=-1, keepdims=True) + EPS) * g


def _conv_taps(buf, w_ref, width, pad, n_rows, n_cols, row_blk, col_blk):
    row_parts = []
    for r0 in range(0, n_rows, row_blk):
        col_parts = []
        for c0 in range(0, n_cols, col_blk):
            acc = None
            for j in range(width):
                start = pad - (width - 1) + j + r0
                term = buf[start:start + row_blk, c0:c0 + col_blk] * w_ref[j:j + 1, c0:c0 + col_blk]
                acc = term if acc is None else acc + term
            col_parts.append(acc)
        row_parts.append(col_parts[0] if len(col_parts) == 1 else jnp.concatenate(col_parts, axis=1))
    return row_parts[0] if len(row_parts) == 1 else jnp.concatenate(row_parts, axis=0)


def _conv_taps_realigned(buf, shifted, w_ref, width, pad, n_rows, row_blk):
    first = pad - (width - 1)
    n_shift = n_rows + (first + width - 1) // 8 * 8 - 8
    for r in range(1, 8):
        shifted[r - 1] = buf[r:r + n_shift, :]
    row_parts = []
    for r0 in range(0, n_rows, row_blk):
        acc = None
        for j in range(width):
            q, r = divmod(first + j, 8)
            lo = 8 * q + r0
            window = buf[lo:lo + row_blk, :] if r == 0 else shifted[r - 1, lo:lo + row_blk, :]
            term = window * w_ref[j:j + 1, :]
            acc = term if acc is None else acc + term
        row_parts.append(acc)
    return row_parts[0] if len(row_parts) == 1 else jnp.concatenate(row_parts, axis=0)


def _l2norm_heads(x, scale):
    parts = []
    for h in range(H_C):
        xh = x[:, h * DK:(h + 1) * DK]
        parts.append(xh * (lax.rsqrt(jnp.sum(xh * xh, axis=-1, keepdims=True) + 1e-6) * scale))
    return jnp.concatenate(parts, axis=1)


def _dot_3pass(a, a_hi, w):
    a_lo = (a - a_hi.astype(F32)).astype(BF16)
    w_hi = w.astype(BF16)
    w_lo = (w - w_hi.astype(F32)).astype(BF16)
    n = w.shape[1]
    both = jnp.dot(a_hi, jnp.concatenate([w_hi, w_lo], axis=1), preferred_element_type=F32)
    return both[:, :n] + both[:, n:] + jnp.dot(a_lo, w_hi, preferred_element_type=F32)


def _in_projection(x, g1_ref, wmain_ref, wsmall_ref):
    h = _rms(x, g1_ref[...])
    hb = h.astype(BF16)
    z_a = jnp.dot(hb, wmain_ref[:, 0:768], preferred_element_type=F32)
    z_b = jnp.dot(hb, wmain_ref[:, 768:1280], preferred_element_type=F32)
    z_qkv = jnp.dot(hb, wmain_ref[:, 1280:2816], preferred_element_type=F32)
    z_gate = jnp.dot(hb, wmain_ref[:, 2816:3328], preferred_element_type=F32)
    z_s = _dot_3pass(h, hb, wsmall_ref[...])
    return z_a, z_b, z_qkv, z_gate, z_s


def _layer_norm_silu(x, g, b):
    mu = jnp.mean(x, axis=-1, keepdims=True)
    xc = x - mu
    y = xc * lax.rsqrt(jnp.mean(xc * xc, axis=-1, keepdims=True) + EPS)
    return _silu(y * g + b)


def _decay_and_beta(z_s, alog_ref, dtb_ref):
    lane = lax.broadcasted_iota(jnp.int32, z_s.shape, 1)
    g = -jnp.exp(alog_ref[...]) * _softplus(z_s + dtb_ref[...])
    g = jnp.where(lane < H_C, g, 0.0)
    beta = jnp.where((lane >= H_C) & (lane < 2 * H_C), _sigmoid(z_s), 0.0)
    return g, beta


def _inproj_prompt_kernel(x_ref, g1_ref, wmain_ref, wsmall_ref, caw_ref, cbw_ref, lng_ref, lnb_ref,
                          ccw_ref, alog_ref, dtb_ref,
                          yab_ref, q_ref, k_ref, v_ref, gate_ref, gb_ref, sta_ref, stb_ref, stq_ref,
                          abuf, bbuf, cbuf, bshift, *, tt, nt):
    t = pl.program_id(1)
    z_a, z_b, z_qkv, z_gate, z_s = _in_projection(x_ref[...], g1_ref, wmain_ref, wsmall_ref)

    @pl.when(t == 0)
    def _():
        abuf[0:PAD_A, :] = jnp.zeros((PAD_A, D_A), F32)
        bbuf[0:PAD_B, :] = jnp.zeros((PAD_B, D_B), F32)
        cbuf[0:PAD_C, :] = jnp.zeros((PAD_C, 3 * D_C), F32)

    abuf[PAD_A:PAD_A + tt, :] = z_a[:, 256:512] * z_a[:, 512:768]
    conv_a = _conv_taps(abuf, caw_ref, W_A, PAD_A, tt, D_A, 64, 256)
    yab_ref[:, 0:D_A] = z_a[:, 0:256] * conv_a

    bbuf[PAD_B:PAD_B + tt, :] = z_b[:, 0:256] * _sigmoid(z_b[:, 256:512])
    conv_b = _conv_taps_realigned(bbuf, bshift, cbw_ref, W_B, PAD_B, tt, 64)
    yab_ref[:, D_A:D_A + D_B] = _layer_norm_silu(conv_b, lng_ref[...], lnb_ref[...])

    cbuf[PAD_C:PAD_C + tt, :] = z_qkv
    qkv = _silu(_conv_taps(cbuf, ccw_ref, W_C, PAD_C, tt, 3 * D_C, 64, 512))
    q_ref[...] = _l2norm_heads(qkv[:, 0:D_C], DK ** -0.5)
    k_ref[...] = _l2norm_heads(qkv[:, D_C:2 * D_C], 1.0)
    v_ref[...] = qkv[:, 2 * D_C:3 * D_C]
    gate_ref[...] = z_gate

    g, beta = _decay_and_beta(z_s, alog_ref, dtb_ref)
    row_in_chunk = lax.broadcasted_iota(jnp.int32, g.shape, 0) & (CHUNK - 1)
    shift = 1
    while shift < CHUNK:
        g = g + jnp.where(row_in_chunk >= shift, pltpu.roll(g, shift, axis=0), 0.0)
        shift *= 2
    gb_ref[...] = g + beta

    @pl.when(t == nt - 1)
    def _():
        sta_ref[0] = abuf[PAD_A + tt - (W_A - 1):PAD_A + tt, :]
        stb_ref[0] = bbuf[PAD_B + tt - (W_B - 1):PAD_B + tt, :]
        stq_ref[0] = cbuf[PAD_C + tt - (W_C - 1):PAD_C + tt, :]

    abuf[0:PAD_A, :] = abuf[tt:tt + PAD_A, :]
    bbuf[0:PAD_B, :] = bbuf[tt:tt + PAD_B, :]
    cbuf[0:PAD_C, :] = cbuf[tt:tt + PAD_C, :]


def _const_spec(shape):
    nd = len(shape)
    return pl.BlockSpec(shape, lambda *_: (0,) * nd)


def _inproj_prompt(x2d, n_b, t_len, g1, wmain, wsmall, caw, cbw, lng, lnb, ccw, alog, dtb):
    tt = min(512, t_len)
    nt = t_len // tt
    n_tok = n_b * t_len
    d = x2d.shape[1]
    row = lambda w: pl.BlockSpec((tt, w), lambda b, t: (b * nt + t, 0))
    st = lambda r, w: pl.BlockSpec((1, r, w), lambda b, t: (b, 0, 0))
    out_shape = (
        jax.ShapeDtypeStruct((n_tok, D_A + D_B), F32),
        jax.ShapeDtypeStruct((n_tok, D_C), F32),
        jax.ShapeDtypeStruct((n_tok, D_C), F32),
        jax.ShapeDtypeStruct((n_tok, D_C), F32),
        jax.ShapeDtypeStruct((n_tok, D_C), F32),
        jax.ShapeDtypeStruct((n_tok, LANES), F32),
        jax.ShapeDtypeStruct((n_b, W_A - 1, D_A), F32),
        jax.ShapeDtypeStruct((n_b, W_B - 1, D_B), F32),
        jax.ShapeDtypeStruct((n_b, W_C - 1, 3 * D_C), F32),
    )
    return pl.pallas_call(
        functools.partial(_inproj_prompt_kernel, tt=tt, nt=nt),
        grid=(n_b, nt),
        in_specs=[row(d), _const_spec(g1.shape), _const_spec(wmain.shape), _const_spec(wsmall.shape),
                  _const_spec(caw.shape), _const_spec(cbw.shape), _const_spec(lng.shape), _const_spec(lnb.shape),
                  _const_spec(ccw.shape), _const_spec(alog.shape), _const_spec(dtb.shape)],
        out_specs=(row(D_A + D_B), row(D_C), row(D_C), row(D_C), row(D_C), row(LANES),
                   st(W_A - 1, D_A), st(W_B - 1, D_B), st(W_C - 1, 3 * D_C)),
        out_shape=out_shape,
        scratch_shapes=[pltpu.VMEM((PAD_A + tt, D_A), F32), pltpu.VMEM((PAD_B + tt, D_B), F32),
                        pltpu.VMEM((PAD_C + tt, 3 * D_C), F32), pltpu.VMEM((7, PAD_B + tt - 8, D_B), F32)],
        compiler_params=pltpu.CompilerParams(dimension_semantics=("arbitrary", "arbitrary"),
                                             vmem_limit_bytes=VMEM_LIMIT),
        name="inproj_prompt",
    )(x2d, g1, wmain, wsmall, caw, cbw, lng, lnb, ccw, alog, dtb)


def _dot_nt(a, b):
    return lax.dot_general(a, b, (((1,), (1,)), ((), ())), preferred_element_type=F32)


def _dot_tn(a, b):
    return lax.dot_general(a, b, (((0,), (0,)), ((), ())), preferred_element_type=F32)


def _bdot(a, b):
    return jnp.dot(a.astype(BF16), b.astype(BF16), preferred_element_type=F32)


def _gated_norm(o, on_g, gate):
    return o * lax.rsqrt(jnp.mean(o * o, axis=-1, keepdims=True) + EPS) * on_g * _silu(gate)


def _delta_prompt_kernel(q_ref, k_ref, v_ref, gb_ref, gate_ref, ong_ref, yc_ref, sfin_ref, s_scr, *, tq, nt):
    t = pl.program_id(1)

    @pl.when(t == 0)
    def _():
        s_scr[...] = jnp.zeros(s_scr.shape, F32)

    c = CHUNK
    n_chunks = tq // c
    n_double = c.bit_length() - 2
    inst = [(ch, h) for ch in range(n_chunks) for h in range(H_C)]
    rows = lambda ch: slice(ch * c, (ch + 1) * c)
    cols = lambda h: slice(h * DK, (h + 1) * DK)
    ri = lax.broadcasted_iota(jnp.int32, (c, c), 0)
    ci = lax.broadcasted_iota(jnp.int32, (c, c), 1)
    gb = gb_ref[...]

    q = [q_ref[rows(ch), cols(h)] for ch, h in inst]
    k = [k_ref[rows(ch), cols(h)] for ch, h in inst]
    v = [v_ref[rows(ch), cols(h)] for ch, h in inst]
    g_col = [gb[rows(ch), h:h + 1] for ch, h in inst]
    b_col = [gb[rows(ch), H_C + h:H_C + h + 1] for ch, h in inst]
    g_row = [jnp.sum(jnp.where(ri == ci, g, 0.0), axis=0, keepdims=True) for g in g_col]
    diff = [gc - gr for gc, gr in zip(g_col, g_row)]
    dec_strict = [jnp.exp(jnp.where(ri > ci, d, -jnp.inf)) for d in diff]
    dec_incl = [jnp.exp(jnp.where(ri >= ci, d, -jnp.inf)) for d in diff]
    e_g = [jnp.exp(g) for g in g_col]
    g_last = [g[c - 1:c, :] for g in g_col]

    qkk = [_dot_nt(jnp.concatenate([qi, ki], axis=0).astype(BF16), ki.astype(BF16)) for qi, ki in zip(q, k)]
    qk = [m[:c] * d for m, d in zip(qkk, dec_incl)]
    p = [-(b * m[c:] * d) for b, m, d in zip(b_col, qkk, dec_strict)]
    sol = [jnp.concatenate([vi * b, ki * (b * eg)], axis=1) for vi, ki, b, eg in zip(v, k, b_col, e_g)]
    for level in range(n_double + 1):
        sol = [s + _bdot(pi, s) for pi, s in zip(p, sol)]
        if level < n_double:
            p = [_bdot(pi, pi) for pi in p]
    kd = [ki * jnp.exp(gl - g) for ki, gl, g in zip(k, g_last, g_col)]
    qk_uw = [_bdot(m, s) for m, s in zip(qk, sol)]
    kd_uw = [_dot_tn(x.astype(BF16), s.astype(BF16)) for x, s in zip(kd, sol)]
    lhs = [jnp.concatenate([qi * eg - a[:, DV:], -b[:, DV:]], axis=0).astype(BF16)
           for qi, eg, a, b in zip(q, e_g, qk_uw, kd_uw)]
    decay = [jnp.exp(gl) for gl in g_last]

    state = [s_scr[h] for h in range(H_C)]
    for ch in range(n_chunks):
        base = ch * H_C
        r = [jnp.dot(lhs[base + h], state[h].astype(BF16), preferred_element_type=F32) for h in range(H_C)]
        for h in range(H_C):
            o = r[h][:c] + qk_uw[base + h][:, :DV]
            yc_ref[rows(ch), cols(h)] = _gated_norm(o, ong_ref[...], gate_ref[rows(ch), cols(h)])
        state = [decay[base + h] * state[h] + r[h][c:] + kd_uw[base + h][:, :DV] for h in range(H_C)]
    for h in range(H_C):
        s_scr[h] = state[h]

    @pl.when(t == nt - 1)
    def _():
        sfin_ref[0] = s_scr[...]


def _delta_prompt(q, k, v, gb, gate, ong, n_b, t_len):
    tq = min(256, t_len)
    nt = t_len // tq
    row = lambda w: pl.BlockSpec((tq, w), lambda b, t: (b * nt + t, 0))
    return pl.pallas_call(
        functools.partial(_delta_prompt_kernel, tq=tq, nt=nt),
        grid=(n_b, nt),
        in_specs=[row(D_C), row(D_C), row(D_C), row(LANES), row(D_C), _const_spec(ong.shape)],
        out_specs=(row(D_C), pl.BlockSpec((1, H_C, DK, DV), lambda b, t: (b, 0, 0, 0))),
        out_shape=(jax.ShapeDtypeStruct((n_b * t_len, D_C), F32),
                   jax.ShapeDtypeStruct((n_b, H_C, DK, DV), F32)),
        scratch_shapes=[pltpu.VMEM((H_C, DK, DV), F32)],
        compiler_params=pltpu.CompilerParams(dimension_semantics=("arbitrary", "arbitrary"),
                                             vmem_limit_bytes=VMEM_LIMIT),
        name="delta_prompt",
    )(q, k, v, gb, gate, ong)


def _inproj_sample_kernel(x_ref, g1_ref, wmain_ref, wsmall_ref, caw_ref, cbw_ref, lng_ref, lnb_ref,
                          ccw_ref, alog_ref, dtb_ref, sta_ref, stb_ref, stq_ref,
                          yab_ref, q_ref, k_ref, v_ref, gate_ref, gb_ref, una_ref, unb_ref, unq_ref):
    z_a, z_b, z_qkv, z_gate, z_s = _in_projection(x_ref[...], g1_ref, wmain_ref, wsmall_ref)

    def conv_step(state_ref, new, w_ref, width):
        acc = new * w_ref[width - 1:width, :]
        for j in range(width - 1):
            acc = acc + state_ref[j] * w_ref[j:j + 1, :]
        return acc

    u_a = z_a[:, 256:512] * z_a[:, 512:768]
    una_ref[...] = u_a
    yab_ref[:, 0:D_A] = z_a[:, 0:256] * conv_step(sta_ref, u_a, caw_ref, W_A)

    u_b = z_b[:, 0:256] * _sigmoid(z_b[:, 256:512])
    unb_ref[...] = u_b
    yab_ref[:, D_A:D_A + D_B] = _layer_norm_silu(conv_step(stb_ref, u_b, cbw_ref, W_B), lng_ref[...], lnb_ref[...])

    unq_ref[...] = z_qkv
    qkv = _silu(conv_step(stq_ref, z_qkv, ccw_ref, W_C))
    q_ref[...] = _l2norm_heads(qkv[:, 0:D_C], DK ** -0.5)
    k_ref[...] = _l2norm_heads(qkv[:, D_C:2 * D_C], 1.0)
    v_ref[...] = qkv[:, 2 * D_C:3 * D_C]
    gate_ref[...] = z_gate
    g, beta = _decay_and_beta(z_s, alog_ref, dtb_ref)
    gb_ref[...] = g + beta


def _inproj_sample(x2d, g1, wmain, wsmall, caw, cbw, lng, lnb, ccw, alog, dtb, st_a, st_b, st_q):
    n = x2d.shape[0]
    args = (x2d, g1, wmain, wsmall, caw, cbw, lng, lnb, ccw, alog, dtb, st_a, st_b, st_q)
    out_widths = (D_A + D_B, D_C, D_C, D_C, D_C, LANES, D_A, D_B, 3 * D_C)
    return pl.pallas_call(
        _inproj_sample_kernel,
        grid=(1,),
        in_specs=[_const_spec(a.shape) for a in args],
        out_specs=tuple(_const_spec((n, w)) for w in out_widths),
        out_shape=tuple(jax.ShapeDtypeStruct((n, w), F32) for w in out_widths),
        compiler_params=pltpu.CompilerParams(dimension_semantics=("arbitrary",), vmem_limit_bytes=VMEM_LIMIT),
        name="inproj_sample",
    )(*args)


def _delta_sample_kernel(q_ref, k_ref, v_ref, gb_ref, gate_ref, ong_ref, s_ref, yc_ref, snew_ref, *, nb):
    q = q_ref[...]
    k = k_ref[...]
    rows = [k[:, h * DK:(h + 1) * DK] for h in range(H_C)] + [q[:, h * DK:(h + 1) * DK] for h in range(H_C)]
    rows.append(jnp.zeros((LANES - 2 * H_C * nb, DK), F32))
    kq_t = jnp.concatenate(rows, axis=0).T
    gb = gb_ref[...]
    for i in range(nb):
        for h in range(H_C):
            cols = slice(h * DK, (h + 1) * DK)
            s = s_ref[i, h]
            k_col = kq_t[:, h * nb + i:h * nb + i + 1]
            q_col = kq_t[:, (H_C + h) * nb + i:(H_C + h) * nb + i + 1]
            k_s = jnp.sum(s * k_col, axis=0, keepdims=True)
            q_s = jnp.sum(s * q_col, axis=0, keepdims=True)
            e_g = jnp.exp(gb[i:i + 1, h:h + 1])
            beta = gb[i:i + 1, H_C + h:H_C + h + 1]
            v_new = beta * (v_ref[i:i + 1, cols] - e_g * k_s)
            qk = jnp.sum(q[i:i + 1, cols] * k[i:i + 1, cols], axis=-1, keepdims=True)
            o = e_g * q_s + qk * v_new
            snew_ref[i, h] = e_g * s + k_col * v_new
            yc_ref[i:i + 1, cols] = _gated_norm(o, ong_ref[...], gate_ref[i:i + 1, cols])


def _delta_sample(q, k, v, gb, gate, ong, state, layer):
    n = q.shape[0]
    nb = 8
    row = lambda w: pl.BlockSpec((nb, w), lambda i: (i, 0))
    st_in = pl.BlockSpec((None, nb, H_C, DK, DV), lambda i: (layer, i, 0, 0, 0))
    st = pl.BlockSpec((nb, H_C, DK, DV), lambda i: (i, 0, 0, 0))
    return pl.pallas_call(
        functools.partial(_delta_sample_kernel, nb=nb),
        grid=(n // nb,),
        in_specs=[row(D_C), row(D_C), row(D_C), row(LANES), row(D_C), _const_spec(ong.shape), st_in],
        out_specs=(row(D_C), st),
        out_shape=(jax.ShapeDtypeStruct((n, D_C), F32), jax.ShapeDtypeStruct(state.shape[1:], F32)),
        compiler_params=pltpu.CompilerParams(dimension_semantics=("arbitrary",), vmem_limit_bytes=VMEM_LIMIT),
        name="delta_sample",
    )(q, k, v, gb, gate, ong, state)


def _outproj_router_kernel(x_ref, yab_ref, yc_ref, woa_ref, woc_ref, g2_ref, wrt_ref,
                           x1_ref, h2_ref, comb_ref, route_ref, cnt_ref):
    x1 = (x_ref[...]
          + jnp.dot(yab_ref[...].astype(BF16), woa_ref[...], preferred_element_type=F32)
          + jnp.dot(yc_ref[...].astype(BF16), woc_ref[...], preferred_element_type=F32))
    x1_ref[...] = x1
    h2 = _rms(x1, g2_ref[...])
    h2_ref[...] = h2.astype(BF16)
    logits = _dot_3pass(h2, h2.astype(BF16), wrt_ref[...])
    lane = lax.broadcasted_iota(jnp.int32, logits.shape, 1)
    neg = -jnp.inf
    gl = jnp.where(lane < N_GROUPS, logits, neg)
    g_max = jnp.max(gl, axis=-1, keepdims=True)
    g_idx = jnp.min(jnp.where(gl == g_max, lane, LANES), axis=-1, keepdims=True)
    g_p = 1.0 / jnp.sum(jnp.exp(gl - g_max), axis=-1, keepdims=True)
    lo = EXPERT_LANE0 + g_idx * EXP_PER_GROUP
    in_group = (lane >= lo) & (lane < lo + EXP_PER_GROUP)
    el = jnp.where(in_group, logits, neg)
    e_max = jnp.max(el, axis=-1, keepdims=True)
    pe = jnp.exp(el - e_max)
    e_prob = pe / jnp.sum(pe, axis=-1, keepdims=True)
    p1 = jnp.max(e_prob, axis=-1, keepdims=True)
    i1 = jnp.min(jnp.where(e_prob == p1, lane, LANES), axis=-1, keepdims=True)
    rest = jnp.where(in_group & (lane != i1), e_prob, -1.0)
    p2 = jnp.max(rest, axis=-1, keepdims=True)
    i2 = jnp.min(jnp.where(rest == p2, lane, LANES), axis=-1, keepdims=True)
    denom = p1 + p2
    w1 = g_p * (p1 / denom)
    w2 = g_p * (p2 / denom)
    comb_ref[...] = jnp.where(lane == i1, w1, 0.0) + jnp.where(lane == i2, w2, 0.0)
    e1 = i1 - EXPERT_LANE0
    e2 = i2 - EXPERT_LANE0
    route_ref[...] = jnp.where(lane == 0, e1.astype(F32),
                               jnp.where(lane == 1, e2.astype(F32),
                                         jnp.where(lane == 2, w1, jnp.where(lane == 3, w2, 0.0))))
    picked = jnp.where((lane == e1) | (lane == e2), 1.0, 0.0)
    cnt_ref[0] = jnp.broadcast_to(jnp.sum(picked, axis=0, keepdims=True), (8, LANES))


def _outproj_router(x2d, yab, yc, woa, woc, g2, wrt):
    n, d = x2d.shape
    tm = min(TOKEN_TILE, n)
    row = lambda w: pl.BlockSpec((tm, w), lambda i: (i, 0))
    return pl.pallas_call(
        _outproj_router_kernel,
        grid=(n // tm,),
        in_specs=[row(d), row(D_A + D_B), row(D_C), _const_spec(woa.shape), _const_spec(woc.shape),
                  _const_spec(g2.shape), _const_spec(wrt.shape)],
        out_specs=(row(d), row(d), row(LANES), row(LANES), pl.BlockSpec((1, 8, LANES), lambda i: (i, 0, 0))),
        out_shape=(jax.ShapeDtypeStruct((n, d), F32), jax.ShapeDtypeStruct((n, d), BF16),
                   jax.ShapeDtypeStruct((n, LANES), F32), jax.ShapeDtypeStruct((n, LANES), F32),
                   jax.ShapeDtypeStruct((n // tm, 8, LANES), F32)),
        compiler_params=pltpu.CompilerParams(dimension_semantics=("arbitrary",), vmem_limit_bytes=VMEM_LIMIT),
        name="outproj_router",
    )(x2d, yab, yc, woa, woc, g2, wrt)


def _route_plan(cnt, n_tok, tt):
    n_tiles = n_tok // tt
    cnt = cnt[:, 0, :N_EXPERTS].astype(jnp.int32)
    pc = (cnt + PIECE - 1) // PIECE * PIECE
    local_off = jnp.cumsum(pc, axis=1) - pc
    seg_len = jnp.sum(pc, axis=0)
    seg_pad = (seg_len + ROW_TILE - 1) // ROW_TILE * ROW_TILE
    seg_end = jnp.cumsum(seg_pad)
    seg_start = seg_end - seg_pad
    run_off = seg_start[None, :] + jnp.cumsum(pc, axis=0) - pc
    n_row_tiles = _max_sorted_rows(n_tok, tt) // ROW_TILE
    tile_row0 = jnp.arange(n_row_tiles, dtype=jnp.int32) * ROW_TILE
    tile_expert = jnp.minimum(jnp.sum((seg_end[None, :] <= tile_row0[:, None]).astype(jnp.int32), axis=1),
                              N_EXPERTS - 1)
    n_active = (seg_end[-1] // ROW_TILE).reshape(1)
    return dict(
        run_off=run_off.reshape(-1), local_off=local_off.reshape(-1), n_pieces=(pc // PIECE).reshape(-1),
        tile_pieces=jnp.sum(pc, axis=1) // PIECE,
        fill_off=seg_start + seg_len, fill_pieces=(seg_pad - seg_len) // PIECE,
        tile_expert=tile_expert, n_active=n_active,
        local_off_lanes=jnp.broadcast_to(
            jnp.pad(local_off.astype(F32), ((0, 0), (0, LANES - N_EXPERTS)))[:, None, :], (n_tiles, 8, LANES)),
    )


def _max_sorted_rows(n_tok, tt):
    n_tiles = n_tok // tt
    rows = 2 * n_tok + n_tiles * N_EXPERTS * (PIECE - 1) + N_EXPERTS * (ROW_TILE - 1)
    return (rows + ROW_TILE - 1) // ROW_TILE * ROW_TILE


def _local_rows(tt):
    return 2 * tt + N_EXPERTS * PIECE


def _local_positions(route, loff_lanes, tt):
    lane = lax.broadcasted_iota(jnp.int32, (tt, LANES), 1).astype(F32)
    oh1 = lane == route[:, 0:1]
    oh2 = lane == route[:, 1:2]
    picked = jnp.where(oh1 | oh2, 1.0, 0.0).astype(BF16)
    ri = lax.broadcasted_iota(jnp.int32, (tt, tt), 0)
    ci = lax.broadcasted_iota(jnp.int32, (tt, tt), 1)
    earlier = jnp.where(ri > ci, 1.0, 0.0).astype(BF16)
    base = jnp.dot(earlier, picked, preferred_element_type=F32) + loff_lanes
    pos1 = jnp.sum(jnp.where(oh1, base, 0.0), axis=-1, keepdims=True)
    pos2 = jnp.sum(jnp.where(oh2, base, 0.0), axis=-1, keepdims=True)
    return pos1, pos2


def _piece_copy(src, dst, sem):
    return pltpu.make_async_copy(src, dst, sem)


def _dispatch_kernel(run_off, local_off, n_pieces, tile_pieces, fill_off, fill_pieces, n_active,
                     h2_ref, route_ref, loff_ref, xs_hbm, lbuf, zbuf, sem, *, tt, n_tiles, n_row_tiles):
    i = pl.program_id(0)
    slot = i % 2
    lrows = _local_rows(tt)

    def wait_tile(tile, s):
        def body(_, c):
            _piece_copy(lbuf.at[s, pl.ds(0, PIECE)], xs_hbm.at[pl.ds(0, PIECE)], sem.at[s]).wait()
            return c
        lax.fori_loop(0, tile_pieces[tile], body, 0)

    @pl.when(i >= 2)
    def _():
        wait_tile(i - 2, slot)

    route = route_ref[...]
    pos1, pos2 = _local_positions(route, loff_ref[0, 0:1, :], tt)
    lane = lax.broadcasted_iota(jnp.int32, (tt, LANES), 1)
    packed = jnp.where(lane == 0, pos1, jnp.where(lane == 1, pos2, 0.0)).T
    row = lax.broadcasted_iota(jnp.int32, (lrows, tt), 0).astype(F32)
    perm = jnp.where((row == packed[0:1, :]) | (row == packed[1:2, :]), 1.0, 0.0).astype(BF16)
    lbuf[slot] = jnp.dot(perm, h2_ref[...], preferred_element_type=F32).astype(BF16)

    for e in range(N_EXPERTS):
        src0 = local_off[i * N_EXPERTS + e]
        dst0 = run_off[i * N_EXPERTS + e]

        def body(p, c, src0=src0, dst0=dst0):
            _piece_copy(lbuf.at[slot, pl.ds(pl.multiple_of(src0 + p * PIECE, PIECE), PIECE)],
                        xs_hbm.at[pl.ds(pl.multiple_of(dst0 + p * PIECE, PIECE), PIECE)], sem.at[slot]).start()
            return c
        lax.fori_loop(0, n_pieces[i * N_EXPERTS + e], body, 0)

    @pl.when(i == n_tiles - 1)
    def _():
        zbuf[...] = jnp.zeros(zbuf.shape, BF16)
        zpiece = zbuf.at[pl.ds(0, PIECE)]
        for e in range(N_EXPERTS):
            def body(p, c, e=e):
                _piece_copy(zpiece, xs_hbm.at[pl.ds(pl.multiple_of(fill_off[e] + p * PIECE, PIECE), PIECE)],
                            sem.at[2]).start()
                return c
            lax.fori_loop(0, fill_pieces[e], body, 0)

        def tail_body(j, c):
            _piece_copy(zbuf, xs_hbm.at[pl.ds(pl.multiple_of(j * ROW_TILE, ROW_TILE), ROW_TILE)], sem.at[3]).start()
            return c
        lax.fori_loop(n_active[0], n_row_tiles, tail_body, 0)
        for e in range(N_EXPERTS):
            def body(p, c):
                _piece_copy(zpiece, xs_hbm.at[pl.ds(0, PIECE)], sem.at[2]).wait()
                return c
            lax.fori_loop(0, fill_pieces[e], body, 0)

        def tail_wait(j, c):
            _piece_copy(zbuf, xs_hbm.at[pl.ds(0, ROW_TILE)], sem.at[3]).wait()
            return c
        lax.fori_loop(n_active[0], n_row_tiles, tail_wait, 0)
        if n_tiles > 1:
            wait_tile(i - 1, 1 - slot)
        wait_tile(i, slot)


def _dispatch(h2, route, plan, tt):
    n, d = h2.shape
    n_tiles = n // tt
    n_rows = _max_sorted_rows(n, tt)
    kernel_fn = functools.partial(_dispatch_kernel, tt=tt, n_tiles=n_tiles, n_row_tiles=n_rows // ROW_TILE)
    grid_spec = pltpu.PrefetchScalarGridSpec(
        num_scalar_prefetch=7,
        grid=(n_tiles,),
        in_specs=[pl.BlockSpec((tt, d), lambda i, *_: (i, 0)),
                  pl.BlockSpec((tt, LANES), lambda i, *_: (i, 0)),
                  pl.BlockSpec((1, 8, LANES), lambda i, *_: (i, 0, 0))],
        out_specs=pl.BlockSpec(memory_space=pl.ANY),
        scratch_shapes=[pltpu.VMEM((2, _local_rows(tt), d), BF16), pltpu.VMEM((ROW_TILE, d), BF16),
                        pltpu.SemaphoreType.DMA((4,))],
    )
    return pl.pallas_call(
        kernel_fn,
        grid_spec=grid_spec,
        out_shape=jax.ShapeDtypeStruct((n_rows, d), BF16),
        compiler_params=pltpu.CompilerParams(dimension_semantics=("arbitrary",), vmem_limit_bytes=VMEM_LIMIT),
        name="moe_dispatch",
    )(plan["run_off"], plan["local_off"], plan["n_pieces"], plan["tile_pieces"], plan["fill_off"],
      plan["fill_pieces"], plan["n_active"], h2, route, plan["local_off_lanes"])


def _experts_kernel(tile_expert, n_active, xs_ref, wg_ref, wu_ref, wd_ref, ys_ref, wg_b, wu_b, wd_b):
    j = pl.program_id(0)
    expert = tile_expert[j]
    previous = tile_expert[jnp.maximum(j - 1, 0)]

    @pl.when((j == 0) | (expert != previous))
    def _():
        wg_b[...] = wg_ref[...].astype(BF16)
        wu_b[...] = wu_ref[...].astype(BF16)
        wd_b[...] = wd_ref[...].astype(BF16)

    @pl.when(j < n_active[0])
    def _():
        x = xs_ref[...]
        gate = jnp.dot(x, wg_b[...], preferred_element_type=F32)
        up = jnp.dot(x, wu_b[...], preferred_element_type=F32)
        act = (_silu(gate) * up).astype(BF16)
        ys_ref[...] = jnp.dot(act, wd_b[...], preferred_element_type=F32).astype(BF16)

    @pl.when(j >= n_active[0])
    def _():
        ys_ref[...] = jnp.zeros(ys_ref.shape, BF16)


def _experts(xs, wg, wu, wd, layer, plan):
    rows, d = xs.shape
    d_e = wg.shape[-1]
    live = lambda j, n_active: jnp.minimum(j, n_active[0] - 1)
    grid_spec = pltpu.PrefetchScalarGridSpec(
        num_scalar_prefetch=2,
        grid=(rows // ROW_TILE,),
        in_specs=[pl.BlockSpec((ROW_TILE, d), lambda j, te, na: (live(j, na), 0)),
                  pl.BlockSpec((None, None, d, d_e), lambda j, te, na: (layer, te[live(j, na)], 0, 0)),
                  pl.BlockSpec((None, None, d, d_e), lambda j, te, na: (layer, te[live(j, na)], 0, 0)),
                  pl.BlockSpec((None, None, d_e, d), lambda j, te, na: (layer, te[live(j, na)], 0, 0))],
        out_specs=pl.BlockSpec((ROW_TILE, d), lambda j, te, na: (j, 0)),
        scratch_shapes=[pltpu.VMEM((d, d_e), BF16), pltpu.VMEM((d, d_e), BF16), pltpu.VMEM((d_e, d), BF16)],
    )
    return pl.pallas_call(
        _experts_kernel,
        grid_spec=grid_spec,
        out_shape=jax.ShapeDtypeStruct((rows, d), BF16),
        compiler_params=pltpu.CompilerParams(dimension_semantics=("arbitrary",), vmem_limit_bytes=VMEM_LIMIT),
        name="moe_experts",
    )(plan["tile_expert"], plan["n_active"], xs, wg, wu, wd)


def _combine_kernel(run_off, local_off, n_pieces, tile_pieces,
                    x1_ref, route_ref, loff_ref, fg_ref, ys_hbm, out_ref, lbuf, sem, *, tt, n_tiles, final_norm):
    i = pl.program_id(0)
    slot = i % 2
    lrows = _local_rows(tt)

    def fetch_tile(tile, s):
        for e in range(N_EXPERTS):
            src0 = run_off[tile * N_EXPERTS + e]
            dst0 = local_off[tile * N_EXPERTS + e]

            def body(p, c, src0=src0, dst0=dst0):
                _piece_copy(ys_hbm.at[pl.ds(pl.multiple_of(src0 + p * PIECE, PIECE), PIECE)],
                            lbuf.at[s, pl.ds(pl.multiple_of(dst0 + p * PIECE, PIECE), PIECE)], sem.at[s]).start()
                return c
            lax.fori_loop(0, n_pieces[tile * N_EXPERTS + e], body, 0)

    @pl.when(i == 0)
    def _():
        lbuf[...] = jnp.zeros(lbuf.shape, BF16)
        fetch_tile(0, 0)

    @pl.when(i + 1 < n_tiles)
    def _():
        fetch_tile(i + 1, 1 - slot)

    def wait_body(_, c):
        _piece_copy(ys_hbm.at[pl.ds(0, PIECE)], lbuf.at[slot, pl.ds(0, PIECE)], sem.at[slot]).wait()
        return c
    lax.fori_loop(0, tile_pieces[i], wait_body, 0)

    route = route_ref[...]
    pos1, pos2 = _local_positions(route, loff_ref[0, 0:1, :], tt)
    col = lax.broadcasted_iota(jnp.int32, (tt, lrows), 1).astype(F32)
    weights = (jnp.where(col == pos1, route[:, 2:3], 0.0) + jnp.where(col == pos2, route[:, 3:4], 0.0)).astype(BF16)
    out = x1_ref[...] + jnp.dot(weights, lbuf[slot], preferred_element_type=F32)
    out_ref[...] = _rms(out, fg_ref[...]) if final_norm else out


def _combine(x1, route, ys, fg, plan, tt, final_norm):
    n, d = x1.shape
    n_tiles = n // tt
    kernel_fn = functools.partial(_combine_kernel, tt=tt, n_tiles=n_tiles, final_norm=final_norm)
    grid_spec = pltpu.PrefetchScalarGridSpec(
        num_scalar_prefetch=4,
        grid=(n_tiles,),
        in_specs=[pl.BlockSpec((tt, d), lambda i, *_: (i, 0)),
                  pl.BlockSpec((tt, LANES), lambda i, *_: (i, 0)),
                  pl.BlockSpec((1, 8, LANES), lambda i, *_: (i, 0, 0)),
                  pl.BlockSpec(fg.shape, lambda i, *_: (0, 0)),
                  pl.BlockSpec(memory_space=pl.ANY)],
        out_specs=pl.BlockSpec((tt, d), lambda i, *_: (i, 0)),
        scratch_shapes=[pltpu.VMEM((2, _local_rows(tt), d), BF16), pltpu.SemaphoreType.DMA((2,))],
    )
    return pl.pallas_call(
        kernel_fn,
        grid_spec=grid_spec,
        out_shape=jax.ShapeDtypeStruct((n, d), F32),
        compiler_params=pltpu.CompilerParams(dimension_semantics=("arbitrary",), vmem_limit_bytes=VMEM_LIMIT),
        name="moe_combine",
    )(plan["run_off"], plan["local_off"], plan["n_pieces"], plan["tile_pieces"], x1, route,
      plan["local_off_lanes"], fg, ys)


def _routed_moe(x1, h2, route, cnt, wg, wu, wd, layer, fg, final_norm):
    n = x1.shape[0]
    tt = min(TOKEN_TILE, n)
    plan = _route_plan(cnt, n, tt)
    xs = _dispatch(h2, route, plan, tt)
    ys = _experts(xs, wg, wu, wd, layer, plan)
    return _combine(x1, route, ys, fg, plan, tt, final_norm)


def _moe_kernel(x1_ref, h2_ref, comb_ref, wg_ref, wu_ref, wd_ref, fg_ref, out_ref, *, final_norm):
    e = pl.program_id(1)

    @pl.when(e == 0)
    def _():
        out_ref[...] = x1_ref[...]

    h2 = h2_ref[...]
    gate = jnp.dot(h2, wg_ref[...].astype(BF16), preferred_element_type=F32)
    up = jnp.dot(h2, wu_ref[...].astype(BF16), preferred_element_type=F32)
    comb = comb_ref[...]
    lane = lax.broadcasted_iota(jnp.int32, comb.shape, 1)
    c_col = jnp.sum(jnp.where(lane == EXPERT_LANE0 + e, comb, 0.0), axis=-1, keepdims=True)
    act = (_silu(gate) * up * c_col).astype(BF16)
    out_ref[...] += jnp.dot(act, wd_ref[...].astype(BF16), preferred_element_type=F32)

    if final_norm:
        @pl.when(e == pl.num_programs(1) - 1)
        def _():
            out_ref[...] = _rms(out_ref[...], fg_ref[...])


def _moe(x1, h2, comb, wg, wu, wd, layer, fg, final_norm):
    n, d = x1.shape
    tm = min(1024, n)
    _, n_e, _, d_e = wg.shape
    row = lambda w: pl.BlockSpec((tm, w), lambda i, e: (i, 0))
    return pl.pallas_call(
        functools.partial(_moe_kernel, final_norm=final_norm),
        grid=(n // tm, n_e),
        in_specs=[row(d), row(d), row(LANES),
                  pl.BlockSpec((None, None, d, d_e), lambda i, e: (layer, e, 0, 0)),
                  pl.BlockSpec((None, None, d, d_e), lambda i, e: (layer, e, 0, 0)),
                  pl.BlockSpec((None, None, d_e, d), lambda i, e: (layer, e, 0, 0)),
                  pl.BlockSpec(fg.shape, lambda i, e: (0, 0))],
        out_specs=row(d),
        out_shape=jax.ShapeDtypeStruct((n, d), F32),
        compiler_params=pltpu.CompilerParams(dimension_semantics=("arbitrary", "arbitrary"),
                                             vmem_limit_bytes=VMEM_LIMIT),
        name="moe",
    )(x1, h2, comb, wg, wu, wd, fg)


def _pad_lanes(a, lane0=0):
    a = a.reshape((1, -1)) if a.ndim == 1 else a
    return jnp.pad(a, ((0, 0), (lane0, LANES - lane0 - a.shape[1])))


def kernel(x_prompt, x_sample, state_conv_a, state_conv_b, state_conv_qkv, state_delta, norm1_g, w_in, conv_a_w,
           conv_b_w, ln_b_g, ln_b_b, conv_c_w, a_log, dt_bias, o_norm_g, w_out, norm2_g, w_group, w_router,
           w_gate, w_up, w_down, final_g):
    n_b, t_len, d = x_prompt.shape
    n_s = x_sample.shape[0]
    depth = w_in.shape[0]
    n_main = 3 * D_A + 2 * D_B + 3 * D_C + D_C
    xp = x_prompt.reshape(n_b * t_len, d)
    xs = x_sample.reshape(n_s, d)
    fg = final_g.reshape(1, d)
    outs = {k: [] for k in ("pa", "pb", "pq", "pd", "sa", "sb", "sq", "sd")}
    for l in range(depth):
        g1 = norm1_g[l].reshape(1, d)
        wmain = w_in[l][:, :n_main].astype(BF16)
        wsmall = _pad_lanes(jnp.concatenate([w_in[l][:, n_main + H_C:], w_in[l][:, n_main:n_main + H_C]], axis=1))
        caw, cbw, ccw = conv_a_w[l], conv_b_w[l], conv_c_w[l]
        lng, lnb = ln_b_g[l].reshape(1, D_B), ln_b_b[l].reshape(1, D_B)
        alog, dtb = _pad_lanes(a_log[l]), _pad_lanes(dt_bias[l])
        ong = o_norm_g[l].reshape(1, DV)
        woa = w_out[l][:D_A + D_B].astype(BF16)
        woc = w_out[l][D_A + D_B:].astype(BF16)
        g2 = norm2_g[l].reshape(1, d)
        wrt = _pad_lanes(jnp.concatenate(
            [w_group[l], w_router[l].transpose(1, 0, 2).reshape(d, N_EXPERTS)], axis=1))
        last = l == depth - 1

        yab, q, k, v, gate, gb, st_a, st_b, st_q = _inproj_prompt(
            xp, n_b, t_len, g1, wmain, wsmall, caw, cbw, lng, lnb, ccw, alog, dtb)
        yc, s_fin = _delta_prompt(q, k, v, gb, gate, ong, n_b, t_len)
        x1, h2, _, route, cnt = _outproj_router(xp, yab, yc, woa, woc, g2, wrt)
        xp = _routed_moe(x1, h2, route, cnt, w_gate, w_up, w_down, l, fg, last)
        outs["pa"].append(st_a); outs["pb"].append(st_b); outs["pq"].append(st_q); outs["pd"].append(s_fin)

        hist_a = state_conv_a[l].transpose(1, 0, 2)
        hist_b = state_conv_b[l].transpose(1, 0, 2)
        hist_q = state_conv_qkv[l].transpose(1, 0, 2)
        yab, q, k, v, gate, gb, un_a, un_b, un_q = _inproj_sample(
            xs, g1, wmain, wsmall, caw, cbw, lng, lnb, ccw, alog, dtb, hist_a, hist_b, hist_q)
        yc, s_new = _delta_sample(q, k, v, gb, gate, ong, state_delta, l)
        x1, h2, comb, _, _ = _outproj_router(xs, yab, yc, woa, woc, g2, wrt)
        xs = _moe(x1, h2, comb, w_gate, w_up, w_down, l, fg, last)
        outs["sa"].append(jnp.concatenate([state_conv_a[l][:, 1:], un_a[:, None]], axis=1))
        outs["sb"].append(jnp.concatenate([state_conv_b[l][:, 1:], un_b[:, None]], axis=1))
        outs["sq"].append(jnp.concatenate([state_conv_qkv[l][:, 1:], un_q[:, None]], axis=1))
        outs["sd"].append(s_new)

    stack = lambda key: jnp.stack(outs[key])
    return (xp.reshape(n_b, t_len, d), xs.reshape(n_s, 1, d),
            stack("pa"), stack("pb"), stack("pq"), stack("pd"),
            stack("sa"), stack("sb"), stack("sq"), stack("sd"))
```

```python
import functools

import jax
import jax.numpy as jnp
from jax import lax
from jax.experimental import pallas as pl
from jax.experimental.pallas import tpu as pltpu

F32 = jnp.float32
BF16 = jnp.bfloat16

EPS = 1e-6
H_C = 4
DK = 128
DV = 128
D_A = 256
D_B = 256
D_C = 512
W_A = 3
W_B = 31
W_C = 4
CHUNK = 64
N_GROUPS = 4
EXP_PER_GROUP = 8
N_EXPERTS = N_GROUPS * EXP_PER_GROUP
LANES = 128
PAD_A = 8
PAD_B = 32
PAD_C = 8
EXPERT_LANE0 = N_GROUPS
VMEM_LIMIT = 56 * 1024 * 1024
TOKEN_TILE = 512
PIECE = 16
ROW_TILE = 512


def _silu(x):
    return x * (1.0 / (1.0 + jnp.exp(-x)))


def _sigmoid(x):
    return 1.0 / (1.0 + jnp.exp(-x))


def _softplus(x):
    return jnp.maximum(x, 0.0) + jnp.log1p(jnp.exp(-jnp.abs(x)))


def _rms(x, g):
    return x * lax.rsqrt(jnp.mean(x * x, axis=-1, keepdims=True) + EPS) * g


def _conv_taps(buf, w_ref, width, pad, n_rows, n_cols, row_blk, col_blk):
    row_parts = []
    for r0 in range(0, n_rows, row_blk):
        col_parts = []
        for c0 in range(0, n_cols, col_blk):
            acc = None
            for j in range(width):
                start = pad - (width - 1) + j + r0
                term = buf[start:start + row_blk, c0:c0 + col_blk] * w_ref[j:j + 1, c0:c0 + col_blk]
                acc = term if acc is None else acc + term
            col_parts.append(acc)
        row_parts.append(col_parts[0] if len(col_parts) == 1 else jnp.concatenate(col_parts, axis=1))
    return row_parts[0] if len(row_parts) == 1 else jnp.concatenate(row_parts, axis=0)


def _conv_taps_realigned(buf, shifted, w_ref, width, pad, n_rows, row_blk):
    first = pad - (width - 1)
    n_shift = n_rows + (first + width - 1) // 8 * 8 - 8
    for r in range(1, 8):
        shifted[r - 1] = buf[r:r + n_shift, :]
    row_parts = []
    for r0 in range(0, n_rows, row_blk):
        acc = None
        for j in range(width):
            q, r = divmod(first + j, 8)
            lo = 8 * q + r0
            window = buf[lo:lo + row_blk, :] if r == 0 else shifted[r - 1, lo:lo + row_blk, :]
            term = window * w_ref[j:j + 1, :]
            acc = term if acc is None else acc + term
        row_parts.append(acc)
    return row_parts[0] if len(row_parts) == 1 else jnp.concatenate(row_parts, axis=0)


def _l2norm_heads(x, scale):
    parts = []
    for h in range(H_C):
        xh = x[:, h * DK:(h + 1) * DK]
        parts.append(xh * (lax.rsqrt(jnp.sum(xh * xh, axis=-1, keepdims=True) + 1e-6) * scale))
    return jnp.concatenate(parts, axis=1)


def _dot_3pass(a, a_hi, w):
    a_lo = (a - a_hi.astype(F32)).astype(BF16)
    w_hi = w.astype(BF16)
    w_lo = (w - w_hi.astype(F32)).astype(BF16)
    n = w.shape[1]
    both = jnp.dot(a_hi, jnp.concatenate([w_hi, w_lo], axis=1), preferred_element_type=F32)
    return both[:, :n] + both[:, n:] + jnp.dot(a_lo, w_hi, preferred_element_type=F32)


def _in_projection(x, g1_ref, wmain_ref, wsmall_ref):
    h = _rms(x, g1_ref[...])
    hb = h.astype(BF16)
    z_a = jnp.dot(hb, wmain_ref[:, 0:768], preferred_element_type=F32)
    z_b = jnp.dot(hb, wmain_ref[:, 768:1280], preferred_element_type=F32)
    z_qkv = jnp.dot(hb, wmain_ref[:, 1280:2816], preferred_element_type=F32)
    z_gate = jnp.dot(hb, wmain_ref[:, 2816:3328], preferred_element_type=F32)
    z_s = _dot_3pass(h, hb, wsmall_ref[...])
    return z_a, z_b, z_qkv, z_gate, z_s


def _layer_norm_silu(x, g, b):
    mu = jnp.mean(x, axis=-1, keepdims=True)
    xc = x - mu
    y = xc * lax.rsqrt(jnp.mean(xc * xc, axis=-1, keepdims=True) + EPS)
    return _silu(y * g + b)


def _decay_and_beta(z_s, alog_ref, dtb_ref):
    lane = lax.broadcasted_iota(jnp.int32, z_s.shape, 1)
    g = -jnp.exp(alog_ref[...]) * _softplus(z_s + dtb_ref[...])
    g = jnp.where(lane < H_C, g, 0.0)
    beta = jnp.where((lane >= H_C) & (lane < 2 * H_C), _sigmoid(z_s), 0.0)
    return g, beta


def _inproj_prompt_kernel(x_ref, g1_ref, wmain_ref, wsmall_ref, caw_ref, cbw_ref, lng_ref, lnb_ref,
                          ccw_ref, alog_ref, dtb_ref,
                          yab_ref, q_ref, k_ref, v_ref, gate_ref, gb_ref, sta_ref, stb_ref, stq_ref,
                          abuf, bbuf, cbuf, bshift, *, tt, nt):
    t = pl.program_id(1)
    z_a, z_b, z_qkv, z_gate, z_s = _in_projection(x_ref[...], g1_ref, wmain_ref, wsmall_ref)

    @pl.when(t == 0)
    def _():
        abuf[0:PAD_A, :] = jnp.zeros((PAD_A, D_A), F32)
        bbuf[0:PAD_B, :] = jnp.zeros((PAD_B, D_B), F32)
        cbuf[0:PAD_C, :] = jnp.zeros((PAD_C, 3 * D_C), F32)

    abuf[PAD_A:PAD_A + tt, :] = z_a[:, 256:512] * z_a[:, 512:768]
    conv_a = _conv_taps(abuf, caw_ref, W_A, PAD_A, tt, D_A, 64, 256)
    yab_ref[:, 0:D_A] = z_a[:, 0:256] * conv_a

    bbuf[PAD_B:PAD_B + tt, :] = z_b[:, 0:256] * _sigmoid(z_b[:, 256:512])
    conv_b = _conv_taps_realigned(bbuf, bshift, cbw_ref, W_B, PAD_B, tt, 64)
    yab_ref[:, D_A:D_A + D_B] = _layer_norm_silu(conv_b, lng_ref[...], lnb_ref[...])

    cbuf[PAD_C:PAD_C + tt, :] = z_qkv
    qkv = _silu(_conv_taps(cbuf, ccw_ref, W_C, PAD_C, tt, 3 * D_C, 64, 512))
    q_ref[...] = _l2norm_heads(qkv[:, 0:D_C], DK ** -0.5)
    k_ref[...] = _l2norm_heads(qkv[:, D_C:2 * D_C], 1.0)
    v_ref[...] = qkv[:, 2 * D_C:3 * D_C]
    gate_ref[...] = z_gate

    g, beta = _decay_and_beta(z_s, alog_ref, dtb_ref)
    row_in_chunk = lax.broadcasted_iota(jnp.int32, g.shape, 0) & (CHUNK - 1)
    shift = 1
    while shift < CHUNK:
        g = g + jnp.where(row_in_chunk >= shift, pltpu.roll(g, shift, axis=0), 0.0)
        shift *= 2
    gb_ref[...] = g + beta

    @pl.when(t == nt - 1)
    def _():
        sta_ref[0] = abuf[PAD_A + tt - (W_A - 1):PAD_A + tt, :]
        stb_ref[0] = bbuf[PAD_B + tt - (W_B - 1):PAD_B + tt, :]
        stq_ref[0] = cbuf[PAD_C + tt - (W_C - 1):PAD_C + tt, :]

    abuf[0:PAD_A, :] = abuf[tt:tt + PAD_A, :]
    bbuf[0:PAD_B, :] = bbuf[tt:tt + PAD_B, :]
    cbuf[0:PAD_C, :] = cbuf[tt:tt + PAD_C, :]


def _const_spec(shape):
    nd = len(shape)
    return pl.BlockSpec(shape, lambda *_: (0,) * nd)


def _inproj_prompt(x2d, n_b, t_len, g1, wmain, wsmall, caw, cbw, lng, lnb, ccw, alog, dtb):
    tt = min(512, t_len)
    nt = t_len // tt
    n_tok = n_b * t_len
    d = x2d.shape[1]
    row = lambda w: pl.BlockSpec((tt, w), lambda b, t: (b * nt + t, 0))
    st = lambda r, w: pl.BlockSpec((1, r, w), lambda b, t: (b, 0, 0))
    out_shape = (
        jax.ShapeDtypeStruct((n_tok, D_A + D_B), F32),
        jax.ShapeDtypeStruct((n_tok, D_C), F32),
        jax.ShapeDtypeStruct((n_tok, D_C), F32),
        jax.ShapeDtypeStruct((n_tok, D_C), F32),
        jax.ShapeDtypeStruct((n_tok, D_C), F32),
        jax.ShapeDtypeStruct((n_tok, LANES), F32),
        jax.ShapeDtypeStruct((n_b, W_A - 1, D_A), F32),
        jax.ShapeDtypeStruct((n_b, W_B - 1, D_B), F32),
        jax.ShapeDtypeStruct((n_b, W_C - 1, 3 * D_C), F32),
    )
    return pl.pallas_call(
        functools.partial(_inproj_prompt_kernel, tt=tt, nt=nt),
        grid=(n_b, nt),
        in_specs=[row(d), _const_spec(g1.shape), _const_spec(wmain.shape), _const_spec(wsmall.shape),
                  _const_spec(caw.shape), _const_spec(cbw.shape), _const_spec(lng.shape), _const_spec(lnb.shape),
                  _const_spec(ccw.shape), _const_spec(alog.shape), _const_spec(dtb.shape)],
        out_specs=(row(D_A + D_B), row(D_C), row(D_C), row(D_C), row(D_C), row(LANES),
                   st(W_A - 1, D_A), st(W_B - 1, D_B), st(W_C - 1, 3 * D_C)),
        out_shape=out_shape,
        scratch_shapes=[pltpu.VMEM((PAD_A + tt, D_A), F32), pltpu.VMEM((PAD_B + tt, D_B), F32),
                        pltpu.VMEM((PAD_C + tt, 3 * D_C), F32), pltpu.VMEM((7, PAD_B + tt - 8, D_B), F32)],
        compiler_params=pltpu.CompilerParams(dimension_semantics=("arbitrary", "arbitrary"),
                                             vmem_limit_bytes=VMEM_LIMIT),
        name="inproj_prompt",
    )(x2d, g1, wmain, wsmall, caw, cbw, lng, lnb, ccw, alog, dtb)


def _dot_nt(a, b):
    return lax.dot_general(a, b, (((1,), (1,)), ((), ())), preferred_element_type=F32)


def _dot_tn(a, b):
    return lax.dot_general(a, b, (((0,), (0,)), ((), ())), preferred_element_type=F32)


def _bdot(a, b):
    return jnp.dot(a.astype(BF16), b.astype(BF16), preferred_element_type=F32)


def _gated_norm(o, on_g, gate):
    return o * lax.rsqrt(jnp.mean(o * o, axis=-1, keepdims=True) + EPS) * on_g * _silu(gate)


def _delta_prompt_kernel(q_ref, k_ref, v_ref, gb_ref, gate_ref, ong_ref, yc_ref, sfin_ref, s_scr, *, tq, nt):
    t = pl.program_id(1)

    @pl.when(t == 0)
    def _():
        s_scr[...] = jnp.zeros(s_scr.shape, F32)

    c = CHUNK
    n_chunks = tq // c
    n_double = c.bit_length() - 2
    inst = [(ch, h) for ch in range(n_chunks) for h in range(H_C)]
    rows = lambda ch: slice(ch * c, (ch + 1) * c)
    cols = lambda h: slice(h * DK, (h + 1) * DK)
    ri = lax.broadcasted_iota(jnp.int32, (c, c), 0)
    ci = lax.broadcasted_iota(jnp.int32, (c, c), 1)
    gb = gb_ref[...]

    q = [q_ref[rows(ch), cols(h)] for ch, h in inst]
    k = [k_ref[rows(ch), cols(h)] for ch, h in inst]
    v = [v_ref[rows(ch), cols(h)] for ch, h in inst]
    g_col = [gb[rows(ch), h:h + 1] for ch, h in inst]
    b_col = [gb[rows(ch), H_C + h:H_C + h + 1] for ch, h in inst]
    g_row = [jnp.sum(jnp.where(ri == ci, g, 0.0), axis=0, keepdims=True) for g in g_col]
    diff = [gc - gr for gc, gr in zip(g_col, g_row)]
    dec_strict = [jnp.exp(jnp.where(ri > ci, d, -jnp.inf)) for d in diff]
    dec_incl = [jnp.exp(jnp.where(ri >= ci, d, -jnp.inf)) for d in diff]
    e_g = [jnp.exp(g) for g in g_col]
    g_last = [g[c - 1:c, :] for g in g_col]

    qkk = [_dot_nt(jnp.concatenate([qi, ki], axis=0).astype(BF16), ki.astype(BF16)) for qi, ki in zip(q, k)]
    qk = [m[:c] * d for m, d in zip(qkk, dec_incl)]
    p = [-(b * m[c:] * d) for b, m, d in zip(b_col, qkk, dec_strict)]
    sol = [jnp.concatenate([vi * b, ki * (b * eg)], axis=1) for vi, ki, b, eg in zip(v, k, b_col, e_g)]
    for level in range(n_double + 1):
        sol = [s + _bdot(pi, s) for pi, s in zip(p, sol)]
        if level < n_double:
            p = [_bdot(pi, pi) for pi in p]
    kd = [ki * jnp.exp(gl - g) for ki, gl, g in zip(k, g_last, g_col)]
    qk_uw = [_bdot(m, s) for m, s in zip(qk, sol)]
    kd_uw = [_dot_tn(x.astype(BF16), s.astype(BF16)) for x, s in zip(kd, sol)]
    lhs = [jnp.concatenate([qi * eg - a[:, DV:], -b[:, DV:]], axis=0).astype(BF16)
           for qi, eg, a, b in zip(q, e_g, qk_uw, kd_uw)]
    decay = [jnp.exp(gl) for gl in g_last]

    state = [s_scr[h] for h in range(H_C)]
    for ch in range(n_chunks):
        base = ch * H_C
        r = [jnp.dot(lhs[base + h], state[h].astype(BF16), preferred_element_type=F32) for h in range(H_C)]
        for h in range(H_C):
            o = r[h][:c] + qk_uw[base + h][:, :DV]
            yc_ref[rows(ch), cols(h)] = _gated_norm(o, ong_ref[...], gate_ref[rows(ch), cols(h)])
        state = [decay[base + h] * state[h] + r[h][c:] + kd_uw[base + h][:, :DV] for h in range(H_C)]
    for h in range(H_C):
        s_scr[h] = state[h]

    @pl.when(t == nt - 1)
    def _():
        sfin_ref[0] = s_scr[...]


def _delta_prompt(q, k, v, gb, gate, ong, n_b, t_len):
    tq = min(256, t_len)
    nt = t_len // tq
    row = lambda w: pl.BlockSpec((tq, w), lambda b, t: (b * nt + t, 0))
    return pl.pallas_call(
        functools.partial(_delta_prompt_kernel, tq=tq, nt=nt),
        grid=(n_b, nt),
        in_specs=[row(D_C), row(D_C), row(D_C), row(LANES), row(D_C), _const_spec(ong.shape)],
        out_specs=(row(D_C), pl.BlockSpec((1, H_C, DK, DV), lambda b, t: (b, 0, 0, 0))),
        out_shape=(jax.ShapeDtypeStruct((n_b * t_len, D_C), F32),
                   jax.ShapeDtypeStruct((n_b, H_C, DK, DV), F32)),
        scratch_shapes=[pltpu.VMEM((H_C, DK, DV), F32)],
        compiler_params=pltpu.CompilerParams(dimension_semantics=("arbitrary", "arbitrary"),
                                             vmem_limit_bytes=VMEM_LIMIT),
        name="delta_prompt",
    )(q, k, v, gb, gate, ong)


def _inproj_sample_kernel(x_ref, g1_ref, wmain_ref, wsmall_ref, caw_ref, cbw_ref, lng_ref, lnb_ref,
                          ccw_ref, alog_ref, dtb_ref, sta_ref, stb_ref, stq_ref,
                          yab_ref, q_ref, k_ref, v_ref, gate_ref, gb_ref, una_ref, unb_ref, unq_ref):
    z_a, z_b, z_qkv, z_gate, z_s = _in_projection(x_ref[...], g1_ref, wmain_ref, wsmall_ref)

    def conv_step(state_ref, new, w_ref, width):
        acc = new * w_ref[width - 1:width, :]
        for j in range(width - 1):
            acc = acc + state_ref[j] * w_ref[j:j + 1, :]
        return acc

    u_a = z_a[:, 256:512] * z_a[:, 512:768]
    una_ref[...] = u_a
    yab_ref[:, 0:D_A] = z_a[:, 0:256] * conv_step(sta_ref, u_a, caw_ref, W_A)

    u_b = z_b[:, 0:256] * _sigmoid(z_b[:, 256:512])
    unb_ref[...] = u_b
    yab_ref[:, D_A:D_A + D_B] = _layer_norm_silu(conv_step(stb_ref, u_b, cbw_ref, W_B), lng_ref[...], lnb_ref[...])

    unq_ref[...] = z_qkv
    qkv = _silu(conv_step(stq_ref, z_qkv, ccw_ref, W_C))
    q_ref[...] = _l2norm_heads(qkv[:, 0:D_C], DK ** -0.5)
    k_ref[...] = _l2norm_heads(qkv[:, D_C:2 * D_C], 1.0)
    v_ref[...] = qkv[:, 2 * D_C:3 * D_C]
    gate_ref[...] = z_gate
    g, beta = _decay_and_beta(z_s, alog_ref, dtb_ref)
    gb_ref[...] = g + beta


def _inproj_sample(x2d, g1, wmain, wsmall, caw, cbw, lng, lnb, ccw, alog, dtb, st_a, st_b, st_q):
    n = x2d.shape[0]
    args = (x2d, g1, wmain, wsmall, caw, cbw, lng, lnb, ccw, alog, dtb, st_a, st_b, st_q)
    out_widths = (D_A + D_B, D_C, D_C, D_C, D_C, LANES, D_A, D_B, 3 * D_C)
    return pl.pallas_call(
        _inproj_sample_kernel,
        grid=(1,),
        in_specs=[_const_spec(a.shape) for a in args],
        out_specs=tuple(_const_spec((n, w)) for w in out_widths),
        out_shape=tuple(jax.ShapeDtypeStruct((n, w), F32) for w in out_widths),
        compiler_params=pltpu.CompilerParams(dimension_semantics=("arbitrary",), vmem_limit_bytes=VMEM_LIMIT),
        name="inproj_sample",
    )(*args)


def _delta_sample_kernel(q_ref, k_ref, v_ref, gb_ref, gate_ref, ong_ref, s_ref, yc_ref, snew_ref, *, nb):
    q = q_ref[...]
    k = k_ref[...]
    rows = [k[:, h * DK:(h + 1) * DK] for h in range(H_C)] + [q[:, h * DK:(h + 1) * DK] for h in range(H_C)]
    rows.append(jnp.zeros((LANES - 2 * H_C * nb, DK), F32))
    kq_t = jnp.concatenate(rows, axis=0).T
    gb = gb_ref[...]
    inst = [(i, h) for i in range(nb) for h in range(H_C)]
    cols = lambda h: slice(h * DK, (h + 1) * DK)
    k_bc = [jnp.broadcast_to(kq_t[:, h * nb + i:h * nb + i + 1], (DK, DV)) for i, h in inst]
    q_bc = [jnp.broadcast_to(kq_t[:, (H_C + h) * nb + i:(H_C + h) * nb + i + 1], (DK, DV)) for i, h in inst]
    k_s = [jnp.sum(s_ref[i, h] * kb, axis=0, keepdims=True) for (i, h), kb in zip(inst, k_bc)]
    q_s = [jnp.sum(s_ref[i, h] * qb, axis=0, keepdims=True) for (i, h), qb in zip(inst, q_bc)]
    e_g = [jnp.exp(gb[i:i + 1, h:h + 1]) for i, h in inst]
    v_new = [gb[i:i + 1, H_C + h:H_C + h + 1] * (v_ref[i:i + 1, cols(h)] - eg * ks)
             for (i, h), eg, ks in zip(inst, e_g, k_s)]
    qk = [jnp.sum(q[i:i + 1, cols(h)] * k[i:i + 1, cols(h)], axis=-1, keepdims=True) for i, h in inst]
    for n, (i, h) in enumerate(inst):
        snew_ref[i, h] = e_g[n] * s_ref[i, h] + k_bc[n] * v_new[n]
    for n, (i, h) in enumerate(inst):
        o = e_g[n] * q_s[n] + qk[n] * v_new[n]
        yc_ref[i:i + 1, cols(h)] = _gated_norm(o, ong_ref[...], gate_ref[i:i + 1, cols(h)])


def _delta_sample(q, k, v, gb, gate, ong, state, layer):
    n = q.shape[0]
    nb = 8
    row = lambda w: pl.BlockSpec((nb, w), lambda i: (i, 0))
    st_in = pl.BlockSpec((None, nb, H_C, DK, DV), lambda i: (layer, i, 0, 0, 0))
    st = pl.BlockSpec((nb, H_C, DK, DV), lambda i: (i, 0, 0, 0))
    return pl.pallas_call(
        functools.partial(_delta_sample_kernel, nb=nb),
        grid=(n // nb,),
        in_specs=[row(D_C), row(D_C), row(D_C), row(LANES), row(D_C), _const_spec(ong.shape), st_in],
        out_specs=(row(D_C), st),
        out_shape=(jax.ShapeDtypeStruct((n, D_C), F32), jax.ShapeDtypeStruct(state.shape[1:], F32)),
        compiler_params=pltpu.CompilerParams(dimension_semantics=("arbitrary",), vmem_limit_bytes=VMEM_LIMIT),
        name="delta_sample",
    )(q, k, v, gb, gate, ong, state)


def _outproj_router_kernel(x_ref, yab_ref, yc_ref, woa_ref, woc_ref, g2_ref, wrt_ref,
                           x1_ref, h2_ref, comb_ref, route_ref, routet_ref, cnt_ref):
    x1 = (x_ref[...]
          + jnp.dot(yab_ref[...].astype(BF16), woa_ref[...], preferred_element_type=F32)
          + jnp.dot(yc_ref[...].astype(BF16), woc_ref[...], preferred_element_type=F32))
    x1_ref[...] = x1
    h2 = _rms(x1, g2_ref[...])
    h2_ref[...] = h2.astype(BF16)
    logits = _dot_3pass(h2, h2.astype(BF16), wrt_ref[...])
    lane = lax.broadcasted_iota(jnp.int32, logits.shape, 1)
    neg = -jnp.inf
    gl = jnp.where(lane < N_GROUPS, logits, neg)
    g_max = jnp.max(gl, axis=-1, keepdims=True)
    g_idx = jnp.min(jnp.where(gl == g_max, lane, LANES), axis=-1, keepdims=True)
    g_p = 1.0 / jnp.sum(jnp.exp(gl - g_max), axis=-1, keepdims=True)
    lo = EXPERT_LANE0 + g_idx * EXP_PER_GROUP
    in_group = (lane >= lo) & (lane < lo + EXP_PER_GROUP)
    el = jnp.where(in_group, logits, neg)
    e_max = jnp.max(el, axis=-1, keepdims=True)
    pe = jnp.exp(el - e_max)
    e_prob = pe / jnp.sum(pe, axis=-1, keepdims=True)
    p1 = jnp.max(e_prob, axis=-1, keepdims=True)
    i1 = jnp.min(jnp.where(e_prob == p1, lane, LANES), axis=-1, keepdims=True)
    rest = jnp.where(in_group & (lane != i1), e_prob, -1.0)
    p2 = jnp.max(rest, axis=-1, keepdims=True)
    i2 = jnp.min(jnp.where(rest == p2, lane, LANES), axis=-1, keepdims=True)
    denom = p1 + p2
    w1 = g_p * (p1 / denom)
    w2 = g_p * (p2 / denom)
    comb_ref[...] = jnp.where(lane == i1, w1, 0.0) + jnp.where(lane == i2, w2, 0.0)
    e1 = i1 - EXPERT_LANE0
    e2 = i2 - EXPERT_LANE0
    oh1 = lane == e1
    oh2 = lane == e2
    picked = jnp.where(oh1 | oh2, 1.0, 0.0)
    cnt = jnp.broadcast_to(jnp.sum(picked, axis=0, keepdims=True), (8, LANES))
    cnt_ref[0] = cnt
    ki = lax.broadcasted_iota(jnp.int32, (LANES, LANES), 0)
    kj = lax.broadcasted_iota(jnp.int32, (LANES, LANES), 1)
    pieces = jnp.ceil(cnt * (1.0 / PIECE)).astype(BF16)
    run_start = PIECE * jnp.dot(pieces, jnp.where(ki < kj, 1.0, 0.0).astype(BF16),
                                preferred_element_type=F32)[0:1, :]
    tm = picked.shape[0]
    ri = lax.broadcasted_iota(jnp.int32, (tm, tm), 0)
    ci = lax.broadcasted_iota(jnp.int32, (tm, tm), 1)
    base = jnp.dot(jnp.where(ri > ci, 1.0, 0.0).astype(BF16), picked.astype(BF16),
                   preferred_element_type=F32) + run_start
    pos1 = jnp.sum(jnp.where(oh1, base, 0.0), axis=-1, keepdims=True)
    pos2 = jnp.sum(jnp.where(oh2, base, 0.0), axis=-1, keepdims=True)
    route = jnp.where(lane == 0, e1.astype(F32), jnp.where(lane == 1, e2.astype(F32), jnp.where(
        lane == 2, w1, jnp.where(lane == 3, w2, jnp.where(lane == 4, pos1, jnp.where(lane == 5, pos2, 0.0))))))
    route_ref[...] = route
    routet_ref[0] = route.T[0:8, :]


def _outproj_router(x2d, yab, yc, woa, woc, g2, wrt):
    n, d = x2d.shape
    tm = min(TOKEN_TILE, n)
    row = lambda w: pl.BlockSpec((tm, w), lambda i: (i, 0))
    return pl.pallas_call(
        _outproj_router_kernel,
        grid=(n // tm,),
        in_specs=[row(d), row(D_A + D_B), row(D_C), _const_spec(woa.shape), _const_spec(woc.shape),
                  _const_spec(g2.shape), _const_spec(wrt.shape)],
        out_specs=(row(d), row(d), row(LANES), row(LANES), pl.BlockSpec((1, 8, tm), lambda i: (i, 0, 0)),
                   pl.BlockSpec((1, 8, LANES), lambda i: (i, 0, 0))),
        out_shape=(jax.ShapeDtypeStruct((n, d), F32), jax.ShapeDtypeStruct((n, d), BF16),
                   jax.ShapeDtypeStruct((n, LANES), F32), jax.ShapeDtypeStruct((n, LANES), F32),
                   jax.ShapeDtypeStruct((n // tm, 8, tm), F32), jax.ShapeDtypeStruct((n // tm, 8, LANES), F32)),
        compiler_params=pltpu.CompilerParams(dimension_semantics=("arbitrary",), vmem_limit_bytes=VMEM_LIMIT),
        name="outproj_router",
    )(x2d, yab, yc, woa, woc, g2, wrt)


def _route_plan(cnt, n_tok, tt):
    n_tiles = n_tok // tt
    cnt = cnt[:, 0, :N_EXPERTS].astype(jnp.int32)
    pc = (cnt + PIECE - 1) // PIECE * PIECE
    local_end = jnp.cumsum(pc, axis=1)
    local_off = local_end - pc
    seg_len = jnp.sum(pc, axis=0)
    seg_pad = (seg_len + ROW_TILE - 1) // ROW_TILE * ROW_TILE
    seg_end = jnp.cumsum(seg_pad)
    seg_start = seg_end - seg_pad
    run_off = seg_start[None, :] + jnp.cumsum(pc, axis=0) - pc
    piece_row = jnp.arange(_local_rows(tt) // PIECE, dtype=jnp.int32) * PIECE
    piece_expert = jnp.minimum(
        jnp.sum((local_end[:, None, :] <= piece_row[None, :, None]).astype(jnp.int32), axis=-1), N_EXPERTS - 1)
    piece_dst = (jnp.take_along_axis(run_off - local_off, piece_expert, axis=1) + piece_row[None, :])
    n_row_tiles = _max_sorted_rows(n_tok, tt) // ROW_TILE
    tile_row0 = jnp.arange(n_row_tiles, dtype=jnp.int32) * ROW_TILE
    tile_expert = jnp.minimum(jnp.sum((seg_end[None, :] <= tile_row0[:, None]).astype(jnp.int32), axis=1),
                              N_EXPERTS - 1)
    used = seg_pad > 0
    order = jnp.cumsum(used.astype(jnp.int32)) - used.astype(jnp.int32)
    ids = jnp.arange(N_EXPERTS, dtype=jnp.int32)
    later = jnp.where(used[None, :] & (ids[None, :] > ids[:, None]), ids[None, :], N_EXPERTS)
    next_used = jnp.min(later, axis=1)
    return dict(
        piece_dst=piece_dst.reshape(-1), tile_pieces=jnp.sum(pc, axis=1) // PIECE,
        fill_off=seg_start + seg_len, fill_pieces=(seg_pad - seg_len) // PIECE,
        tile_expert=tile_expert, n_active=(seg_end[-1] // ROW_TILE).reshape(1),
        expert_order=order, next_used=next_used,
    )


def _max_sorted_rows(n_tok, tt):
    n_tiles = n_tok // tt
    rows = 2 * n_tok + n_tiles * N_EXPERTS * (PIECE - 1) + N_EXPERTS * (ROW_TILE - 1)
    return (rows + ROW_TILE - 1) // ROW_TILE * ROW_TILE


def _local_rows(tt):
    return 2 * tt + N_EXPERTS * PIECE


def _piece_copy(src, dst, sem):
    return pltpu.make_async_copy(src, dst, sem)


def _dispatch_kernel(piece_dst, tile_pieces, fill_off, fill_pieces, n_active,
                     h2_ref, routet_ref, xs_hbm, lbuf, zbuf, sem, *, tt, n_tiles, n_row_tiles):
    i = pl.program_id(0)
    slot = i % 2
    lrows = _local_rows(tt)
    max_pieces = lrows // PIECE

    def wait_tile(tile, s):
        def body(_, c):
            _piece_copy(lbuf.at[s, pl.ds(0, PIECE)], xs_hbm.at[pl.ds(0, PIECE)], sem.at[s]).wait()
            return c
        lax.fori_loop(0, tile_pieces[tile], body, 0)

    @pl.when(i >= 2)
    def _():
        wait_tile(i - 2, slot)

    pos = routet_ref[0]
    row = lax.broadcasted_iota(jnp.int32, (lrows, tt), 0).astype(F32)
    perm = jnp.where((row == pos[4:5, :]) | (row == pos[5:6, :]), 1.0, 0.0).astype(BF16)
    lbuf[slot] = jnp.dot(perm, h2_ref[...], preferred_element_type=F32).astype(BF16)

    def send(p, c):
        dst = piece_dst[i * max_pieces + p]
        _piece_copy(lbuf.at[slot, pl.ds(pl.multiple_of(p * PIECE, PIECE), PIECE)],
                    xs_hbm.at[pl.ds(pl.multiple_of(dst, PIECE), PIECE)], sem.at[slot]).start()
        return c
    lax.fori_loop(0, tile_pieces[i], send, 0)

    @pl.when(i == n_tiles - 1)
    def _():
        zbuf[...] = jnp.zeros(zbuf.shape, BF16)
        zpiece = zbuf.at[pl.ds(0, PIECE)]
        for e in range(N_EXPERTS):
            def body(p, c, e=e):
                _piece_copy(zpiece, xs_hbm.at[pl.ds(pl.multiple_of(fill_off[e] + p * PIECE, PIECE), PIECE)],
                            sem.at[2]).start()
                return c
            lax.fori_loop(0, fill_pieces[e], body, 0)

        def tail_body(j, c):
            _piece_copy(zbuf, xs_hbm.at[pl.ds(pl.multiple_of(j * ROW_TILE, ROW_TILE), ROW_TILE)], sem.at[3]).start()
            return c
        lax.fori_loop(n_active[0], n_row_tiles, tail_body, 0)
        for e in range(N_EXPERTS):
            def body(p, c):
                _piece_copy(zpiece, xs_hbm.at[pl.ds(0, PIECE)], sem.at[2]).wait()
                return c
            lax.fori_loop(0, fill_pieces[e], body, 0)

        def tail_wait(j, c):
            _piece_copy(zbuf, xs_hbm.at[pl.ds(0, ROW_TILE)], sem.at[3]).wait()
            return c
        lax.fori_loop(n_active[0], n_row_tiles, tail_wait, 0)
        if n_tiles > 1:
            wait_tile(i - 1, 1 - slot)
        wait_tile(i, slot)


def _dispatch(h2, routet, plan, tt):
    n, d = h2.shape
    n_tiles = n // tt
    n_rows = _max_sorted_rows(n, tt)
    kernel_fn = functools.partial(_dispatch_kernel, tt=tt, n_tiles=n_tiles, n_row_tiles=n_rows // ROW_TILE)
    grid_spec = pltpu.PrefetchScalarGridSpec(
        num_scalar_prefetch=5,
        grid=(n_tiles,),
        in_specs=[pl.BlockSpec((tt, d), lambda i, *_: (i, 0)),
                  pl.BlockSpec((1, 8, tt), lambda i, *_: (i, 0, 0))],
        out_specs=pl.BlockSpec(memory_space=pl.ANY),
        scratch_shapes=[pltpu.VMEM((2, _local_rows(tt), d), BF16), pltpu.VMEM((ROW_TILE, d), BF16),
                        pltpu.SemaphoreType.DMA((4,))],
    )
    return pl.pallas_call(
        kernel_fn,
        grid_spec=grid_spec,
        out_shape=jax.ShapeDtypeStruct((n_rows, d), BF16),
        compiler_params=pltpu.CompilerParams(dimension_semantics=("arbitrary",), vmem_limit_bytes=VMEM_LIMIT),
        name="moe_dispatch",
    )(plan["piece_dst"], plan["tile_pieces"], plan["fill_off"], plan["fill_pieces"], plan["n_active"],
      h2, routet)


def _experts_kernel(tile_expert, n_active, expert_order, next_used,
                    xs_ref, wg_hbm, wu_hbm, wd_hbm, ys_ref, wbuf_g, wbuf_u, wbuf_d, wg_b, wu_b, wd_b, sem,
                    *, layer):
    j = pl.program_id(0)
    expert = tile_expert[j]
    previous = tile_expert[jnp.maximum(j - 1, 0)]
    active = j < n_active[0]
    half = expert_order[expert] % 2

    def weight_copies(e, s):
        return (_piece_copy(wg_hbm.at[layer, e], wbuf_g.at[s], sem.at[s, 0]),
                _piece_copy(wu_hbm.at[layer, e], wbuf_u.at[s], sem.at[s, 1]),
                _piece_copy(wd_hbm.at[layer, e], wbuf_d.at[s], sem.at[s, 2]))

    @pl.when(j == 0)
    def _():
        for c in weight_copies(expert, half):
            c.start()

    @pl.when(active & ((j == 0) | (expert != previous)))
    def _():
        for c in weight_copies(expert, half):
            c.wait()
        wg_b[...] = wbuf_g[half].astype(BF16)
        wu_b[...] = wbuf_u[half].astype(BF16)
        wd_b[...] = wbuf_d[half].astype(BF16)
        upcoming = next_used[expert]

        @pl.when(upcoming < N_EXPERTS)
        def _():
            for c in weight_copies(upcoming, 1 - half):
                c.start()

    @pl.when(active)
    def _():
        x = xs_ref[...]
        gate = jnp.dot(x, wg_b[...], preferred_element_type=F32)
        up = jnp.dot(x, wu_b[...], preferred_element_type=F32)
        act = (_silu(gate) * up).astype(BF16)
        ys_ref[...] = jnp.dot(act, wd_b[...], preferred_element_type=F32).astype(BF16)

    @pl.when(jnp.logical_not(active))
    def _():
        ys_ref[...] = jnp.zeros(ys_ref.shape, BF16)


def _experts(xs, wg, wu, wd, layer, plan):
    rows, d = xs.shape
    d_e = wg.shape[-1]
    live = lambda j, te, na, *_: (jnp.minimum(j, na[0] - 1), 0)
    grid_spec = pltpu.PrefetchScalarGridSpec(
        num_scalar_prefetch=4,
        grid=(rows // ROW_TILE,),
        in_specs=[pl.BlockSpec((ROW_TILE, d), live),
                  pl.BlockSpec(memory_space=pl.ANY), pl.BlockSpec(memory_space=pl.ANY),
                  pl.BlockSpec(memory_space=pl.ANY)],
        out_specs=pl.BlockSpec((ROW_TILE, d), lambda j, *_: (j, 0)),
        scratch_shapes=[pltpu.VMEM((2, d, d_e), F32), pltpu.VMEM((2, d, d_e), F32), pltpu.VMEM((2, d_e, d), F32),
                        pltpu.VMEM((d, d_e), BF16), pltpu.VMEM((d, d_e), BF16), pltpu.VMEM((d_e, d), BF16),
                        pltpu.SemaphoreType.DMA((2, 3))],
    )
    return pl.pallas_call(
        functools.partial(_experts_kernel, layer=layer),
        grid_spec=grid_spec,
        out_shape=jax.ShapeDtypeStruct((rows, d), BF16),
        compiler_params=pltpu.CompilerParams(dimension_semantics=("arbitrary",), vmem_limit_bytes=VMEM_LIMIT),
        name="moe_experts",
    )(plan["tile_expert"], plan["n_active"], plan["expert_order"], plan["next_used"], xs, wg, wu, wd)


def _combine_kernel(piece_src, tile_pieces, x1_ref, route_ref, fg_ref, ys_hbm, out_ref, lbuf, sem,
                    *, tt, n_tiles, final_norm):
    i = pl.program_id(0)
    slot = i % 2
    lrows = _local_rows(tt)
    max_pieces = lrows // PIECE

    def fetch_tile(tile, s):
        def body(p, c):
            src = piece_src[tile * max_pieces + p]
            _piece_copy(ys_hbm.at[pl.ds(pl.multiple_of(src, PIECE), PIECE)],
                        lbuf.at[s, pl.ds(pl.multiple_of(p * PIECE, PIECE), PIECE)], sem.at[s]).start()
            return c
        lax.fori_loop(0, tile_pieces[tile], body, 0)

    @pl.when(i == 0)
    def _():
        lbuf[...] = jnp.zeros(lbuf.shape, BF16)
        fetch_tile(0, 0)

    @pl.when(i + 1 < n_tiles)
    def _():
        fetch_tile(i + 1, 1 - slot)

    def wait_body(_, c):
        _piece_copy(ys_hbm.at[pl.ds(0, PIECE)], lbuf.at[slot, pl.ds(0, PIECE)], sem.at[slot]).wait()
        return c
    lax.fori_loop(0, tile_pieces[i], wait_body, 0)

    route = route_ref[...]
    col = lax.broadcasted_iota(jnp.int32, (tt, lrows), 1).astype(F32)
    weights = (jnp.where(col == route[:, 4:5], route[:, 2:3], 0.0)
               + jnp.where(col == route[:, 5:6], route[:, 3:4], 0.0)).astype(BF16)
    out = x1_ref[...] + jnp.dot(weights, lbuf[slot], preferred_element_type=F32)
    out_ref[...] = _rms(out, fg_ref[...]) if final_norm else out


def _combine(x1, route, ys, fg, plan, tt, final_norm):
    n, d = x1.shape
    n_tiles = n // tt
    kernel_fn = functools.partial(_combine_kernel, tt=tt, n_tiles=n_tiles, final_norm=final_norm)
    grid_spec = pltpu.PrefetchScalarGridSpec(
        num_scalar_prefetch=2,
        grid=(n_tiles,),
        in_specs=[pl.BlockSpec((tt, d), lambda i, *_: (i, 0)),
                  pl.BlockSpec((tt, LANES), lambda i, *_: (i, 0)),
                  pl.BlockSpec(fg.shape, lambda i, *_: (0, 0)),
                  pl.BlockSpec(memory_space=pl.ANY)],
        out_specs=pl.BlockSpec((tt, d), lambda i, *_: (i, 0)),
        scratch_shapes=[pltpu.VMEM((2, _local_rows(tt), d), BF16), pltpu.SemaphoreType.DMA((2,))],
    )
    return pl.pallas_call(
        kernel_fn,
        grid_spec=grid_spec,
        out_shape=jax.ShapeDtypeStruct((n, d), F32),
        compiler_params=pltpu.CompilerParams(dimension_semantics=("arbitrary",), vmem_limit_bytes=VMEM_LIMIT),
        name="moe_combine",
    )(plan["piece_dst"], plan["tile_pieces"], x1, route, fg, ys)


def _routed_moe(x1, h2, route, routet, cnt, wg, wu, wd, layer, fg, final_norm):
    n = x1.shape[0]
    tt = min(TOKEN_TILE, n)
    plan = _route_plan(cnt, n, tt)
    xs = _dispatch(h2, routet, plan, tt)
    ys = _experts(xs, wg, wu, wd, layer, plan)
    return _combine(x1, route, ys, fg, plan, tt, final_norm)


def _moe_kernel(x1_ref, h2_ref, comb_ref, wg_ref, wu_ref, wd_ref, fg_ref, out_ref, *, final_norm):
    e = pl.program_id(1)

    @pl.when(e == 0)
    def _():
        out_ref[...] = x1_ref[...]

    h2 = h2_ref[...]
    gate = jnp.dot(h2, wg_ref[...].astype(BF16), preferred_element_type=F32)
    up = jnp.dot(h2, wu_ref[...].astype(BF16), preferred_element_type=F32)
    comb = comb_ref[...]
    lane = lax.broadcasted_iota(jnp.int32, comb.shape, 1)
    c_col = jnp.sum(jnp.where(lane == EXPERT_LANE0 + e, comb, 0.0), axis=-1, keepdims=True)
    act = (_silu(gate) * up * c_col).astype(BF16)
    out_ref[...] += jnp.dot(act, wd_ref[...].astype(BF16), preferred_element_type=F32)

    if final_norm:
        @pl.when(e == pl.num_programs(1) - 1)
        def _():
            out_ref[...] = _rms(out_ref[...], fg_ref[...])


def _moe(x1, h2, comb, wg, wu, wd, layer, fg, final_norm):
    n, d = x1.shape
    tm = min(1024, n)
    _, n_e, _, d_e = wg.shape
    row = lambda w: pl.BlockSpec((tm, w), lambda i, e: (i, 0))
    return pl.pallas_call(
        functools.partial(_moe_kernel, final_norm=final_norm),
        grid=(n // tm, n_e),
        in_specs=[row(d), row(d), row(LANES),
                  pl.BlockSpec((None, None, d, d_e), lambda i, e: (layer, e, 0, 0)),
                  pl.BlockSpec((None, None, d, d_e), lambda i, e: (layer, e, 0, 0)),
                  pl.BlockSpec((None, None, d_e, d), lambda i, e: (layer, e, 0, 0)),
                  pl.BlockSpec(fg.shape, lambda i, e: (0, 0))],
        out_specs=row(d),
        out_shape=jax.ShapeDtypeStruct((n, d), F32),
        compiler_params=pltpu.CompilerParams(dimension_semantics=("arbitrary", "arbitrary"),
                                             vmem_limit_bytes=VMEM_LIMIT),
        name="moe",
    )(x1, h2, comb, wg, wu, wd, fg)


def _pad_lanes(a, lane0=0):
    a = a.reshape((1, -1)) if a.ndim == 1 else a
    return jnp.pad(a, ((0, 0), (lane0, LANES - lane0 - a.shape[1])))


def kernel(x_prompt, x_sample, state_conv_a, state_conv_b, state_conv_qkv, state_delta, norm1_g, w_in, conv_a_w,
           conv_b_w, ln_b_g, ln_b_b, conv_c_w, a_log, dt_bias, o_norm_g, w_out, norm2_g, w_group, w_router,
           w_gate, w_up, w_down, final_g):
    n_b, t_len, d = x_prompt.shape
    n_s = x_sample.shape[0]
    depth = w_in.shape[0]
    n_main = 3 * D_A + 2 * D_B + 3 * D_C + D_C
    xp = x_prompt.reshape(n_b * t_len, d)
    xs = x_sample.reshape(n_s, d)
    fg = final_g.reshape(1, d)
    outs = {k: [] for k in ("pa", "pb", "pq", "pd", "sa", "sb", "sq", "sd")}
    for l in range(depth):
        g1 = norm1_g[l].reshape(1, d)
        wmain = w_in[l][:, :n_main].astype(BF16)
        wsmall = _pad_lanes(jnp.concatenate([w_in[l][:, n_main + H_C:], w_in[l][:, n_main:n_main + H_C]], axis=1))
        caw, cbw, ccw = conv_a_w[l], conv_b_w[l], conv_c_w[l]
        lng, lnb = ln_b_g[l].reshape(1, D_B), ln_b_b[l].reshape(1, D_B)
        alog, dtb = _pad_lanes(a_log[l]), _pad_lanes(dt_bias[l])
        ong = o_norm_g[l].reshape(1, DV)
        woa = w_out[l][:D_A + D_B].astype(BF16)
        woc = w_out[l][D_A + D_B:].astype(BF16)
        g2 = norm2_g[l].reshape(1, d)
        wrt = _pad_lanes(jnp.concatenate(
            [w_group[l], w_router[l].transpose(1, 0, 2).reshape(d, N_EXPERTS)], axis=1))
        last = l == depth - 1

        yab, q, k, v, gate, gb, st_a, st_b, st_q = _inproj_prompt(
            xp, n_b, t_len, g1, wmain, wsmall, caw, cbw, lng, lnb, ccw, alog, dtb)
        yc, s_fin = _delta_prompt(q, k, v, gb, gate, ong, n_b, t_len)
        x1, h2, _, route, routet, cnt = _outproj_router(xp, yab, yc, woa, woc, g2, wrt)
        xp = _routed_moe(x1, h2, route, routet, cnt, w_gate, w_up, w_down, l, fg, last)
        outs["pa"].append(st_a); outs["pb"].append(st_b); outs["pq"].append(st_q); outs["pd"].append(s_fin)

        hist_a = state_conv_a[l].transpose(1, 0, 2)
        hist_b = state_conv_b[l].transpose(1, 0, 2)
        hist_q = state_conv_qkv[l].transpose(1, 0, 2)
        yab, q, k, v, gate, gb, un_a, un_b, un_q = _inproj_sample(
            xs, g1, wmain, wsmall, caw, cbw, lng, lnb, ccw, alog, dtb, hist_a, hist_b, hist_q)
        yc, s_new = _delta_sample(q, k, v, gb, gate, ong, state_delta, l)
        x1, h2, comb, _, _, _ = _outproj_router(xs, yab, yc, woa, woc, g2, wrt)
        xs = _moe(x1, h2, comb, w_gate, w_up, w_down, l, fg, last)
        outs["sa"].append(jnp.concatenate([state_conv_a[l][:, 1:], un_a[:, None]], axis=1))
        outs["sb"].append(jnp.concatenate([state_conv_b[l][:, 1:], un_b[:, None]], axis=1))
        outs["sq"].append(jnp.concatenate([state_conv_qkv[l][:, 1:], un_q[:, None]], axis=1))
        outs["sd"].append(s_new)

    stack = lambda key: jnp.stack(outs[key])
    return (xp.reshape(n_b, t_len, d), xs.reshape(n_s, 1, d),
            stack("pa"), stack("pb"), stack("pq"), stack("pd"),
            stack("sa"), stack("sb"), stack("sq"), stack("sd"))
```

```python
import functools

import jax
import jax.numpy as jnp
from jax import lax
from jax.experimental import pallas as pl
from jax.experimental.pallas import tpu as pltpu

F32 = jnp.float32
BF16 = jnp.bfloat16

EPS = 1e-6
H_C = 4
DK = 128
DV = 128
D_A = 256
D_B = 256
D_C = 512
W_A = 3
W_B = 31
W_C = 4
CHUNK = 64
N_GROUPS = 4
EXP_PER_GROUP = 8
N_EXPERTS = N_GROUPS * EXP_PER_GROUP
LANES = 128
PAD_A = 8
PAD_B = 32
PAD_C = 8
EXPERT_LANE0 = N_GROUPS
VMEM_LIMIT = 56 * 1024 * 1024
TOKEN_TILE = 512
PIECE = 16
ROW_TILE = 512
SORT_CHUNK = 256


def _silu(x):
    return x * (1.0 / (1.0 + jnp.exp(-x)))


def _sigmoid(x):
    return 1.0 / (1.0 + jnp.exp(-x))


def _softplus(x):
    return jnp.maximum(x, 0.0) + jnp.log1p(jnp.exp(-jnp.abs(x)))


def _rms(x, g):
    return x * lax.rsqrt(jnp.mean(x * x, axis=-1, keepdims=True) + EPS) * g


def _conv_taps(buf, w_ref, width, pad, n_rows, n_cols, row_blk, col_blk):
    row_parts = []
    for r0 in range(0, n_rows, row_blk):
        col_parts = []
        for c0 in range(0, n_cols, col_blk):
            acc = None
            for j in range(width):
                start = pad - (width - 1) + j + r0
                term = buf[start:start + row_blk, c0:c0 + col_blk] * w_ref[j:j + 1, c0:c0 + col_blk]
                acc = term if acc is None else acc + term
            col_parts.append(acc)
        row_parts.append(col_parts[0] if len(col_parts) == 1 else jnp.concatenate(col_parts, axis=1))
    return row_parts[0] if len(row_parts) == 1 else jnp.concatenate(row_parts, axis=0)


def _conv_taps_realigned(buf, shifted, w_ref, width, pad, n_rows, row_blk):
    first = pad - (width - 1)
    n_shift = n_rows + (first + width - 1) // 8 * 8 - 8
    for r in range(1, 8):
        shifted[r - 1] = buf[r:r + n_shift, :]
    row_parts = []
    for r0 in range(0, n_rows, row_blk):
        acc = None
        for j in range(width):
            q, r = divmod(first + j, 8)
            lo = 8 * q + r0
            window = buf[lo:lo + row_blk, :] if r == 0 else shifted[r - 1, lo:lo + row_blk, :]
            term = window * w_ref[j:j + 1, :]
            acc = term if acc is None else acc + term
        row_parts.append(acc)
    return row_parts[0] if len(row_parts) == 1 else jnp.concatenate(row_parts, axis=0)


def _l2norm_heads(x, scale):
    parts = []
    for h in range(H_C):
        xh = x[:, h * DK:(h + 1) * DK]
        parts.append(xh * (lax.rsqrt(jnp.sum(xh * xh, axis=-1, keepdims=True) + 1e-6) * scale))
    return jnp.concatenate(parts, axis=1)


def _dot_3pass(a, a_hi, w):
    a_lo = (a - a_hi.astype(F32)).astype(BF16)
    w_hi = w.astype(BF16)
    w_lo = (w - w_hi.astype(F32)).astype(BF16)
    n = w.shape[1]
    both = jnp.dot(a_hi, jnp.concatenate([w_hi, w_lo], axis=1), preferred_element_type=F32)
    return both[:, :n] + both[:, n:] + jnp.dot(a_lo, w_hi, preferred_element_type=F32)


def _in_projection(x, g1_ref, wmain_ref, wsmall_ref):
    h = _rms(x, g1_ref[...])
    hb = h.astype(BF16)
    z_a = jnp.dot(hb, wmain_ref[:, 0:768], preferred_element_type=F32)
    z_b = jnp.dot(hb, wmain_ref[:, 768:1280], preferred_element_type=F32)
    z_qkv = jnp.dot(hb, wmain_ref[:, 1280:2816], preferred_element_type=F32)
    z_gate = jnp.dot(hb, wmain_ref[:, 2816:3328], preferred_element_type=F32)
    z_s = _dot_3pass(h, hb, wsmall_ref[...])
    return z_a, z_b, z_qkv, z_gate, z_s


def _layer_norm_silu(x, g, b):
    mu = jnp.mean(x, axis=-1, keepdims=True)
    xc = x - mu
    y = xc * lax.rsqrt(jnp.mean(xc * xc, axis=-1, keepdims=True) + EPS)
    return _silu(y * g + b)


def _decay_and_beta(z_s, alog_ref, dtb_ref):
    lane = lax.broadcasted_iota(jnp.int32, z_s.shape, 1)
    g = -jnp.exp(alog_ref[...]) * _softplus(z_s + dtb_ref[...])
    g = jnp.where(lane < H_C, g, 0.0)
    beta = jnp.where((lane >= H_C) & (lane < 2 * H_C), _sigmoid(z_s), 0.0)
    return g, beta


def _inproj_prompt_kernel(x_ref, g1_ref, wmain_ref, wsmall_ref, caw_ref, cbw_ref, lng_ref, lnb_ref,
                          ccw_ref, alog_ref, dtb_ref,
                          yab_ref, q_ref, k_ref, v_ref, gate_ref, gb_ref, sta_ref, stb_ref, stq_ref,
                          abuf, bbuf, cbuf, bshift, *, tt, nt):
    t = pl.program_id(1)

    @pl.when(t == 0)
    def _():
        abuf[0:PAD_A, :] = jnp.zeros((PAD_A, D_A), F32)
        bbuf[0:PAD_B, :] = jnp.zeros((PAD_B, D_B), F32)
        cbuf[0:PAD_C, :] = jnp.zeros((PAD_C, 3 * D_C), F32)

    h = _rms(x_ref[...], g1_ref[...])
    hb = h.astype(BF16)
    project = lambda lo, hi: jnp.dot(hb, wmain_ref[:, lo:hi], preferred_element_type=F32)

    z_a = project(0, 3 * D_A)
    abuf[PAD_A:PAD_A + tt, :] = z_a[:, 256:512] * z_a[:, 512:768]
    conv_a = _conv_taps(abuf, caw_ref, W_A, PAD_A, tt, D_A, 64, 256)
    yab_ref[:, 0:D_A] = z_a[:, 0:256] * conv_a

    z_b = project(3 * D_A, 3 * D_A + 2 * D_B)
    bbuf[PAD_B:PAD_B + tt, :] = z_b[:, 0:256] * _sigmoid(z_b[:, 256:512])
    conv_b = _conv_taps_realigned(bbuf, bshift, cbw_ref, W_B, PAD_B, tt, 64)
    yab_ref[:, D_A:D_A + D_B] = _layer_norm_silu(conv_b, lng_ref[...], lnb_ref[...])

    qkv0 = 3 * D_A + 2 * D_B
    for part, (out_ref, scale) in enumerate(((q_ref, DK ** -0.5), (k_ref, 1.0), (v_ref, None))):
        c0 = part * D_C
        cbuf[PAD_C:PAD_C + tt, c0:c0 + D_C] = project(qkv0 + c0, qkv0 + c0 + D_C)
        conv = _silu(_conv_taps(cbuf.at[:, c0:c0 + D_C], ccw_ref.at[:, c0:c0 + D_C], W_C, PAD_C, tt, D_C, 64, 512))
        out_ref[...] = conv if scale is None else _l2norm_heads(conv, scale)
    gate_ref[...] = project(qkv0 + 3 * D_C, qkv0 + 4 * D_C)

    z_s = _dot_3pass(h, hb, wsmall_ref[...])
    g, beta = _decay_and_beta(z_s, alog_ref, dtb_ref)
    row_in_chunk = lax.broadcasted_iota(jnp.int32, g.shape, 0) & (CHUNK - 1)
    shift = 1
    while shift < CHUNK:
        g = g + jnp.where(row_in_chunk >= shift, pltpu.roll(g, shift, axis=0), 0.0)
        shift *= 2
    gb_ref[...] = g + beta

    @pl.when(t == nt - 1)
    def _():
        sta_ref[0] = abuf[PAD_A + tt - (W_A - 1):PAD_A + tt, :]
        stb_ref[0] = bbuf[PAD_B + tt - (W_B - 1):PAD_B + tt, :]
        stq_ref[0] = cbuf[PAD_C + tt - (W_C - 1):PAD_C + tt, :]

    abuf[0:PAD_A, :] = abuf[tt:tt + PAD_A, :]
    bbuf[0:PAD_B, :] = bbuf[tt:tt + PAD_B, :]
    cbuf[0:PAD_C, :] = cbuf[tt:tt + PAD_C, :]


def _const_spec(shape):
    nd = len(shape)
    return pl.BlockSpec(shape, lambda *_: (0,) * nd)


def _inproj_prompt(x2d, n_b, t_len, g1, wmain, wsmall, caw, cbw, lng, lnb, ccw, alog, dtb):
    tt = min(512, t_len)
    nt = t_len // tt
    n_tok = n_b * t_len
    d = x2d.shape[1]
    row = lambda w: pl.BlockSpec((tt, w), lambda b, t: (b * nt + t, 0))
    st = lambda r, w: pl.BlockSpec((1, r, w), lambda b, t: (b, 0, 0))
    out_shape = (
        jax.ShapeDtypeStruct((n_tok, D_A + D_B), F32),
        jax.ShapeDtypeStruct((n_tok, D_C), F32),
        jax.ShapeDtypeStruct((n_tok, D_C), F32),
        jax.ShapeDtypeStruct((n_tok, D_C), F32),
        jax.ShapeDtypeStruct((n_tok, D_C), F32),
        jax.ShapeDtypeStruct((n_tok, LANES), F32),
        jax.ShapeDtypeStruct((n_b, W_A - 1, D_A), F32),
        jax.ShapeDtypeStruct((n_b, W_B - 1, D_B), F32),
        jax.ShapeDtypeStruct((n_b, W_C - 1, 3 * D_C), F32),
    )
    return pl.pallas_call(
        functools.partial(_inproj_prompt_kernel, tt=tt, nt=nt),
        grid=(n_b, nt),
        in_specs=[row(d), _const_spec(g1.shape), _const_spec(wmain.shape), _const_spec(wsmall.shape),
                  _const_spec(caw.shape), _const_spec(cbw.shape), _const_spec(lng.shape), _const_spec(lnb.shape),
                  _const_spec(ccw.shape), _const_spec(alog.shape), _const_spec(dtb.shape)],
        out_specs=(row(D_A + D_B), row(D_C), row(D_C), row(D_C), row(D_C), row(LANES),
                   st(W_A - 1, D_A), st(W_B - 1, D_B), st(W_C - 1, 3 * D_C)),
        out_shape=out_shape,
        scratch_shapes=[pltpu.VMEM((PAD_A + tt, D_A), F32), pltpu.VMEM((PAD_B + tt, D_B), F32),
                        pltpu.VMEM((PAD_C + tt, 3 * D_C), F32), pltpu.VMEM((7, PAD_B + tt - 8, D_B), F32)],
        compiler_params=pltpu.CompilerParams(dimension_semantics=("arbitrary", "arbitrary"),
                                             vmem_limit_bytes=VMEM_LIMIT),
        name="inproj_prompt",
    )(x2d, g1, wmain, wsmall, caw, cbw, lng, lnb, ccw, alog, dtb)


def _dot_nt(a, b):
    return lax.dot_general(a, b, (((1,), (1,)), ((), ())), preferred_element_type=F32)


def _dot_tn(a, b):
    return lax.dot_general(a, b, (((0,), (0,)), ((), ())), preferred_element_type=F32)


def _bdot(a, b):
    return jnp.dot(a.astype(BF16), b.astype(BF16), preferred_element_type=F32)


def _gated_norm(o, on_g, gate):
    return o * lax.rsqrt(jnp.mean(o * o, axis=-1, keepdims=True) + EPS) * on_g * _silu(gate)


def _delta_prompt_kernel(q_ref, k_ref, v_ref, gb_ref, gate_ref, ong_ref, yc_ref, sfin_ref, s_scr, *, tq, nt):
    t = pl.program_id(1)

    @pl.when(t == 0)
    def _():
        s_scr[...] = jnp.zeros(s_scr.shape, F32)

    c = CHUNK
    n_chunks = tq // c
    n_double = c.bit_length() - 2
    inst = [(ch, h) for ch in range(n_chunks) for h in range(H_C)]
    rows = lambda ch: slice(ch * c, (ch + 1) * c)
    cols = lambda h: slice(h * DK, (h + 1) * DK)
    ri = lax.broadcasted_iota(jnp.int32, (c, c), 0)
    ci = lax.broadcasted_iota(jnp.int32, (c, c), 1)
    gb = gb_ref[...]

    q = [q_ref[rows(ch), cols(h)] for ch, h in inst]
    k = [k_ref[rows(ch), cols(h)] for ch, h in inst]
    v = [v_ref[rows(ch), cols(h)] for ch, h in inst]
    g_col = [gb[rows(ch), h:h + 1] for ch, h in inst]
    b_col = [gb[rows(ch), H_C + h:H_C + h + 1] for ch, h in inst]
    g_row = [jnp.sum(jnp.where(ri == ci, g, 0.0), axis=0, keepdims=True) for g in g_col]
    diff = [gc - gr for gc, gr in zip(g_col, g_row)]
    dec_strict = [jnp.exp(jnp.where(ri > ci, d, -jnp.inf)) for d in diff]
    dec_incl = [jnp.exp(jnp.where(ri >= ci, d, -jnp.inf)) for d in diff]
    e_g = [jnp.exp(g) for g in g_col]
    g_last = [g[c - 1:c, :] for g in g_col]

    qkk = [_dot_nt(jnp.concatenate([qi, ki], axis=0).astype(BF16), ki.astype(BF16)) for qi, ki in zip(q, k)]
    qk = [m[:c] * d for m, d in zip(qkk, dec_incl)]
    p = [-(b * m[c:] * d) for b, m, d in zip(b_col, qkk, dec_strict)]
    sol = [jnp.concatenate([vi * b, ki * (b * eg)], axis=1) for vi, ki, b, eg in zip(v, k, b_col, e_g)]
    for level in range(n_double + 1):
        sol = [s + _bdot(pi, s) for pi, s in zip(p, sol)]
        if level < n_double:
            p = [_bdot(pi, pi) for pi in p]
    kd = [ki * jnp.exp(gl - g) for ki, gl, g in zip(k, g_last, g_col)]
    qk_uw = [_bdot(m, s) for m, s in zip(qk, sol)]
    kd_uw = [_dot_tn(x.astype(BF16), s.astype(BF16)) for x, s in zip(kd, sol)]
    lhs = [jnp.concatenate([qi * eg - a[:, DV:], -b[:, DV:]], axis=0).astype(BF16)
           for qi, eg, a, b in zip(q, e_g, qk_uw, kd_uw)]
    decay = [jnp.exp(gl) for gl in g_last]

    state = [s_scr[h] for h in range(H_C)]
    for ch in range(n_chunks):
        base = ch * H_C
        r = [jnp.dot(lhs[base + h], state[h].astype(BF16), preferred_element_type=F32) for h in range(H_C)]
        for h in range(H_C):
            o = r[h][:c] + qk_uw[base + h][:, :DV]
            yc_ref[rows(ch), cols(h)] = _gated_norm(o, ong_ref[...], gate_ref[rows(ch), cols(h)])
        state = [decay[base + h] * state[h] + r[h][c:] + kd_uw[base + h][:, :DV] for h in range(H_C)]
    for h in range(H_C):
        s_scr[h] = state[h]

    @pl.when(t == nt - 1)
    def _():
        sfin_ref[0] = s_scr[...]


def _delta_prompt(q, k, v, gb, gate, ong, n_b, t_len):
    tq = min(256, t_len)
    nt = t_len // tq
    row = lambda w: pl.BlockSpec((tq, w), lambda b, t: (b * nt + t, 0))
    return pl.pallas_call(
        functools.partial(_delta_prompt_kernel, tq=tq, nt=nt),
        grid=(n_b, nt),
        in_specs=[row(D_C), row(D_C), row(D_C), row(LANES), row(D_C), _const_spec(ong.shape)],
        out_specs=(row(D_C), pl.BlockSpec((1, H_C, DK, DV), lambda b, t: (b, 0, 0, 0))),
        out_shape=(jax.ShapeDtypeStruct((n_b * t_len, D_C), F32),
                   jax.ShapeDtypeStruct((n_b, H_C, DK, DV), F32)),
        scratch_shapes=[pltpu.VMEM((H_C, DK, DV), F32)],
        compiler_params=pltpu.CompilerParams(dimension_semantics=("arbitrary", "arbitrary"),
                                             vmem_limit_bytes=VMEM_LIMIT),
        name="delta_prompt",
    )(q, k, v, gb, gate, ong)


def _inproj_sample_kernel(x_ref, g1_ref, wmain_ref, wsmall_ref, caw_ref, cbw_ref, lng_ref, lnb_ref,
                          ccw_ref, alog_ref, dtb_ref, sta_ref, stb_ref, stq_ref,
                          yab_ref, q_ref, k_ref, v_ref, gate_ref, gb_ref, una_ref, unb_ref, unq_ref):
    z_a, z_b, z_qkv, z_gate, z_s = _in_projection(x_ref[...], g1_ref, wmain_ref, wsmall_ref)

    def conv_step(state_ref, new, w_ref, width):
        acc = new * w_ref[width - 1:width, :]
        for j in range(width - 1):
            acc = acc + state_ref[j] * w_ref[j:j + 1, :]
        return acc

    u_a = z_a[:, 256:512] * z_a[:, 512:768]
    una_ref[...] = u_a
    yab_ref[:, 0:D_A] = z_a[:, 0:256] * conv_step(sta_ref, u_a, caw_ref, W_A)

    u_b = z_b[:, 0:256] * _sigmoid(z_b[:, 256:512])
    unb_ref[...] = u_b
    yab_ref[:, D_A:D_A + D_B] = _layer_norm_silu(conv_step(stb_ref, u_b, cbw_ref, W_B), lng_ref[...], lnb_ref[...])

    unq_ref[...] = z_qkv
    qkv = _silu(conv_step(stq_ref, z_qkv, ccw_ref, W_C))
    q_ref[...] = _l2norm_heads(qkv[:, 0:D_C], DK ** -0.5)
    k_ref[...] = _l2norm_heads(qkv[:, D_C:2 * D_C], 1.0)
    v_ref[...] = qkv[:, 2 * D_C:3 * D_C]
    gate_ref[...] = z_gate
    g, beta = _decay_and_beta(z_s, alog_ref, dtb_ref)
    gb_ref[...] = g + beta


def _inproj_sample(x2d, g1, wmain, wsmall, caw, cbw, lng, lnb, ccw, alog, dtb, st_a, st_b, st_q):
    n = x2d.shape[0]
    args = (x2d, g1, wmain, wsmall, caw, cbw, lng, lnb, ccw, alog, dtb, st_a, st_b, st_q)
    out_widths = (D_A + D_B, D_C, D_C, D_C, D_C, LANES, D_A, D_B, 3 * D_C)
    return pl.pallas_call(
        _inproj_sample_kernel,
        grid=(1,),
        in_specs=[_const_spec(a.shape) for a in args],
        out_specs=tuple(_const_spec((n, w)) for w in out_widths),
        out_shape=tuple(jax.ShapeDtypeStruct((n, w), F32) for w in out_widths),
        compiler_params=pltpu.CompilerParams(dimension_semantics=("arbitrary",), vmem_limit_bytes=VMEM_LIMIT),
        name="inproj_sample",
    )(*args)


def _delta_sample_kernel(q_ref, k_ref, v_ref, gb_ref, gate_ref, ong_ref, s_ref, yc_ref, snew_ref, *, nb):
    q = q_ref[...]
    k = k_ref[...]
    rows = [k[:, h * DK:(h + 1) * DK] for h in range(H_C)] + [q[:, h * DK:(h + 1) * DK] for h in range(H_C)]
    rows.append(jnp.zeros((LANES - 2 * H_C * nb, DK), F32))
    kq_t = jnp.concatenate(rows, axis=0).T
    gb = gb_ref[...]
    inst = [(i, h) for i in range(nb) for h in range(H_C)]
    cols = lambda h: slice(h * DK, (h + 1) * DK)
    k_bc = [jnp.broadcast_to(kq_t[:, h * nb + i:h * nb + i + 1], (DK, DV)) for i, h in inst]
    q_bc = [jnp.broadcast_to(kq_t[:, (H_C + h) * nb + i:(H_C + h) * nb + i + 1], (DK, DV)) for i, h in inst]
    k_s = [jnp.sum(s_ref[i, h] * kb, axis=0, keepdims=True) for (i, h), kb in zip(inst, k_bc)]
    q_s = [jnp.sum(s_ref[i, h] * qb, axis=0, keepdims=True) for (i, h), qb in zip(inst, q_bc)]
    e_g = [jnp.exp(gb[i:i + 1, h:h + 1]) for i, h in inst]
    v_new = [gb[i:i + 1, H_C + h:H_C + h + 1] * (v_ref[i:i + 1, cols(h)] - eg * ks)
             for (i, h), eg, ks in zip(inst, e_g, k_s)]
    qk = [jnp.sum(q[i:i + 1, cols(h)] * k[i:i + 1, cols(h)], axis=-1, keepdims=True) for i, h in inst]
    for n, (i, h) in enumerate(inst):
        snew_ref[i, h] = e_g[n] * s_ref[i, h] + k_bc[n] * v_new[n]
    for n, (i, h) in enumerate(inst):
        o = e_g[n] * q_s[n] + qk[n] * v_new[n]
        yc_ref[i:i + 1, cols(h)] = _gated_norm(o, ong_ref[...], gate_ref[i:i + 1, cols(h)])


def _delta_sample(q, k, v, gb, gate, ong, state, layer):
    n = q.shape[0]
    nb = 8
    row = lambda w: pl.BlockSpec((nb, w), lambda i: (i, 0))
    st_in = pl.BlockSpec((None, nb, H_C, DK, DV), lambda i: (layer, i, 0, 0, 0))
    st = pl.BlockSpec((nb, H_C, DK, DV), lambda i: (i, 0, 0, 0))
    return pl.pallas_call(
        functools.partial(_delta_sample_kernel, nb=nb),
        grid=(n // nb,),
        in_specs=[row(D_C), row(D_C), row(D_C), row(LANES), row(D_C), _const_spec(ong.shape), st_in],
        out_specs=(row(D_C), st),
        out_shape=(jax.ShapeDtypeStruct((n, D_C), F32), jax.ShapeDtypeStruct(state.shape[1:], F32)),
        compiler_params=pltpu.CompilerParams(dimension_semantics=("arbitrary",), vmem_limit_bytes=VMEM_LIMIT),
        name="delta_sample",
    )(q, k, v, gb, gate, ong, state)


def _outproj_router_kernel(x_ref, yab_ref, yc_ref, woa_ref, woc_ref, g2_ref, wrt_ref,
                           x1_ref, h2_ref, comb_ref, route_ref, routet_ref, cnt_ref):
    x1 = (x_ref[...]
          + jnp.dot(yab_ref[...].astype(BF16), woa_ref[...], preferred_element_type=F32)
          + jnp.dot(yc_ref[...].astype(BF16), woc_ref[...], preferred_element_type=F32))
    x1_ref[...] = x1
    h2 = _rms(x1, g2_ref[...])
    h2_ref[...] = h2.astype(BF16)
    logits = _dot_3pass(h2, h2.astype(BF16), wrt_ref[...])
    lane = lax.broadcasted_iota(jnp.int32, logits.shape, 1)
    neg = -jnp.inf
    gl = jnp.where(lane < N_GROUPS, logits, neg)
    g_max = jnp.max(gl, axis=-1, keepdims=True)
    g_idx = jnp.min(jnp.where(gl == g_max, lane, LANES), axis=-1, keepdims=True)
    g_p = 1.0 / jnp.sum(jnp.exp(gl - g_max), axis=-1, keepdims=True)
    lo = EXPERT_LANE0 + g_idx * EXP_PER_GROUP
    in_group = (lane >= lo) & (lane < lo + EXP_PER_GROUP)
    el = jnp.where(in_group, logits, neg)
    e_max = jnp.max(el, axis=-1, keepdims=True)
    pe = jnp.exp(el - e_max)
    e_prob = pe / jnp.sum(pe, axis=-1, keepdims=True)
    p1 = jnp.max(e_prob, axis=-1, keepdims=True)
    i1 = jnp.min(jnp.where(e_prob == p1, lane, LANES), axis=-1, keepdims=True)
    rest = jnp.where(in_group & (lane != i1), e_prob, -1.0)
    p2 = jnp.max(rest, axis=-1, keepdims=True)
    i2 = jnp.min(jnp.where(rest == p2, lane, LANES), axis=-1, keepdims=True)
    denom = p1 + p2
    w1 = g_p * (p1 / denom)
    w2 = g_p * (p2 / denom)
    comb_ref[...] = jnp.where(lane == i1, w1, 0.0) + jnp.where(lane == i2, w2, 0.0)
    e1 = i1 - EXPERT_LANE0
    e2 = i2 - EXPERT_LANE0
    oh1 = lane == e1
    oh2 = lane == e2
    picked = jnp.where(oh1 | oh2, 1.0, 0.0)
    cnt = jnp.broadcast_to(jnp.sum(picked, axis=0, keepdims=True), (8, LANES))
    cnt_ref[0] = cnt
    ki = lax.broadcasted_iota(jnp.int32, (LANES, LANES), 0)
    kj = lax.broadcasted_iota(jnp.int32, (LANES, LANES), 1)
    pieces = jnp.ceil(cnt * (1.0 / PIECE)).astype(BF16)
    run_start = PIECE * jnp.dot(pieces, jnp.where(ki < kj, 1.0, 0.0).astype(BF16),
                                preferred_element_type=F32)[0:1, :]
    tm = picked.shape[0]
    ri = lax.broadcasted_iota(jnp.int32, (tm, tm), 0)
    ci = lax.broadcasted_iota(jnp.int32, (tm, tm), 1)
    base = jnp.dot(jnp.where(ri > ci, 1.0, 0.0).astype(BF16), picked.astype(BF16),
                   preferred_element_type=F32) + run_start
    pos1 = jnp.sum(jnp.where(oh1, base, 0.0), axis=-1, keepdims=True)
    pos2 = jnp.sum(jnp.where(oh2, base, 0.0), axis=-1, keepdims=True)
    route = jnp.where(lane == 0, e1.astype(F32), jnp.where(lane == 1, e2.astype(F32), jnp.where(
        lane == 2, w1, jnp.where(lane == 3, w2, jnp.where(lane == 4, pos1, jnp.where(lane == 5, pos2, 0.0))))))
    route_ref[...] = route
    routet_ref[0] = route.T[0:8, :]


def _outproj_router(x2d, yab, yc, woa, woc, g2, wrt):
    n, d = x2d.shape
    tm = min(TOKEN_TILE, n)
    row = lambda w: pl.BlockSpec((tm, w), lambda i: (i, 0))
    return pl.pallas_call(
        _outproj_router_kernel,
        grid=(n // tm,),
        in_specs=[row(d), row(D_A + D_B), row(D_C), _const_spec(woa.shape), _const_spec(woc.shape),
                  _const_spec(g2.shape), _const_spec(wrt.shape)],
        out_specs=(row(d), row(d), row(LANES), row(LANES), pl.BlockSpec((1, 8, tm), lambda i: (i, 0, 0)),
                   pl.BlockSpec((1, 8, LANES), lambda i: (i, 0, 0))),
        out_shape=(jax.ShapeDtypeStruct((n, d), F32), jax.ShapeDtypeStruct((n, d), BF16),
                   jax.ShapeDtypeStruct((n, LANES), F32), jax.ShapeDtypeStruct((n, LANES), F32),
                   jax.ShapeDtypeStruct((n // tm, 8, tm), F32), jax.ShapeDtypeStruct((n // tm, 8, LANES), F32)),
        compiler_params=pltpu.CompilerParams(dimension_semantics=("arbitrary",), vmem_limit_bytes=VMEM_LIMIT),
        name="outproj_router",
    )(x2d, yab, yc, woa, woc, g2, wrt)


def _route_plan(cnt, n_tok, tt):
    n_tiles = n_tok // tt
    cnt = cnt[:, 0, :N_EXPERTS].astype(jnp.int32)
    pc = (cnt + PIECE - 1) // PIECE * PIECE
    local_end = jnp.cumsum(pc, axis=1)
    local_off = local_end - pc
    seg_len = jnp.sum(pc, axis=0)
    seg_pad = (seg_len + ROW_TILE - 1) // ROW_TILE * ROW_TILE
    seg_end = jnp.cumsum(seg_pad)
    seg_start = seg_end - seg_pad
    run_off = seg_start[None, :] + jnp.cumsum(pc, axis=0) - pc
    piece_row = jnp.arange(_local_rows(tt) // PIECE, dtype=jnp.int32) * PIECE
    in_run = ((local_off[:, None, :] <= piece_row[None, :, None])
              & (piece_row[None, :, None] < local_end[:, None, :]))
    piece_dst = jnp.sum(jnp.where(in_run, (run_off - local_off)[:, None, :], 0), axis=-1) + piece_row[None, :]
    n_row_tiles = _max_sorted_rows(n_tok, tt) // ROW_TILE
    tile_row0 = jnp.arange(n_row_tiles, dtype=jnp.int32) * ROW_TILE
    tile_expert = jnp.minimum(jnp.sum((seg_end[None, :] <= tile_row0[:, None]).astype(jnp.int32), axis=1),
                              N_EXPERTS - 1)
    used = seg_pad > 0
    order = jnp.cumsum(used.astype(jnp.int32)) - used.astype(jnp.int32)
    ids = jnp.arange(N_EXPERTS, dtype=jnp.int32)
    later = jnp.where(used[None, :] & (ids[None, :] > ids[:, None]), ids[None, :], N_EXPERTS)
    next_used = jnp.min(later, axis=1)
    return dict(
        piece_dst=piece_dst.reshape(-1), tile_pieces=jnp.sum(pc, axis=1) // PIECE,
        fill_off=seg_start + seg_len, fill_pieces=(seg_pad - seg_len) // PIECE,
        tile_expert=tile_expert, n_active=(seg_end[-1] // ROW_TILE).reshape(1),
        expert_order=order, next_used=next_used,
    )


def _max_sorted_rows(n_tok, tt):
    n_tiles = n_tok // tt
    rows = 2 * n_tok + n_tiles * N_EXPERTS * (PIECE - 1) + N_EXPERTS * (ROW_TILE - 1)
    return (rows + ROW_TILE - 1) // ROW_TILE * ROW_TILE


def _local_rows(tt):
    return 2 * tt + N_EXPERTS * PIECE


def _piece_copy(src, dst, sem):
    return pltpu.make_async_copy(src, dst, sem)


def _dispatch_kernel(piece_dst, tile_pieces, fill_off, fill_pieces, n_active,
                     h2_ref, routet_ref, xs_hbm, lbuf, zbuf, sem, *, tt, n_tiles, n_row_tiles):
    i = pl.program_id(0)
    slot = i % 2
    lrows = _local_rows(tt)
    max_pieces = lrows // PIECE

    def wait_tile(tile, s):
        def body(_, c):
            _piece_copy(lbuf.at[s, pl.ds(0, PIECE)], xs_hbm.at[pl.ds(0, PIECE)], sem.at[s]).wait()
            return c
        lax.fori_loop(0, tile_pieces[tile], body, 0)

    @pl.when(i >= 2)
    def _():
        wait_tile(i - 2, slot)

    pos = routet_ref[0]
    h2 = h2_ref[...]
    for r0 in range(0, lrows, SORT_CHUNK):
        row = (lax.broadcasted_iota(jnp.int32, (SORT_CHUNK, tt), 0) + r0).astype(F32)
        perm = jnp.where((row == pos[4:5, :]) | (row == pos[5:6, :]), 1.0, 0.0).astype(BF16)
        lbuf[slot, r0:r0 + SORT_CHUNK] = jnp.dot(perm, h2, preferred_element_type=F32).astype(BF16)

    def send(p, c):
        dst = piece_dst[i * max_pieces + p]
        _piece_copy(lbuf.at[slot, pl.ds(pl.multiple_of(p * PIECE, PIECE), PIECE)],
                    xs_hbm.at[pl.ds(pl.multiple_of(dst, PIECE), PIECE)], sem.at[slot]).start()
        return c
    lax.fori_loop(0, tile_pieces[i], send, 0)

    @pl.when(i == n_tiles - 1)
    def _():
        zbuf[...] = jnp.zeros(zbuf.shape, BF16)
        zpiece = zbuf.at[pl.ds(0, PIECE)]
        for e in range(N_EXPERTS):
            def body(p, c, e=e):
                _piece_copy(zpiece, xs_hbm.at[pl.ds(pl.multiple_of(fill_off[e] + p * PIECE, PIECE), PIECE)],
                            sem.at[2]).start()
                return c
            lax.fori_loop(0, fill_pieces[e], body, 0)

        def tail_body(j, c):
            _piece_copy(zbuf, xs_hbm.at[pl.ds(pl.multiple_of(j * ROW_TILE, ROW_TILE), ROW_TILE)], sem.at[3]).start()
            return c
        lax.fori_loop(n_active[0], n_row_tiles, tail_body, 0)
        for e in range(N_EXPERTS):
            def body(p, c):
                _piece_copy(zpiece, xs_hbm.at[pl.ds(0, PIECE)], sem.at[2]).wait()
                return c
            lax.fori_loop(0, fill_pieces[e], body, 0)

        def tail_wait(j, c):
            _piece_copy(zbuf, xs_hbm.at[pl.ds(0, ROW_TILE)], sem.at[3]).wait()
            return c
        lax.fori_loop(n_active[0], n_row_tiles, tail_wait, 0)
        if n_tiles > 1:
            wait_tile(i - 1, 1 - slot)
        wait_tile(i, slot)


def _dispatch(h2, routet, plan, tt):
    n, d = h2.shape
    n_tiles = n // tt
    n_rows = _max_sorted_rows(n, tt)
    kernel_fn = functools.partial(_dispatch_kernel, tt=tt, n_tiles=n_tiles, n_row_tiles=n_rows // ROW_TILE)
    grid_spec = pltpu.PrefetchScalarGridSpec(
        num_scalar_prefetch=5,
        grid=(n_tiles,),
        in_specs=[pl.BlockSpec((tt, d), lambda i, *_: (i, 0)),
                  pl.BlockSpec((1, 8, tt), lambda i, *_: (i, 0, 0))],
        out_specs=pl.BlockSpec(memory_space=pl.ANY),
        scratch_shapes=[pltpu.VMEM((2, _local_rows(tt), d), BF16), pltpu.VMEM((ROW_TILE, d), BF16),
                        pltpu.SemaphoreType.DMA((4,))],
    )
    return pl.pallas_call(
        kernel_fn,
        grid_spec=grid_spec,
        out_shape=jax.ShapeDtypeStruct((n_rows, d), BF16),
        compiler_params=pltpu.CompilerParams(dimension_semantics=("arbitrary",), vmem_limit_bytes=VMEM_LIMIT),
        name="moe_dispatch",
    )(plan["piece_dst"], plan["tile_pieces"], plan["fill_off"], plan["fill_pieces"], plan["n_active"],
      h2, routet)


def _experts_kernel(tile_expert, n_active, expert_order, next_used,
                    xs_ref, wg_hbm, wu_hbm, wd_hbm, ys_ref, wbuf_g, wbuf_u, wbuf_d, wg_b, wu_b, wd_b, sem,
                    *, layer):
    j = pl.program_id(0)
    expert = tile_expert[j]
    previous = tile_expert[jnp.maximum(j - 1, 0)]
    active = j < n_active[0]
    half = expert_order[expert] % 2

    def weight_copies(e, s):
        return (_piece_copy(wg_hbm.at[layer, e], wbuf_g.at[s], sem.at[s, 0]),
                _piece_copy(wu_hbm.at[layer, e], wbuf_u.at[s], sem.at[s, 1]),
                _piece_copy(wd_hbm.at[layer, e], wbuf_d.at[s], sem.at[s, 2]))

    @pl.when(j == 0)
    def _():
        for c in weight_copies(expert, half):
            c.start()

    @pl.when(active & ((j == 0) | (expert != previous)))
    def _():
        for c in weight_copies(expert, half):
            c.wait()
        wg_b[...] = wbuf_g[half].astype(BF16)
        wu_b[...] = wbuf_u[half].astype(BF16)
        wd_b[...] = wbuf_d[half].astype(BF16)
        upcoming = next_used[expert]

        @pl.when(upcoming < N_EXPERTS)
        def _():
            for c in weight_copies(upcoming, 1 - half):
                c.start()

    @pl.when(active)
    def _():
        x = xs_ref[...]
        gate = jnp.dot(x, wg_b[...], preferred_element_type=F32)
        up = jnp.dot(x, wu_b[...], preferred_element_type=F32)
        act = (_silu(gate) * up).astype(BF16)
        ys_ref[...] = jnp.dot(act, wd_b[...], preferred_element_type=F32).astype(BF16)

    @pl.when(jnp.logical_not(active))
    def _():
        ys_ref[...] = jnp.zeros(ys_ref.shape, BF16)


def _experts(xs, wg, wu, wd, layer, plan):
    rows, d = xs.shape
    d_e = wg.shape[-1]
    live = lambda j, te, na, *_: (jnp.minimum(j, na[0] - 1), 0)
    grid_spec = pltpu.PrefetchScalarGridSpec(
        num_scalar_prefetch=4,
        grid=(rows // ROW_TILE,),
        in_specs=[pl.BlockSpec((ROW_TILE, d), live),
                  pl.BlockSpec(memory_space=pl.ANY), pl.BlockSpec(memory_space=pl.ANY),
                  pl.BlockSpec(memory_space=pl.ANY)],
        out_specs=pl.BlockSpec((ROW_TILE, d), lambda j, *_: (j, 0)),
        scratch_shapes=[pltpu.VMEM((2, d, d_e), F32), pltpu.VMEM((2, d, d_e), F32), pltpu.VMEM((2, d_e, d), F32),
                        pltpu.VMEM((d, d_e), BF16), pltpu.VMEM((d, d_e), BF16), pltpu.VMEM((d_e, d), BF16),
                        pltpu.SemaphoreType.DMA((2, 3))],
    )
    return pl.pallas_call(
        functools.partial(_experts_kernel, layer=layer),
        grid_spec=grid_spec,
        out_shape=jax.ShapeDtypeStruct((rows, d), BF16),
        compiler_params=pltpu.CompilerParams(dimension_semantics=("arbitrary",), vmem_limit_bytes=VMEM_LIMIT),
        name="moe_experts",
    )(plan["tile_expert"], plan["n_active"], plan["expert_order"], plan["next_used"], xs, wg, wu, wd)


def _combine_kernel(piece_src, tile_pieces, x1_ref, route_ref, fg_ref, ys_hbm, out_ref, lbuf, sem,
                    *, tt, n_tiles, final_norm):
    i = pl.program_id(0)
    slot = i % 2
    lrows = _local_rows(tt)
    max_pieces = lrows // PIECE

    def fetch_tile(tile, s):
        def body(p, c):
            src = piece_src[tile * max_pieces + p]
            _piece_copy(ys_hbm.at[pl.ds(pl.multiple_of(src, PIECE), PIECE)],
                        lbuf.at[s, pl.ds(pl.multiple_of(p * PIECE, PIECE), PIECE)], sem.at[s]).start()
            return c
        lax.fori_loop(0, tile_pieces[tile], body, 0)

    @pl.when(i == 0)
    def _():
        lbuf[...] = jnp.zeros(lbuf.shape, BF16)
        fetch_tile(0, 0)

    @pl.when(i + 1 < n_tiles)
    def _():
        fetch_tile(i + 1, 1 - slot)

    def wait_body(_, c):
        _piece_copy(ys_hbm.at[pl.ds(0, PIECE)], lbuf.at[slot, pl.ds(0, PIECE)], sem.at[slot]).wait()
        return c
    lax.fori_loop(0, tile_pieces[i], wait_body, 0)

    route = route_ref[...]
    out = x1_ref[...]
    for r0 in range(0, lrows, SORT_CHUNK):
        col = (lax.broadcasted_iota(jnp.int32, (tt, SORT_CHUNK), 1) + r0).astype(F32)
        weights = (jnp.where(col == route[:, 4:5], route[:, 2:3], 0.0)
                   + jnp.where(col == route[:, 5:6], route[:, 3:4], 0.0)).astype(BF16)
        out = out + jnp.dot(weights, lbuf[slot, r0:r0 + SORT_CHUNK], preferred_element_type=F32)
    out_ref[...] = _rms(out, fg_ref[...]) if final_norm else out


def _combine(x1, route, ys, fg, plan, tt, final_norm):
    n, d = x1.shape
    n_tiles = n // tt
    kernel_fn = functools.partial(_combine_kernel, tt=tt, n_tiles=n_tiles, final_norm=final_norm)
    grid_spec = pltpu.PrefetchScalarGridSpec(
        num_scalar_prefetch=2,
        grid=(n_tiles,),
        in_specs=[pl.BlockSpec((tt, d), lambda i, *_: (i, 0)),
                  pl.BlockSpec((tt, LANES), lambda i, *_: (i, 0)),
                  pl.BlockSpec(fg.shape, lambda i, *_: (0, 0)),
                  pl.BlockSpec(memory_space=pl.ANY)],
        out_specs=pl.BlockSpec((tt, d), lambda i, *_: (i, 0)),
        scratch_shapes=[pltpu.VMEM((2, _local_rows(tt), d), BF16), pltpu.SemaphoreType.DMA((2,))],
    )
    return pl.pallas_call(
        kernel_fn,
        grid_spec=grid_spec,
        out_shape=jax.ShapeDtypeStruct((n, d), F32),
        compiler_params=pltpu.CompilerParams(dimension_semantics=("arbitrary",), vmem_limit_bytes=VMEM_LIMIT),
        name="moe_combine",
    )(plan["piece_dst"], plan["tile_pieces"], x1, route, fg, ys)


def _routed_moe(x1, h2, route, routet, cnt, wg, wu, wd, layer, fg, final_norm):
    n = x1.shape[0]
    tt = min(TOKEN_TILE, n)
    plan = _route_plan(cnt, n, tt)
    xs = _dispatch(h2, routet, plan, tt)
    ys = _experts(xs, wg, wu, wd, layer, plan)
    return _combine(x1, route, ys, fg, plan, tt, final_norm)


def _moe_kernel(x1_ref, h2_ref, comb_ref, wg_ref, wu_ref, wd_ref, fg_ref, out_ref, *, final_norm):
    e = pl.program_id(1)

    @pl.when(e == 0)
    def _():
        out_ref[...] = x1_ref[...]

    h2 = h2_ref[...]
    gate = jnp.dot(h2, wg_ref[...].astype(BF16), preferred_element_type=F32)
    up = jnp.dot(h2, wu_ref[...].astype(BF16), preferred_element_type=F32)
    comb = comb_ref[...]
    lane = lax.broadcasted_iota(jnp.int32, comb.shape, 1)
    c_col = jnp.sum(jnp.where(lane == EXPERT_LANE0 + e, comb, 0.0), axis=-1, keepdims=True)
    act = (_silu(gate) * up * c_col).astype(BF16)
    out_ref[...] += jnp.dot(act, wd_ref[...].astype(BF16), preferred_element_type=F32)

    if final_norm:
        @pl.when(e == pl.num_programs(1) - 1)
        def _():
            out_ref[...] = _rms(out_ref[...], fg_ref[...])


def _moe(x1, h2, comb, wg, wu, wd, layer, fg, final_norm):
    n, d = x1.shape
    tm = min(1024, n)
    _, n_e, _, d_e = wg.shape
    row = lambda w: pl.BlockSpec((tm, w), lambda i, e: (i, 0))
    return pl.pallas_call(
        functools.partial(_moe_kernel, final_norm=final_norm),
        grid=(n // tm, n_e),
        in_specs=[row(d), row(d), row(LANES),
                  pl.BlockSpec((None, None, d, d_e), lambda i, e: (layer, e, 0, 0)),
                  pl.BlockSpec((None, None, d, d_e), lambda i, e: (layer, e, 0, 0)),
                  pl.BlockSpec((None, None, d_e, d), lambda i, e: (layer, e, 0, 0)),
                  pl.BlockSpec(fg.shape, lambda i, e: (0, 0))],
        out_specs=row(d),
        out_shape=jax.ShapeDtypeStruct((n, d), F32),
        compiler_params=pltpu.CompilerParams(dimension_semantics=("arbitrary", "arbitrary"),
                                             vmem_limit_bytes=VMEM_LIMIT),
        name="moe",
    )(x1, h2, comb, wg, wu, wd, fg)


def _pad_lanes(a, lane0=0):
    a = a.reshape((1, -1)) if a.ndim == 1 else a
    return jnp.pad(a, ((0, 0), (lane0, LANES - lane0 - a.shape[1])))


def kernel(x_prompt, x_sample, state_conv_a, state_conv_b, state_conv_qkv, state_delta, norm1_g, w_in, conv_a_w,
           conv_b_w, ln_b_g, ln_b_b, conv_c_w, a_log, dt_bias, o_norm_g, w_out, norm2_g, w_group, w_router,
           w_gate, w_up, w_down, final_g):
    n_b, t_len, d = x_prompt.shape
    n_s = x_sample.shape[0]
    depth = w_in.shape[0]
    n_main = 3 * D_A + 2 * D_B + 3 * D_C + D_C
    xp = x_prompt.reshape(n_b * t_len, d)
    xs = x_sample.reshape(n_s, d)
    fg = final_g.reshape(1, d)
    outs = {k: [] for k in ("pa", "pb", "pq", "pd", "sa", "sb", "sq", "sd")}
    for l in range(depth):
        g1 = norm1_g[l].reshape(1, d)
        wmain = w_in[l][:, :n_main].astype(BF16)
        wsmall = _pad_lanes(jnp.concatenate([w_in[l][:, n_main + H_C:], w_in[l][:, n_main:n_main + H_C]], axis=1))
        caw, cbw, ccw = conv_a_w[l], conv_b_w[l], conv_c_w[l]
        lng, lnb = ln_b_g[l].reshape(1, D_B), ln_b_b[l].reshape(1, D_B)
        alog, dtb = _pad_lanes(a_log[l]), _pad_lanes(dt_bias[l])
        ong = o_norm_g[l].reshape(1, DV)
        woa = w_out[l][:D_A + D_B].astype(BF16)
        woc = w_out[l][D_A + D_B:].astype(BF16)
        g2 = norm2_g[l].reshape(1, d)
        wrt = _pad_lanes(jnp.concatenate(
            [w_group[l], w_router[l].transpose(1, 0, 2).reshape(d, N_EXPERTS)], axis=1))
        last = l == depth - 1

        yab, q, k, v, gate, gb, st_a, st_b, st_q = _inproj_prompt(
            xp, n_b, t_len, g1, wmain, wsmall, caw, cbw, lng, lnb, ccw, alog, dtb)
        yc, s_fin = _delta_prompt(q, k, v, gb, gate, ong, n_b, t_len)
        x1, h2, _, route, routet, cnt = _outproj_router(xp, yab, yc, woa, woc, g2, wrt)
        xp = _routed_moe(x1, h2, route, routet, cnt, w_gate, w_up, w_down, l, fg, last)
        outs["pa"].append(st_a); outs["pb"].append(st_b); outs["pq"].append(st_q); outs["pd"].append(s_fin)

        hist_a = state_conv_a[l].transpose(1, 0, 2)
        hist_b = state_conv_b[l].transpose(1, 0, 2)
        hist_q = state_conv_qkv[l].transpose(1, 0, 2)
        yab, q, k, v, gate, gb, un_a, un_b, un_q = _inproj_sample(
            xs, g1, wmain, wsmall, caw, cbw, lng, lnb, ccw, alog, dtb, hist_a, hist_b, hist_q)
        yc, s_new = _delta_sample(q, k, v, gb, gate, ong, state_delta, l)
        x1, h2, comb, _, _, _ = _outproj_router(xs, yab, yc, woa, woc, g2, wrt)
        xs = _moe(x1, h2, comb, w_gate, w_up, w_down, l, fg, last)
        outs["sa"].append(jnp.concatenate([state_conv_a[l][:, 1:], un_a[:, None]], axis=1))
        outs["sb"].append(jnp.concatenate([state_conv_b[l][:, 1:], un_b[:, None]], axis=1))
        outs["sq"].append(jnp.concatenate([state_conv_qkv[l][:, 1:], un_q[:, None]], axis=1))
        outs["sd"].append(s_new)

    stack = lambda key: jnp.stack(outs[key])
    return (xp.reshape(n_b, t_len, d), xs.reshape(n_s, 1, d),
            stack("pa"), stack("pb"), stack("pq"), stack("pd"),
            stack("sa"), stack("sb"), stack("sq"), stack("sd"))
```

```python
import functools

import jax
import jax.numpy as jnp
from jax import lax
from jax.experimental import pallas as pl
from jax.experimental.pallas import tpu as pltpu

F32 = jnp.float32
BF16 = jnp.bfloat16

EPS = 1e-6
H_C = 4
DK = 128
DV = 128
D_A = 256
D_B = 256
D_C = 512
W_A = 3
W_B = 31
W_C = 4
CHUNK = 64
N_GROUPS = 4
EXP_PER_GROUP = 8
N_EXPERTS = N_GROUPS * EXP_PER_GROUP
LANES = 128
PAD_A = 8
PAD_B = 32
PAD_C = 8
EXPERT_LANE0 = N_GROUPS
VMEM_LIMIT = 56 * 1024 * 1024
TOKEN_TILE = 512
PIECE = 16
ROW_TILE = 512
SORT_CHUNK = 256


def _silu(x):
    return x * (1.0 / (1.0 + jnp.exp(-x)))


def _sigmoid(x):
    return 1.0 / (1.0 + jnp.exp(-x))


def _softplus(x):
    return jnp.maximum(x, 0.0) + jnp.log1p(jnp.exp(-jnp.abs(x)))


def _rms(x, g):
    return x * lax.rsqrt(jnp.mean(x * x, axis=-1, keepdims=True) + EPS) * g


def _conv_taps(buf, w_ref, width, pad, n_rows, n_cols, row_blk, col_blk):
    row_parts = []
    for r0 in range(0, n_rows, row_blk):
        col_parts = []
        for c0 in range(0, n_cols, col_blk):
            acc = None
            for j in range(width):
                start = pad - (width - 1) + j + r0
                term = buf[start:start + row_blk, c0:c0 + col_blk] * w_ref[j:j + 1, c0:c0 + col_blk]
                acc = term if acc is None else acc + term
            col_parts.append(acc)
        row_parts.append(col_parts[0] if len(col_parts) == 1 else jnp.concatenate(col_parts, axis=1))
    return row_parts[0] if len(row_parts) == 1 else jnp.concatenate(row_parts, axis=0)


def _conv_taps_realigned(buf, shifted, w_ref, width, pad, n_rows, row_blk):
    first = pad - (width - 1)
    n_shift = n_rows + (first + width - 1) // 8 * 8 - 8
    for r in range(1, 8):
        shifted[r - 1] = buf[r:r + n_shift, :]
    row_parts = []
    for r0 in range(0, n_rows, row_blk):
        acc = None
        for j in range(width):
            q, r = divmod(first + j, 8)
            lo = 8 * q + r0
            window = buf[lo:lo + row_blk, :] if r == 0 else shifted[r - 1, lo:lo + row_blk, :]
            term = window * w_ref[j:j + 1, :]
            acc = term if acc is None else acc + term
        row_parts.append(acc)
    return row_parts[0] if len(row_parts) == 1 else jnp.concatenate(row_parts, axis=0)


def _l2norm_heads(x, scale):
    parts = []
    for h in range(H_C):
        xh = x[:, h * DK:(h + 1) * DK]
        parts.append(xh * (lax.rsqrt(jnp.sum(xh * xh, axis=-1, keepdims=True) + 1e-6) * scale))
    return jnp.concatenate(parts, axis=1)


def _dot_3pass(a, a_hi, w):
    a_lo = (a - a_hi.astype(F32)).astype(BF16)
    w_hi = w.astype(BF16)
    w_lo = (w - w_hi.astype(F32)).astype(BF16)
    n = w.shape[1]
    both = jnp.dot(a_hi, jnp.concatenate([w_hi, w_lo], axis=1), preferred_element_type=F32)
    return both[:, :n] + both[:, n:] + jnp.dot(a_lo, w_hi, preferred_element_type=F32)


def _in_projection(x, g1_ref, wmain_ref, wsmall_ref):
    h = _rms(x, g1_ref[...])
    hb = h.astype(BF16)
    z_a = jnp.dot(hb, wmain_ref[:, 0:768], preferred_element_type=F32)
    z_b = jnp.dot(hb, wmain_ref[:, 768:1280], preferred_element_type=F32)
    z_qkv = jnp.dot(hb, wmain_ref[:, 1280:2816], preferred_element_type=F32)
    z_gate = jnp.dot(hb, wmain_ref[:, 2816:3328], preferred_element_type=F32)
    z_s = _dot_3pass(h, hb, wsmall_ref[...])
    return z_a, z_b, z_qkv, z_gate, z_s


def _layer_norm_silu(x, g, b):
    mu = jnp.mean(x, axis=-1, keepdims=True)
    xc = x - mu
    y = xc * lax.rsqrt(jnp.mean(xc * xc, axis=-1, keepdims=True) + EPS)
    return _silu(y * g + b)


def _decay_and_beta(z_s, alog_ref, dtb_ref):
    lane = lax.broadcasted_iota(jnp.int32, z_s.shape, 1)
    g = -jnp.exp(alog_ref[...]) * _softplus(z_s + dtb_ref[...])
    g = jnp.where(lane < H_C, g, 0.0)
    beta = jnp.where((lane >= H_C) & (lane < 2 * H_C), _sigmoid(z_s), 0.0)
    return g, beta


def _inproj_prompt_kernel(x_ref, g1_ref, wmain_ref, wsmall_ref, caw_ref, cbw_ref, lng_ref, lnb_ref,
                          ccw_ref, alog_ref, dtb_ref,
                          yab_ref, q_ref, k_ref, v_ref, gate_ref, gb_ref, sta_ref, stb_ref, stq_ref,
                          abuf, bbuf, cbuf, bshift, *, tt, nt):
    t = pl.program_id(1)

    @pl.when(t == 0)
    def _():
        abuf[0:PAD_A, :] = jnp.zeros((PAD_A, D_A), F32)
        bbuf[0:PAD_B, :] = jnp.zeros((PAD_B, D_B), F32)
        cbuf[0:PAD_C, :] = jnp.zeros((PAD_C, 3 * D_C), F32)

    h = _rms(x_ref[...], g1_ref[...])
    hb = h.astype(BF16)
    project = lambda lo, hi: jnp.dot(hb, wmain_ref[:, lo:hi], preferred_element_type=F32)

    z_a = project(0, 3 * D_A)
    abuf[PAD_A:PAD_A + tt, :] = z_a[:, 256:512] * z_a[:, 512:768]
    conv_a = _conv_taps(abuf, caw_ref, W_A, PAD_A, tt, D_A, 64, 256)
    yab_ref[:, 0:D_A] = z_a[:, 0:256] * conv_a

    z_b = project(3 * D_A, 3 * D_A + 2 * D_B)
    bbuf[PAD_B:PAD_B + tt, :] = z_b[:, 0:256] * _sigmoid(z_b[:, 256:512])
    conv_b = _conv_taps_realigned(bbuf, bshift, cbw_ref, W_B, PAD_B, tt, 64)
    yab_ref[:, D_A:D_A + D_B] = _layer_norm_silu(conv_b, lng_ref[...], lnb_ref[...])

    qkv0 = 3 * D_A + 2 * D_B
    for part, (out_ref, scale) in enumerate(((q_ref, DK ** -0.5), (k_ref, 1.0), (v_ref, None))):
        c0 = part * D_C
        cbuf[PAD_C:PAD_C + tt, c0:c0 + D_C] = project(qkv0 + c0, qkv0 + c0 + D_C)
        conv = _silu(_conv_taps(cbuf.at[:, c0:c0 + D_C], ccw_ref.at[:, c0:c0 + D_C], W_C, PAD_C, tt, D_C, 64, 512))
        out_ref[...] = conv if scale is None else _l2norm_heads(conv, scale)
    gate_ref[...] = project(qkv0 + 3 * D_C, qkv0 + 4 * D_C)

    z_s = _dot_3pass(h, hb, wsmall_ref[...])
    g, beta = _decay_and_beta(z_s, alog_ref, dtb_ref)
    row_in_chunk = lax.broadcasted_iota(jnp.int32, g.shape, 0) & (CHUNK - 1)
    shift = 1
    while shift < CHUNK:
        g = g + jnp.where(row_in_chunk >= shift, pltpu.roll(g, shift, axis=0), 0.0)
        shift *= 2
    gb_ref[...] = g + beta

    @pl.when(t == nt - 1)
    def _():
        sta_ref[0] = abuf[PAD_A + tt - (W_A - 1):PAD_A + tt, :]
        stb_ref[0] = bbuf[PAD_B + tt - (W_B - 1):PAD_B + tt, :]
        stq_ref[0] = cbuf[PAD_C + tt - (W_C - 1):PAD_C + tt, :]

    abuf[0:PAD_A, :] = abuf[tt:tt + PAD_A, :]
    bbuf[0:PAD_B, :] = bbuf[tt:tt + PAD_B, :]
    cbuf[0:PAD_C, :] = cbuf[tt:tt + PAD_C, :]


def _const_spec(shape):
    nd = len(shape)
    return pl.BlockSpec(shape, lambda *_: (0,) * nd)


def _inproj_prompt(x2d, n_b, t_len, g1, wmain, wsmall, caw, cbw, lng, lnb, ccw, alog, dtb):
    tt = min(512, t_len)
    nt = t_len // tt
    n_tok = n_b * t_len
    d = x2d.shape[1]
    row = lambda w: pl.BlockSpec((tt, w), lambda b, t: (b * nt + t, 0))
    st = lambda r, w: pl.BlockSpec((1, r, w), lambda b, t: (b, 0, 0))
    out_shape = (
        jax.ShapeDtypeStruct((n_tok, D_A + D_B), F32),
        jax.ShapeDtypeStruct((n_tok, D_C), F32),
        jax.ShapeDtypeStruct((n_tok, D_C), F32),
        jax.ShapeDtypeStruct((n_tok, D_C), F32),
        jax.ShapeDtypeStruct((n_tok, D_C), F32),
        jax.ShapeDtypeStruct((n_tok, LANES), F32),
        jax.ShapeDtypeStruct((n_b, W_A - 1, D_A), F32),
        jax.ShapeDtypeStruct((n_b, W_B - 1, D_B), F32),
        jax.ShapeDtypeStruct((n_b, W_C - 1, 3 * D_C), F32),
    )
    return pl.pallas_call(
        functools.partial(_inproj_prompt_kernel, tt=tt, nt=nt),
        grid=(n_b, nt),
        in_specs=[row(d), _const_spec(g1.shape), _const_spec(wmain.shape), _const_spec(wsmall.shape),
                  _const_spec(caw.shape), _const_spec(cbw.shape), _const_spec(lng.shape), _const_spec(lnb.shape),
                  _const_spec(ccw.shape), _const_spec(alog.shape), _const_spec(dtb.shape)],
        out_specs=(row(D_A + D_B), row(D_C), row(D_C), row(D_C), row(D_C), row(LANES),
                   st(W_A - 1, D_A), st(W_B - 1, D_B), st(W_C - 1, 3 * D_C)),
        out_shape=out_shape,
        scratch_shapes=[pltpu.VMEM((PAD_A + tt, D_A), F32), pltpu.VMEM((PAD_B + tt, D_B), F32),
                        pltpu.VMEM((PAD_C + tt, 3 * D_C), F32), pltpu.VMEM((7, PAD_B + tt - 8, D_B), F32)],
        compiler_params=pltpu.CompilerParams(dimension_semantics=("arbitrary", "arbitrary"),
                                             vmem_limit_bytes=VMEM_LIMIT),
        name="inproj_prompt",
    )(x2d, g1, wmain, wsmall, caw, cbw, lng, lnb, ccw, alog, dtb)


def _dot_nt(a, b):
    return lax.dot_general(a, b, (((1,), (1,)), ((), ())), preferred_element_type=F32)


def _dot_tn(a, b):
    return lax.dot_general(a, b, (((0,), (0,)), ((), ())), preferred_element_type=F32)


def _bdot(a, b):
    return jnp.dot(a.astype(BF16), b.astype(BF16), preferred_element_type=F32)


def _gated_norm(o, on_g, gate):
    return o * lax.rsqrt(jnp.mean(o * o, axis=-1, keepdims=True) + EPS) * on_g * _silu(gate)


def _delta_prompt_kernel(q_ref, k_ref, v_ref, gb_ref, gate_ref, ong_ref, yc_ref, sfin_ref, s_scr, *, tq, nt):
    t = pl.program_id(1)

    @pl.when(t == 0)
    def _():
        s_scr[...] = jnp.zeros(s_scr.shape, F32)

    c = CHUNK
    n_chunks = tq // c
    n_double = c.bit_length() - 2
    inst = [(ch, h) for ch in range(n_chunks) for h in range(H_C)]
    rows = lambda ch: slice(ch * c, (ch + 1) * c)
    cols = lambda h: slice(h * DK, (h + 1) * DK)
    ri = lax.broadcasted_iota(jnp.int32, (c, c), 0)
    ci = lax.broadcasted_iota(jnp.int32, (c, c), 1)
    gb = gb_ref[...]

    q = [q_ref[rows(ch), cols(h)] for ch, h in inst]
    k = [k_ref[rows(ch), cols(h)] for ch, h in inst]
    v = [v_ref[rows(ch), cols(h)] for ch, h in inst]
    g_col = [gb[rows(ch), h:h + 1] for ch, h in inst]
    b_col = [gb[rows(ch), H_C + h:H_C + h + 1] for ch, h in inst]
    g_row = [jnp.sum(jnp.where(ri == ci, g, 0.0), axis=0, keepdims=True) for g in g_col]
    diff = [gc - gr for gc, gr in zip(g_col, g_row)]
    dec_strict = [jnp.exp(jnp.where(ri > ci, d, -jnp.inf)) for d in diff]
    dec_incl = [jnp.exp(jnp.where(ri >= ci, d, -jnp.inf)) for d in diff]
    e_g = [jnp.exp(g) for g in g_col]
    g_last = [g[c - 1:c, :] for g in g_col]

    qkk = [_dot_nt(jnp.concatenate([qi, ki], axis=0).astype(BF16), ki.astype(BF16)) for qi, ki in zip(q, k)]
    qk = [m[:c] * d for m, d in zip(qkk, dec_incl)]
    p = [-(b * m[c:] * d) for b, m, d in zip(b_col, qkk, dec_strict)]
    sol = [jnp.concatenate([vi * b, ki * (b * eg)], axis=1) for vi, ki, b, eg in zip(v, k, b_col, e_g)]
    for level in range(n_double + 1):
        sol = [s + _bdot(pi, s) for pi, s in zip(p, sol)]
        if level < n_double:
            p = [_bdot(pi, pi) for pi in p]
    kd = [ki * jnp.exp(gl - g) for ki, gl, g in zip(k, g_last, g_col)]
    qk_uw = [_bdot(m, s) for m, s in zip(qk, sol)]
    kd_uw = [_dot_tn(x.astype(BF16), s.astype(BF16)) for x, s in zip(kd, sol)]
    lhs = [jnp.concatenate([qi * eg - a[:, DV:], -b[:, DV:]], axis=0).astype(BF16)
           for qi, eg, a, b in zip(q, e_g, qk_uw, kd_uw)]
    decay = [jnp.exp(gl) for gl in g_last]

    state = [s_scr[h] for h in range(H_C)]
    for ch in range(n_chunks):
        base = ch * H_C
        r = [jnp.dot(lhs[base + h], state[h].astype(BF16), preferred_element_type=F32) for h in range(H_C)]
        for h in range(H_C):
            o = r[h][:c] + qk_uw[base + h][:, :DV]
            yc_ref[rows(ch), cols(h)] = _gated_norm(o, ong_ref[...], gate_ref[rows(ch), cols(h)])
        state = [decay[base + h] * state[h] + r[h][c:] + kd_uw[base + h][:, :DV] for h in range(H_C)]
    for h in range(H_C):
        s_scr[h] = state[h]

    @pl.when(t == nt - 1)
    def _():
        sfin_ref[0] = s_scr[...]


def _delta_prompt(q, k, v, gb, gate, ong, n_b, t_len):
    tq = min(512, t_len)
    nt = t_len // tq
    row = lambda w: pl.BlockSpec((tq, w), lambda b, t: (b * nt + t, 0))
    return pl.pallas_call(
        functools.partial(_delta_prompt_kernel, tq=tq, nt=nt),
        grid=(n_b, nt),
        in_specs=[row(D_C), row(D_C), row(D_C), row(LANES), row(D_C), _const_spec(ong.shape)],
        out_specs=(row(D_C), pl.BlockSpec((1, H_C, DK, DV), lambda b, t: (b, 0, 0, 0))),
        out_shape=(jax.ShapeDtypeStruct((n_b * t_len, D_C), F32),
                   jax.ShapeDtypeStruct((n_b, H_C, DK, DV), F32)),
        scratch_shapes=[pltpu.VMEM((H_C, DK, DV), F32)],
        compiler_params=pltpu.CompilerParams(dimension_semantics=("arbitrary", "arbitrary"),
                                             vmem_limit_bytes=VMEM_LIMIT),
        name="delta_prompt",
    )(q, k, v, gb, gate, ong)


def _inproj_sample_kernel(x_ref, g1_ref, wmain_ref, wsmall_ref, caw_ref, cbw_ref, lng_ref, lnb_ref,
                          ccw_ref, alog_ref, dtb_ref, sta_ref, stb_ref, stq_ref,
                          yab_ref, q_ref, k_ref, v_ref, gate_ref, gb_ref, una_ref, unb_ref, unq_ref):
    z_a, z_b, z_qkv, z_gate, z_s = _in_projection(x_ref[...], g1_ref, wmain_ref, wsmall_ref)

    def conv_step(state_ref, new, w_ref, width):
        acc = new * w_ref[width - 1:width, :]
        for j in range(width - 1):
            acc = acc + state_ref[j] * w_ref[j:j + 1, :]
        return acc

    u_a = z_a[:, 256:512] * z_a[:, 512:768]
    una_ref[...] = u_a
    yab_ref[:, 0:D_A] = z_a[:, 0:256] * conv_step(sta_ref, u_a, caw_ref, W_A)

    u_b = z_b[:, 0:256] * _sigmoid(z_b[:, 256:512])
    unb_ref[...] = u_b
    yab_ref[:, D_A:D_A + D_B] = _layer_norm_silu(conv_step(stb_ref, u_b, cbw_ref, W_B), lng_ref[...], lnb_ref[...])

    unq_ref[...] = z_qkv
    qkv = _silu(conv_step(stq_ref, z_qkv, ccw_ref, W_C))
    q_ref[...] = _l2norm_heads(qkv[:, 0:D_C], DK ** -0.5)
    k_ref[...] = _l2norm_heads(qkv[:, D_C:2 * D_C], 1.0)
    v_ref[...] = qkv[:, 2 * D_C:3 * D_C]
    gate_ref[...] = z_gate
    g, beta = _decay_and_beta(z_s, alog_ref, dtb_ref)
    gb_ref[...] = g + beta


def _inproj_sample(x2d, g1, wmain, wsmall, caw, cbw, lng, lnb, ccw, alog, dtb, st_a, st_b, st_q):
    n = x2d.shape[0]
    args = (x2d, g1, wmain, wsmall, caw, cbw, lng, lnb, ccw, alog, dtb, st_a, st_b, st_q)
    out_widths = (D_A + D_B, D_C, D_C, D_C, D_C, LANES, D_A, D_B, 3 * D_C)
    return pl.pallas_call(
        _inproj_sample_kernel,
        grid=(1,),
        in_specs=[_const_spec(a.shape) for a in args],
        out_specs=tuple(_const_spec((n, w)) for w in out_widths),
        out_shape=tuple(jax.ShapeDtypeStruct((n, w), F32) for w in out_widths),
        compiler_params=pltpu.CompilerParams(dimension_semantics=("arbitrary",), vmem_limit_bytes=VMEM_LIMIT),
        name="inproj_sample",
    )(*args)


def _delta_sample_kernel(q_ref, k_ref, v_ref, gb_ref, gate_ref, ong_ref, s_ref, yc_ref, snew_ref, *, nb):
    q = q_ref[...]
    k = k_ref[...]
    rows = [k[:, h * DK:(h + 1) * DK] for h in range(H_C)] + [q[:, h * DK:(h + 1) * DK] for h in range(H_C)]
    rows.append(jnp.zeros((LANES - 2 * H_C * nb, DK), F32))
    kq_t = jnp.concatenate(rows, axis=0).T
    gb = gb_ref[...]
    inst = [(i, h) for i in range(nb) for h in range(H_C)]
    cols = lambda h: slice(h * DK, (h + 1) * DK)
    k_bc = [jnp.broadcast_to(kq_t[:, h * nb + i:h * nb + i + 1], (DK, DV)) for i, h in inst]
    q_bc = [jnp.broadcast_to(kq_t[:, (H_C + h) * nb + i:(H_C + h) * nb + i + 1], (DK, DV)) for i, h in inst]
    k_s = [jnp.sum(s_ref[i, h] * kb, axis=0, keepdims=True) for (i, h), kb in zip(inst, k_bc)]
    q_s = [jnp.sum(s_ref[i, h] * qb, axis=0, keepdims=True) for (i, h), qb in zip(inst, q_bc)]
    e_g = [jnp.exp(gb[i:i + 1, h:h + 1]) for i, h in inst]
    v_new = [gb[i:i + 1, H_C + h:H_C + h + 1] * (v_ref[i:i + 1, cols(h)] - eg * ks)
             for (i, h), eg, ks in zip(inst, e_g, k_s)]
    qk = [jnp.sum(q[i:i + 1, cols(h)] * k[i:i + 1, cols(h)], axis=-1, keepdims=True) for i, h in inst]
    for n, (i, h) in enumerate(inst):
        snew_ref[i, h] = e_g[n] * s_ref[i, h] + k_bc[n] * v_new[n]
    for n, (i, h) in enumerate(inst):
        o = e_g[n] * q_s[n] + qk[n] * v_new[n]
        yc_ref[i:i + 1, cols(h)] = _gated_norm(o, ong_ref[...], gate_ref[i:i + 1, cols(h)])


def _delta_sample(q, k, v, gb, gate, ong, state, layer):
    n = q.shape[0]
    nb = 8
    row = lambda w: pl.BlockSpec((nb, w), lambda i: (i, 0))
    st_in = pl.BlockSpec((None, nb, H_C, DK, DV), lambda i: (layer, i, 0, 0, 0))
    st = pl.BlockSpec((nb, H_C, DK, DV), lambda i: (i, 0, 0, 0))
    return pl.pallas_call(
        functools.partial(_delta_sample_kernel, nb=nb),
        grid=(n // nb,),
        in_specs=[row(D_C), row(D_C), row(D_C), row(LANES), row(D_C), _const_spec(ong.shape), st_in],
        out_specs=(row(D_C), st),
        out_shape=(jax.ShapeDtypeStruct((n, D_C), F32), jax.ShapeDtypeStruct(state.shape[1:], F32)),
        compiler_params=pltpu.CompilerParams(dimension_semantics=("arbitrary",), vmem_limit_bytes=VMEM_LIMIT),
        name="delta_sample",
    )(q, k, v, gb, gate, ong, state)


def _outproj_router_kernel(x_ref, yab_ref, yc_ref, xs_ref, yabs_ref, ycs_ref, woa_ref, woc_ref, g2_ref, wrt_ref,
                           x1_ref, h2_ref, route_ref, routet_ref, cnt_ref, *, n_prompt_tiles):
    is_prompt = pl.program_id(0) < n_prompt_tiles
    tm = x_ref.shape[0]

    def pick(prompt_ref, sample_ref):
        sample = sample_ref[...]
        sample = jnp.concatenate([sample, jnp.zeros((tm - sample.shape[0], sample.shape[1]), F32)], axis=0)
        return jnp.where(is_prompt, prompt_ref[...], sample)

    x1 = (pick(x_ref, xs_ref)
          + jnp.dot(pick(yab_ref, yabs_ref).astype(BF16), woa_ref[...], preferred_element_type=F32)
          + jnp.dot(pick(yc_ref, ycs_ref).astype(BF16), woc_ref[...], preferred_element_type=F32))
    x1_ref[...] = x1
    h2 = _rms(x1, g2_ref[...])
    h2_ref[...] = h2.astype(BF16)
    logits = _dot_3pass(h2, h2.astype(BF16), wrt_ref[...])
    lane = lax.broadcasted_iota(jnp.int32, logits.shape, 1)
    neg = -jnp.inf
    gl = jnp.where(lane < N_GROUPS, logits, neg)
    g_max = jnp.max(gl, axis=-1, keepdims=True)
    g_idx = jnp.min(jnp.where(gl == g_max, lane, LANES), axis=-1, keepdims=True)
    g_p = 1.0 / jnp.sum(jnp.exp(gl - g_max), axis=-1, keepdims=True)
    lo = EXPERT_LANE0 + g_idx * EXP_PER_GROUP
    in_group = (lane >= lo) & (lane < lo + EXP_PER_GROUP)
    el = jnp.where(in_group, logits, neg)
    e_max = jnp.max(el, axis=-1, keepdims=True)
    pe = jnp.exp(el - e_max)
    e_prob = pe / jnp.sum(pe, axis=-1, keepdims=True)
    p1 = jnp.max(e_prob, axis=-1, keepdims=True)
    i1 = jnp.min(jnp.where(e_prob == p1, lane, LANES), axis=-1, keepdims=True)
    rest = jnp.where(in_group & (lane != i1), e_prob, -1.0)
    p2 = jnp.max(rest, axis=-1, keepdims=True)
    i2 = jnp.min(jnp.where(rest == p2, lane, LANES), axis=-1, keepdims=True)
    denom = p1 + p2
    w1 = g_p * (p1 / denom)
    w2 = g_p * (p2 / denom)
    e1 = i1 - EXPERT_LANE0
    e2 = i2 - EXPERT_LANE0
    oh1 = lane == e1
    oh2 = lane == e2
    picked = jnp.where(oh1 | oh2, 1.0, 0.0)
    cnt = jnp.broadcast_to(jnp.sum(picked, axis=0, keepdims=True), (8, LANES))
    cnt_ref[0] = cnt
    ki = lax.broadcasted_iota(jnp.int32, (LANES, LANES), 0)
    kj = lax.broadcasted_iota(jnp.int32, (LANES, LANES), 1)
    pieces = jnp.ceil(cnt * (1.0 / PIECE)).astype(BF16)
    run_start = PIECE * jnp.dot(pieces, jnp.where(ki < kj, 1.0, 0.0).astype(BF16),
                                preferred_element_type=F32)[0:1, :]
    ri = lax.broadcasted_iota(jnp.int32, (tm, tm), 0)
    ci = lax.broadcasted_iota(jnp.int32, (tm, tm), 1)
    base = jnp.dot(jnp.where(ri > ci, 1.0, 0.0).astype(BF16), picked.astype(BF16),
                   preferred_element_type=F32) + run_start
    pos1 = jnp.sum(jnp.where(oh1, base, 0.0), axis=-1, keepdims=True)
    pos2 = jnp.sum(jnp.where(oh2, base, 0.0), axis=-1, keepdims=True)
    route = jnp.where(lane == 0, e1.astype(F32), jnp.where(lane == 1, e2.astype(F32), jnp.where(
        lane == 2, w1, jnp.where(lane == 3, w2, jnp.where(lane == 4, pos1, jnp.where(lane == 5, pos2, 0.0))))))
    route_ref[...] = route
    routet_ref[0] = route.T[0:8, :]


def _outproj_router(prompt, sample, woa, woc, g2, wrt):
    x2d = prompt[0]
    n, d = x2d.shape
    tm = min(TOKEN_TILE, n)
    n_prompt_tiles = n // tm
    n_all = n + tm
    assert sample[0].shape[0] <= tm and sample[0].shape[0] % 8 == 0
    p_row = lambda w: pl.BlockSpec((tm, w), lambda i: (jnp.minimum(i, n_prompt_tiles - 1), 0))
    row = lambda w: pl.BlockSpec((tm, w), lambda i: (i, 0))
    return pl.pallas_call(
        functools.partial(_outproj_router_kernel, n_prompt_tiles=n_prompt_tiles),
        grid=(n_prompt_tiles + 1,),
        in_specs=[p_row(d), p_row(D_A + D_B), p_row(D_C)] + [_const_spec(a.shape) for a in sample]
                 + [_const_spec(woa.shape), _const_spec(woc.shape), _const_spec(g2.shape), _const_spec(wrt.shape)],
        out_specs=(row(d), row(d), row(LANES), pl.BlockSpec((1, 8, tm), lambda i: (i, 0, 0)),
                   pl.BlockSpec((1, 8, LANES), lambda i: (i, 0, 0))),
        out_shape=(jax.ShapeDtypeStruct((n_all, d), F32), jax.ShapeDtypeStruct((n_all, d), BF16),
                   jax.ShapeDtypeStruct((n_all, LANES), F32),
                   jax.ShapeDtypeStruct((n_all // tm, 8, tm), F32),
                   jax.ShapeDtypeStruct((n_all // tm, 8, LANES), F32)),
        compiler_params=pltpu.CompilerParams(dimension_semantics=("arbitrary",), vmem_limit_bytes=VMEM_LIMIT),
        name="outproj_router",
    )(*prompt, *sample, woa, woc, g2, wrt)


def _route_plan(cnt, n_tok, tt):
    n_tiles = n_tok // tt
    cnt = cnt[:, 0, :N_EXPERTS].astype(jnp.int32)
    pc = (cnt + PIECE - 1) // PIECE * PIECE
    local_end = jnp.cumsum(pc, axis=1)
    local_off = local_end - pc
    seg_len = jnp.sum(pc, axis=0)
    seg_pad = (seg_len + ROW_TILE - 1) // ROW_TILE * ROW_TILE
    seg_end = jnp.cumsum(seg_pad)
    seg_start = seg_end - seg_pad
    run_off = seg_start[None, :] + jnp.cumsum(pc, axis=0) - pc
    piece_row = jnp.arange(_local_rows(tt) // PIECE, dtype=jnp.int32) * PIECE
    in_run = ((local_off[:, None, :] <= piece_row[None, :, None])
              & (piece_row[None, :, None] < local_end[:, None, :]))
    piece_dst = jnp.sum(jnp.where(in_run, (run_off - local_off)[:, None, :], 0), axis=-1) + piece_row[None, :]
    n_row_tiles = _max_sorted_rows(n_tok, tt) // ROW_TILE
    tile_row0 = jnp.arange(n_row_tiles, dtype=jnp.int32) * ROW_TILE
    tile_expert = jnp.minimum(jnp.sum((seg_end[None, :] <= tile_row0[:, None]).astype(jnp.int32), axis=1),
                              N_EXPERTS - 1)
    used = seg_pad > 0
    order = jnp.cumsum(used.astype(jnp.int32)) - used.astype(jnp.int32)
    ids = jnp.arange(N_EXPERTS, dtype=jnp.int32)
    later = jnp.where(used[None, :] & (ids[None, :] > ids[:, None]), ids[None, :], N_EXPERTS)
    next_used = jnp.min(later, axis=1)
    return dict(
        piece_dst=piece_dst.reshape(-1), tile_pieces=jnp.sum(pc, axis=1) // PIECE,
        fill_off=seg_start + seg_len, fill_pieces=(seg_pad - seg_len) // PIECE,
        tile_expert=tile_expert, n_active=(seg_end[-1] // ROW_TILE).reshape(1),
        expert_order=order, next_used=next_used,
    )


def _max_sorted_rows(n_tok, tt):
    n_tiles = n_tok // tt
    rows = 2 * n_tok + n_tiles * N_EXPERTS * (PIECE - 1) + N_EXPERTS * (ROW_TILE - 1)
    return (rows + ROW_TILE - 1) // ROW_TILE * ROW_TILE


def _local_rows(tt):
    return 2 * tt + N_EXPERTS * PIECE


def _piece_copy(src, dst, sem):
    return pltpu.make_async_copy(src, dst, sem)


def _dispatch_kernel(piece_dst, tile_pieces, fill_off, fill_pieces, n_active,
                     h2_ref, routet_ref, xs_hbm, lbuf, zbuf, sem, *, tt, n_tiles, n_row_tiles):
    i = pl.program_id(0)
    slot = i % 2
    lrows = _local_rows(tt)
    max_pieces = lrows // PIECE

    def wait_tile(tile, s):
        def body(_, c):
            _piece_copy(lbuf.at[s, pl.ds(0, PIECE)], xs_hbm.at[pl.ds(0, PIECE)], sem.at[s]).wait()
            return c
        lax.fori_loop(0, tile_pieces[tile], body, 0)

    @pl.when(i >= 2)
    def _():
        wait_tile(i - 2, slot)

    pos = routet_ref[0]
    h2 = h2_ref[...]
    for r0 in range(0, lrows, SORT_CHUNK):
        row = (lax.broadcasted_iota(jnp.int32, (SORT_CHUNK, tt), 0) + r0).astype(F32)
        perm = jnp.where((row == pos[4:5, :]) | (row == pos[5:6, :]), 1.0, 0.0).astype(BF16)
        lbuf[slot, r0:r0 + SORT_CHUNK] = jnp.dot(perm, h2, preferred_element_type=F32).astype(BF16)

    def send(p, c):
        dst = piece_dst[i * max_pieces + p]
        _piece_copy(lbuf.at[slot, pl.ds(pl.multiple_of(p * PIECE, PIECE), PIECE)],
                    xs_hbm.at[pl.ds(pl.multiple_of(dst, PIECE), PIECE)], sem.at[slot]).start()
        return c
    lax.fori_loop(0, tile_pieces[i], send, 0)

    @pl.when(i == n_tiles - 1)
    def _():
        zbuf[...] = jnp.zeros(zbuf.shape, BF16)
        zpiece = zbuf.at[pl.ds(0, PIECE)]
        for e in range(N_EXPERTS):
            def body(p, c, e=e):
                _piece_copy(zpiece, xs_hbm.at[pl.ds(pl.multiple_of(fill_off[e] + p * PIECE, PIECE), PIECE)],
                            sem.at[2]).start()
                return c
            lax.fori_loop(0, fill_pieces[e], body, 0)

        def tail_body(j, c):
            _piece_copy(zbuf, xs_hbm.at[pl.ds(pl.multiple_of(j * ROW_TILE, ROW_TILE), ROW_TILE)], sem.at[3]).start()
            return c
        lax.fori_loop(n_active[0], n_row_tiles, tail_body, 0)
        for e in range(N_EXPERTS):
            def body(p, c):
                _piece_copy(zpiece, xs_hbm.at[pl.ds(0, PIECE)], sem.at[2]).wait()
                return c
            lax.fori_loop(0, fill_pieces[e], body, 0)

        def tail_wait(j, c):
            _piece_copy(zbuf, xs_hbm.at[pl.ds(0, ROW_TILE)], sem.at[3]).wait()
            return c
        lax.fori_loop(n_active[0], n_row_tiles, tail_wait, 0)
        if n_tiles > 1:
            wait_tile(i - 1, 1 - slot)
        wait_tile(i, slot)


def _dispatch(h2, routet, plan, tt):
    n, d = h2.shape
    n_tiles = n // tt
    n_rows = _max_sorted_rows(n, tt)
    kernel_fn = functools.partial(_dispatch_kernel, tt=tt, n_tiles=n_tiles, n_row_tiles=n_rows // ROW_TILE)
    grid_spec = pltpu.PrefetchScalarGridSpec(
        num_scalar_prefetch=5,
        grid=(n_tiles,),
        in_specs=[pl.BlockSpec((tt, d), lambda i, *_: (i, 0)),
                  pl.BlockSpec((1, 8, tt), lambda i, *_: (i, 0, 0))],
        out_specs=pl.BlockSpec(memory_space=pl.ANY),
        scratch_shapes=[pltpu.VMEM((2, _local_rows(tt), d), BF16), pltpu.VMEM((ROW_TILE, d), BF16),
                        pltpu.SemaphoreType.DMA((4,))],
    )
    return pl.pallas_call(
        kernel_fn,
        grid_spec=grid_spec,
        out_shape=jax.ShapeDtypeStruct((n_rows, d), BF16),
        compiler_params=pltpu.CompilerParams(dimension_semantics=("arbitrary",), vmem_limit_bytes=VMEM_LIMIT),
        name="moe_dispatch",
    )(plan["piece_dst"], plan["tile_pieces"], plan["fill_off"], plan["fill_pieces"], plan["n_active"],
      h2, routet)


def _experts_kernel(tile_expert, n_active, expert_order, next_used,
                    xs_ref, wg_hbm, wu_hbm, wd_hbm, ys_ref, wbuf_g, wbuf_u, wbuf_d, wg_b, wu_b, wd_b, sem,
                    *, layer):
    j = pl.program_id(0)
    expert = tile_expert[j]
    previous = tile_expert[jnp.maximum(j - 1, 0)]
    active = j < n_active[0]
    half = expert_order[expert] % 2

    def weight_copies(e, s):
        return (_piece_copy(wg_hbm.at[layer, e], wbuf_g.at[s], sem.at[s, 0]),
                _piece_copy(wu_hbm.at[layer, e], wbuf_u.at[s], sem.at[s, 1]),
                _piece_copy(wd_hbm.at[layer, e], wbuf_d.at[s], sem.at[s, 2]))

    @pl.when(j == 0)
    def _():
        for c in weight_copies(expert, half):
            c.start()

    @pl.when(active & ((j == 0) | (expert != previous)))
    def _():
        for c in weight_copies(expert, half):
            c.wait()
        wg_b[...] = wbuf_g[half].astype(BF16)
        wu_b[...] = wbuf_u[half].astype(BF16)
        wd_b[...] = wbuf_d[half].astype(BF16)
        upcoming = next_used[expert]

        @pl.when(upcoming < N_EXPERTS)
        def _():
            for c in weight_copies(upcoming, 1 - half):
                c.start()

    @pl.when(active)
    def _():
        x = xs_ref[...]
        gate = jnp.dot(x, wg_b[...], preferred_element_type=F32)
        up = jnp.dot(x, wu_b[...], preferred_element_type=F32)
        act = (_silu(gate) * up).astype(BF16)
        ys_ref[...] = jnp.dot(act, wd_b[...], preferred_element_type=F32).astype(BF16)

    @pl.when(jnp.logical_not(active))
    def _():
        ys_ref[...] = jnp.zeros(ys_ref.shape, BF16)


def _experts(xs, wg, wu, wd, layer, plan):
    rows, d = xs.shape
    d_e = wg.shape[-1]
    live = lambda j, te, na, *_: (jnp.minimum(j, na[0] - 1), 0)
    grid_spec = pltpu.PrefetchScalarGridSpec(
        num_scalar_prefetch=4,
        grid=(rows // ROW_TILE,),
        in_specs=[pl.BlockSpec((ROW_TILE, d), live),
                  pl.BlockSpec(memory_space=pl.ANY), pl.BlockSpec(memory_space=pl.ANY),
                  pl.BlockSpec(memory_space=pl.ANY)],
        out_specs=pl.BlockSpec((ROW_TILE, d), lambda j, *_: (j, 0)),
        scratch_shapes=[pltpu.VMEM((2, d, d_e), F32), pltpu.VMEM((2, d, d_e), F32), pltpu.VMEM((2, d_e, d), F32),
                        pltpu.VMEM((d, d_e), BF16), pltpu.VMEM((d, d_e), BF16), pltpu.VMEM((d_e, d), BF16),
                        pltpu.SemaphoreType.DMA((2, 3))],
    )
    return pl.pallas_call(
        functools.partial(_experts_kernel, layer=layer),
        grid_spec=grid_spec,
        out_shape=jax.ShapeDtypeStruct((rows, d), BF16),
        compiler_params=pltpu.CompilerParams(dimension_semantics=("arbitrary",), vmem_limit_bytes=VMEM_LIMIT),
        name="moe_experts",
    )(plan["tile_expert"], plan["n_active"], plan["expert_order"], plan["next_used"], xs, wg, wu, wd)


def _combine_kernel(piece_src, tile_pieces, x1_ref, route_ref, fg_ref, ys_hbm, out_ref, outs_ref, lbuf, sem,
                    *, tt, n_tiles, final_norm):
    i = pl.program_id(0)
    slot = i % 2
    lrows = _local_rows(tt)
    max_pieces = lrows // PIECE

    def fetch_tile(tile, s):
        def body(p, c):
            src = piece_src[tile * max_pieces + p]
            _piece_copy(ys_hbm.at[pl.ds(pl.multiple_of(src, PIECE), PIECE)],
                        lbuf.at[s, pl.ds(pl.multiple_of(p * PIECE, PIECE), PIECE)], sem.at[s]).start()
            return c
        lax.fori_loop(0, tile_pieces[tile], body, 0)

    @pl.when(i == 0)
    def _():
        lbuf[...] = jnp.zeros(lbuf.shape, BF16)
        fetch_tile(0, 0)

    @pl.when(i + 1 < n_tiles)
    def _():
        fetch_tile(i + 1, 1 - slot)

    def wait_body(_, c):
        _piece_copy(ys_hbm.at[pl.ds(0, PIECE)], lbuf.at[slot, pl.ds(0, PIECE)], sem.at[slot]).wait()
        return c
    lax.fori_loop(0, tile_pieces[i], wait_body, 0)

    route = route_ref[...]
    out = x1_ref[...]
    for r0 in range(0, lrows, SORT_CHUNK):
        col = (lax.broadcasted_iota(jnp.int32, (tt, SORT_CHUNK), 1) + r0).astype(F32)
        weights = (jnp.where(col == route[:, 4:5], route[:, 2:3], 0.0)
                   + jnp.where(col == route[:, 5:6], route[:, 3:4], 0.0)).astype(BF16)
        out = out + jnp.dot(weights, lbuf[slot, r0:r0 + SORT_CHUNK], preferred_element_type=F32)
    out = _rms(out, fg_ref[...]) if final_norm else out

    @pl.when(i < n_tiles - 1)
    def _():
        out_ref[...] = out

    @pl.when(i == n_tiles - 1)
    def _():
        outs_ref[...] = out[0:outs_ref.shape[0]]


def _combine(x1, route, ys, fg, plan, tt, n_sample, final_norm):
    n, d = x1.shape
    n_tiles = n // tt
    kernel_fn = functools.partial(_combine_kernel, tt=tt, n_tiles=n_tiles, final_norm=final_norm)
    grid_spec = pltpu.PrefetchScalarGridSpec(
        num_scalar_prefetch=2,
        grid=(n_tiles,),
        in_specs=[pl.BlockSpec((tt, d), lambda i, *_: (i, 0)),
                  pl.BlockSpec((tt, LANES), lambda i, *_: (i, 0)),
                  pl.BlockSpec(fg.shape, lambda i, *_: (0, 0)),
                  pl.BlockSpec(memory_space=pl.ANY)],
        out_specs=(pl.BlockSpec((tt, d), lambda i, *_: (jnp.minimum(i, n_tiles - 2), 0)),
                   pl.BlockSpec((n_sample, d), lambda i, *_: (0, 0))),
        scratch_shapes=[pltpu.VMEM((2, _local_rows(tt), d), BF16), pltpu.SemaphoreType.DMA((2,))],
    )
    return pl.pallas_call(
        kernel_fn,
        grid_spec=grid_spec,
        out_shape=(jax.ShapeDtypeStruct((n - tt, d), F32), jax.ShapeDtypeStruct((n_sample, d), F32)),
        compiler_params=pltpu.CompilerParams(dimension_semantics=("arbitrary",), vmem_limit_bytes=VMEM_LIMIT),
        name="moe_combine",
    )(plan["piece_dst"], plan["tile_pieces"], x1, route, fg, ys)


def _routed_moe(x1, h2, route, routet, cnt, wg, wu, wd, layer, fg, n_sample, final_norm):
    n = x1.shape[0]
    tt = n // cnt.shape[0]
    plan = _route_plan(cnt, n, tt)
    xs = _dispatch(h2, routet, plan, tt)
    ys = _experts(xs, wg, wu, wd, layer, plan)
    return _combine(x1, route, ys, fg, plan, tt, n_sample, final_norm)


def _pad_lanes(a, lane0=0):
    a = a.reshape((1, -1)) if a.ndim == 1 else a
    return jnp.pad(a, ((0, 0), (lane0, LANES - lane0 - a.shape[1])))


def kernel(x_prompt, x_sample, state_conv_a, state_conv_b, state_conv_qkv, state_delta, norm1_g, w_in, conv_a_w,
           conv_b_w, ln_b_g, ln_b_b, conv_c_w, a_log, dt_bias, o_norm_g, w_out, norm2_g, w_group, w_router,
           w_gate, w_up, w_down, final_g):
    n_b, t_len, d = x_prompt.shape
    n_s = x_sample.shape[0]
    depth = w_in.shape[0]
    n_main = 3 * D_A + 2 * D_B + 3 * D_C + D_C
    xp = x_prompt.reshape(n_b * t_len, d)
    xs = x_sample.reshape(n_s, d)
    fg = final_g.reshape(1, d)
    outs = {k: [] for k in ("pa", "pb", "pq", "pd", "sa", "sb", "sq", "sd")}
    for l in range(depth):
        g1 = norm1_g[l].reshape(1, d)
        wmain = w_in[l][:, :n_main].astype(BF16)
        wsmall = _pad_lanes(jnp.concatenate([w_in[l][:, n_main + H_C:], w_in[l][:, n_main:n_main + H_C]], axis=1))
        caw, cbw, ccw = conv_a_w[l], conv_b_w[l], conv_c_w[l]
        lng, lnb = ln_b_g[l].reshape(1, D_B), ln_b_b[l].reshape(1, D_B)
        alog, dtb = _pad_lanes(a_log[l]), _pad_lanes(dt_bias[l])
        ong = o_norm_g[l].reshape(1, DV)
        woa = w_out[l][:D_A + D_B].astype(BF16)
        woc = w_out[l][D_A + D_B:].astype(BF16)
        g2 = norm2_g[l].reshape(1, d)
        wrt = _pad_lanes(jnp.concatenate(
            [w_group[l], w_router[l].transpose(1, 0, 2).reshape(d, N_EXPERTS)], axis=1))
        last = l == depth - 1

        yab, q, k, v, gate, gb, st_a, st_b, st_q = _inproj_prompt(
            xp, n_b, t_len, g1, wmain, wsmall, caw, cbw, lng, lnb, ccw, alog, dtb)
        yc, s_fin = _delta_prompt(q, k, v, gb, gate, ong, n_b, t_len)
        outs["pa"].append(st_a); outs["pb"].append(st_b); outs["pq"].append(st_q); outs["pd"].append(s_fin)

        hist_a = state_conv_a[l].transpose(1, 0, 2)
        hist_b = state_conv_b[l].transpose(1, 0, 2)
        hist_q = state_conv_qkv[l].transpose(1, 0, 2)
        yab_s, q, k, v, gate, gb, un_a, un_b, un_q = _inproj_sample(
            xs, g1, wmain, wsmall, caw, cbw, lng, lnb, ccw, alog, dtb, hist_a, hist_b, hist_q)
        yc_s, s_new = _delta_sample(q, k, v, gb, gate, ong, state_delta, l)

        x1, h2, route, routet, cnt = _outproj_router((xp, yab, yc), (xs, yab_s, yc_s), woa, woc, g2, wrt)
        xp, xs = _routed_moe(x1, h2, route, routet, cnt, w_gate, w_up, w_down, l, fg, n_s, last)
        outs["sa"].append(jnp.concatenate([state_conv_a[l][:, 1:], un_a[:, None]], axis=1))
        outs["sb"].append(jnp.concatenate([state_conv_b[l][:, 1:], un_b[:, None]], axis=1))
        outs["sq"].append(jnp.concatenate([state_conv_qkv[l][:, 1:], un_q[:, None]], axis=1))
        outs["sd"].append(s_new)

    stack = lambda key: jnp.stack(outs[key])
    return (xp.reshape(n_b, t_len, d), xs.reshape(n_s, 1, d),
            stack("pa"), stack("pb"), stack("pq"), stack("pd"),
            stack("sa"), stack("sb"), stack("sq"), stack("sd"))
```

```python
import functools

import jax
import jax.numpy as jnp
from jax import lax
from jax.experimental import pallas as pl
from jax.experimental.pallas import tpu as pltpu

F32 = jnp.float32
BF16 = jnp.bfloat16

EPS = 1e-6
H_C = 4
DK = 128
DV = 128
D_A = 256
D_B = 256
D_C = 512
W_A = 3
W_B = 31
W_C = 4
CHUNK = 64
N_GROUPS = 4
EXP_PER_GROUP = 8
N_EXPERTS = N_GROUPS * EXP_PER_GROUP
LANES = 128
PAD_A = 8
PAD_B = 32
PAD_C = 8
EXPERT_LANE0 = N_GROUPS
VMEM_LIMIT = 56 * 1024 * 1024
TOKEN_TILE = 512
PIECE = 16
ROW_TILE = 512
SORT_CHUNK = 256


def _silu(x):
    return x * (1.0 / (1.0 + jnp.exp(-x)))


def _sigmoid(x):
    return 1.0 / (1.0 + jnp.exp(-x))


def _softplus(x):
    return jnp.maximum(x, 0.0) + jnp.log1p(jnp.exp(-jnp.abs(x)))


def _rms(x, g):
    return x * lax.rsqrt(jnp.mean(x * x, axis=-1, keepdims=True) + EPS) * g


def _conv_taps(buf, w_ref, width, pad, n_rows, n_cols, row_blk, col_blk):
    row_parts = []
    for r0 in range(0, n_rows, row_blk):
        col_parts = []
        for c0 in range(0, n_cols, col_blk):
            acc = None
            for j in range(width):
                start = pad - (width - 1) + j + r0
                term = buf[start:start + row_blk, c0:c0 + col_blk] * w_ref[j:j + 1, c0:c0 + col_blk]
                acc = term if acc is None else acc + term
            col_parts.append(acc)
        row_parts.append(col_parts[0] if len(col_parts) == 1 else jnp.concatenate(col_parts, axis=1))
    return row_parts[0] if len(row_parts) == 1 else jnp.concatenate(row_parts, axis=0)


def _conv_taps_realigned(buf, shifted, w_ref, width, pad, n_rows, row_blk):
    first = pad - (width - 1)
    n_shift = n_rows + (first + width - 1) // 8 * 8 - 8
    for r in range(1, 8):
        shifted[r - 1] = buf[r:r + n_shift, :]
    row_parts = []
    for r0 in range(0, n_rows, row_blk):
        acc = None
        for j in range(width):
            q, r = divmod(first + j, 8)
            lo = 8 * q + r0
            window = buf[lo:lo + row_blk, :] if r == 0 else shifted[r - 1, lo:lo + row_blk, :]
            term = window * w_ref[j:j + 1, :]
            acc = term if acc is None else acc + term
        row_parts.append(acc)
    return row_parts[0] if len(row_parts) == 1 else jnp.concatenate(row_parts, axis=0)


def _l2norm_heads(x, scale):
    parts = []
    for h in range(H_C):
        xh = x[:, h * DK:(h + 1) * DK]
        parts.append(xh * (lax.rsqrt(jnp.sum(xh * xh, axis=-1, keepdims=True) + 1e-6) * scale))
    return jnp.concatenate(parts, axis=1)


def _dot_3pass(a, a_hi, w):
    a_lo = (a - a_hi.astype(F32)).astype(BF16)
    w_hi = w.astype(BF16)
    w_lo = (w - w_hi.astype(F32)).astype(BF16)
    n = w.shape[1]
    both = jnp.dot(a_hi, jnp.concatenate([w_hi, w_lo], axis=1), preferred_element_type=F32)
    return both[:, :n] + both[:, n:] + jnp.dot(a_lo, w_hi, preferred_element_type=F32)


def _in_projection(x, g1_ref, wmain_ref, wsmall_ref):
    h = _rms(x, g1_ref[...])
    hb = h.astype(BF16)
    z_a = jnp.dot(hb, wmain_ref[:, 0:768], preferred_element_type=F32)
    z_b = jnp.dot(hb, wmain_ref[:, 768:1280], preferred_element_type=F32)
    z_qkv = jnp.dot(hb, wmain_ref[:, 1280:2816], preferred_element_type=F32)
    z_gate = jnp.dot(hb, wmain_ref[:, 2816:3328], preferred_element_type=F32)
    z_s = _dot_3pass(h, hb, wsmall_ref[...])
    return z_a, z_b, z_qkv, z_gate, z_s


def _layer_norm_silu(x, g, b):
    mu = jnp.mean(x, axis=-1, keepdims=True)
    xc = x - mu
    y = xc * lax.rsqrt(jnp.mean(xc * xc, axis=-1, keepdims=True) + EPS)
    return _silu(y * g + b)


def _decay_and_beta(z_s, alog_ref, dtb_ref):
    lane = lax.broadcasted_iota(jnp.int32, z_s.shape, 1)
    g = -jnp.exp(alog_ref[...]) * _softplus(z_s + dtb_ref[...])
    g = jnp.where(lane < H_C, g, 0.0)
    beta = jnp.where((lane >= H_C) & (lane < 2 * H_C), _sigmoid(z_s), 0.0)
    return g, beta


def _inproj_prompt_kernel(x_ref, g1_ref, wmain_ref, wsmall_ref, caw_ref, cbw_ref, lng_ref, lnb_ref,
                          ccw_ref, alog_ref, dtb_ref,
                          yab_ref, q_ref, k_ref, v_ref, gate_ref, gb_ref, sta_ref, stb_ref, stq_ref,
                          abuf, bbuf, cbuf, bshift, *, tt, nt):
    t = pl.program_id(1)

    @pl.when(t == 0)
    def _():
        abuf[0:PAD_A, :] = jnp.zeros((PAD_A, D_A), F32)
        bbuf[0:PAD_B, :] = jnp.zeros((PAD_B, D_B), F32)
        cbuf[0:PAD_C, :] = jnp.zeros((PAD_C, 3 * D_C), F32)

    h = _rms(x_ref[...], g1_ref[...])
    hb = h.astype(BF16)
    project = lambda lo, hi: jnp.dot(hb, wmain_ref[:, lo:hi], preferred_element_type=F32)

    z_a = project(0, 3 * D_A)
    abuf[PAD_A:PAD_A + tt, :] = z_a[:, 256:512] * z_a[:, 512:768]
    conv_a = _conv_taps(abuf, caw_ref, W_A, PAD_A, tt, D_A, 64, 256)
    yab_ref[:, 0:D_A] = z_a[:, 0:256] * conv_a

    z_b = project(3 * D_A, 3 * D_A + 2 * D_B)
    bbuf[PAD_B:PAD_B + tt, :] = z_b[:, 0:256] * _sigmoid(z_b[:, 256:512])
    conv_b = _conv_taps_realigned(bbuf, bshift, cbw_ref, W_B, PAD_B, tt, 64)
    yab_ref[:, D_A:D_A + D_B] = _layer_norm_silu(conv_b, lng_ref[...], lnb_ref[...])

    qkv0 = 3 * D_A + 2 * D_B
    for part, (out_ref, scale) in enumerate(((q_ref, DK ** -0.5), (k_ref, 1.0), (v_ref, None))):
        c0 = part * D_C
        cbuf[PAD_C:PAD_C + tt, c0:c0 + D_C] = project(qkv0 + c0, qkv0 + c0 + D_C)
        conv = _silu(_conv_taps(cbuf.at[:, c0:c0 + D_C], ccw_ref.at[:, c0:c0 + D_C], W_C, PAD_C, tt, D_C, 64, 512))
        out_ref[...] = conv if scale is None else _l2norm_heads(conv, scale)
    gate_ref[...] = project(qkv0 + 3 * D_C, qkv0 + 4 * D_C)

    z_s = _dot_3pass(h, hb, wsmall_ref[...])
    g, beta = _decay_and_beta(z_s, alog_ref, dtb_ref)
    row_in_chunk = lax.broadcasted_iota(jnp.int32, g.shape, 0) & (CHUNK - 1)
    shift = 1
    while shift < CHUNK:
        g = g + jnp.where(row_in_chunk >= shift, pltpu.roll(g, shift, axis=0), 0.0)
        shift *= 2
    gb_ref[...] = g + beta

    @pl.when(t == nt - 1)
    def _():
        sta_ref[0] = abuf[PAD_A + tt - (W_A - 1):PAD_A + tt, :]
        stb_ref[0] = bbuf[PAD_B + tt - (W_B - 1):PAD_B + tt, :]
        stq_ref[0] = cbuf[PAD_C + tt - (W_C - 1):PAD_C + tt, :]

    abuf[0:PAD_A, :] = abuf[tt:tt + PAD_A, :]
    bbuf[0:PAD_B, :] = bbuf[tt:tt + PAD_B, :]
    cbuf[0:PAD_C, :] = cbuf[tt:tt + PAD_C, :]


def _const_spec(shape):
    nd = len(shape)
    return pl.BlockSpec(shape, lambda *_: (0,) * nd)


def _inproj_prompt(x2d, n_b, t_len, g1, wmain, wsmall, caw, cbw, lng, lnb, ccw, alog, dtb):
    tt = min(512, t_len)
    nt = t_len // tt
    n_tok = n_b * t_len
    d = x2d.shape[1]
    row = lambda w: pl.BlockSpec((tt, w), lambda b, t: (b * nt + t, 0))
    st = lambda r, w: pl.BlockSpec((1, r, w), lambda b, t: (b, 0, 0))
    out_shape = (
        jax.ShapeDtypeStruct((n_tok, D_A + D_B), F32),
        jax.ShapeDtypeStruct((n_tok, D_C), F32),
        jax.ShapeDtypeStruct((n_tok, D_C), F32),
        jax.ShapeDtypeStruct((n_tok, D_C), F32),
        jax.ShapeDtypeStruct((n_tok, D_C), F32),
        jax.ShapeDtypeStruct((n_tok, LANES), F32),
        jax.ShapeDtypeStruct((n_b, W_A - 1, D_A), F32),
        jax.ShapeDtypeStruct((n_b, W_B - 1, D_B), F32),
        jax.ShapeDtypeStruct((n_b, W_C - 1, 3 * D_C), F32),
    )
    return pl.pallas_call(
        functools.partial(_inproj_prompt_kernel, tt=tt, nt=nt),
        grid=(n_b, nt),
        in_specs=[row(d), _const_spec(g1.shape), _const_spec(wmain.shape), _const_spec(wsmall.shape),
                  _const_spec(caw.shape), _const_spec(cbw.shape), _const_spec(lng.shape), _const_spec(lnb.shape),
                  _const_spec(ccw.shape), _const_spec(alog.shape), _const_spec(dtb.shape)],
        out_specs=(row(D_A + D_B), row(D_C), row(D_C), row(D_C), row(D_C), row(LANES),
                   st(W_A - 1, D_A), st(W_B - 1, D_B), st(W_C - 1, 3 * D_C)),
        out_shape=out_shape,
        scratch_shapes=[pltpu.VMEM((PAD_A + tt, D_A), F32), pltpu.VMEM((PAD_B + tt, D_B), F32),
                        pltpu.VMEM((PAD_C + tt, 3 * D_C), F32), pltpu.VMEM((7, PAD_B + tt - 8, D_B), F32)],
        compiler_params=pltpu.CompilerParams(dimension_semantics=("arbitrary", "arbitrary"),
                                             vmem_limit_bytes=VMEM_LIMIT),
        name="inproj_prompt",
    )(x2d, g1, wmain, wsmall, caw, cbw, lng, lnb, ccw, alog, dtb)


def _dot_nt(a, b):
    return lax.dot_general(a, b, (((1,), (1,)), ((), ())), preferred_element_type=F32)


def _dot_tn(a, b):
    return lax.dot_general(a, b, (((0,), (0,)), ((), ())), preferred_element_type=F32)


def _bdot(a, b):
    return jnp.dot(a.astype(BF16), b.astype(BF16), preferred_element_type=F32)


def _gated_norm(o, on_g, gate):
    return o * lax.rsqrt(jnp.mean(o * o, axis=-1, keepdims=True) + EPS) * on_g * _silu(gate)


def _delta_prompt_kernel(q_ref, k_ref, v_ref, gb_ref, gate_ref, ong_ref, yc_ref, sfin_ref, s_scr, *, tq, nt):
    t = pl.program_id(1)

    @pl.when(t == 0)
    def _():
        s_scr[...] = jnp.zeros(s_scr.shape, F32)

    c = CHUNK
    n_chunks = tq // c
    n_double = c.bit_length() - 2
    inst = [(ch, h) for ch in range(n_chunks) for h in range(H_C)]
    rows = lambda ch: slice(ch * c, (ch + 1) * c)
    cols = lambda h: slice(h * DK, (h + 1) * DK)
    ri = lax.broadcasted_iota(jnp.int32, (c, c), 0)
    ci = lax.broadcasted_iota(jnp.int32, (c, c), 1)
    gb = gb_ref[...]

    q = [q_ref[rows(ch), cols(h)] for ch, h in inst]
    k = [k_ref[rows(ch), cols(h)] for ch, h in inst]
    v = [v_ref[rows(ch), cols(h)] for ch, h in inst]
    g_col = [gb[rows(ch), h:h + 1] for ch, h in inst]
    b_col = [gb[rows(ch), H_C + h:H_C + h + 1] for ch, h in inst]
    g_row = [jnp.sum(jnp.where(ri == ci, g, 0.0), axis=0, keepdims=True) for g in g_col]
    diff = [gc - gr for gc, gr in zip(g_col, g_row)]
    dec_strict = [jnp.exp(jnp.where(ri > ci, d, -jnp.inf)) for d in diff]
    dec_incl = [jnp.exp(jnp.where(ri >= ci, d, -jnp.inf)) for d in diff]
    e_g = [jnp.exp(g) for g in g_col]
    g_last = [g[c - 1:c, :] for g in g_col]

    qkk = [_dot_nt(jnp.concatenate([qi, ki], axis=0).astype(BF16), ki.astype(BF16)) for qi, ki in zip(q, k)]
    qk = [m[:c] * d for m, d in zip(qkk, dec_incl)]
    p = [-(b * m[c:] * d) for b, m, d in zip(b_col, qkk, dec_strict)]
    sol = [jnp.concatenate([vi * b, ki * (b * eg)], axis=1) for vi, ki, b, eg in zip(v, k, b_col, e_g)]
    for level in range(n_double + 1):
        sol = [s + _bdot(pi, s) for pi, s in zip(p, sol)]
        if level < n_double:
            p = [_bdot(pi, pi) for pi in p]
    kd = [ki * jnp.exp(gl - g) for ki, gl, g in zip(k, g_last, g_col)]
    qk_uw = [_bdot(m, s) for m, s in zip(qk, sol)]
    kd_uw = [_dot_tn(x.astype(BF16), s.astype(BF16)) for x, s in zip(kd, sol)]
    lhs = [jnp.concatenate([qi * eg - a[:, DV:], -b[:, DV:]], axis=0).astype(BF16)
           for qi, eg, a, b in zip(q, e_g, qk_uw, kd_uw)]
    decay = [jnp.exp(gl) for gl in g_last]

    state = [s_scr[h] for h in range(H_C)]
    for ch in range(n_chunks):
        base = ch * H_C
        r = [jnp.dot(lhs[base + h], state[h].astype(BF16), preferred_element_type=F32) for h in range(H_C)]
        for h in range(H_C):
            o = r[h][:c] + qk_uw[base + h][:, :DV]
            yc_ref[rows(ch), cols(h)] = _gated_norm(o, ong_ref[...], gate_ref[rows(ch), cols(h)])
        state = [decay[base + h] * state[h] + r[h][c:] + kd_uw[base + h][:, :DV] for h in range(H_C)]
    for h in range(H_C):
        s_scr[h] = state[h]

    @pl.when(t == nt - 1)
    def _():
        sfin_ref[0] = s_scr[...]


def _delta_prompt(q, k, v, gb, gate, ong, n_b, t_len):
    tq = min(512, t_len)
    nt = t_len // tq
    row = lambda w: pl.BlockSpec((tq, w), lambda b, t: (b * nt + t, 0))
    return pl.pallas_call(
        functools.partial(_delta_prompt_kernel, tq=tq, nt=nt),
        grid=(n_b, nt),
        in_specs=[row(D_C), row(D_C), row(D_C), row(LANES), row(D_C), _const_spec(ong.shape)],
        out_specs=(row(D_C), pl.BlockSpec((1, H_C, DK, DV), lambda b, t: (b, 0, 0, 0))),
        out_shape=(jax.ShapeDtypeStruct((n_b * t_len, D_C), F32),
                   jax.ShapeDtypeStruct((n_b, H_C, DK, DV), F32)),
        scratch_shapes=[pltpu.VMEM((H_C, DK, DV), F32)],
        compiler_params=pltpu.CompilerParams(dimension_semantics=("arbitrary", "arbitrary"),
                                             vmem_limit_bytes=VMEM_LIMIT),
        name="delta_prompt",
    )(q, k, v, gb, gate, ong)


def _inproj_sample_kernel(x_ref, g1_ref, wmain_ref, wsmall_ref, caw_ref, cbw_ref, lng_ref, lnb_ref,
                          ccw_ref, alog_ref, dtb_ref, sta_ref, stb_ref, stq_ref,
                          yab_ref, q_ref, k_ref, v_ref, gate_ref, gb_ref, una_ref, unb_ref, unq_ref):
    z_a, z_b, z_qkv, z_gate, z_s = _in_projection(x_ref[...], g1_ref, wmain_ref, wsmall_ref)

    def conv_step(state_ref, new, w_ref, width):
        acc = new * w_ref[width - 1:width, :]
        for j in range(width - 1):
            acc = acc + state_ref[j] * w_ref[j:j + 1, :]
        return acc

    u_a = z_a[:, 256:512] * z_a[:, 512:768]
    una_ref[...] = u_a
    yab_ref[:, 0:D_A] = z_a[:, 0:256] * conv_step(sta_ref, u_a, caw_ref, W_A)

    u_b = z_b[:, 0:256] * _sigmoid(z_b[:, 256:512])
    unb_ref[...] = u_b
    yab_ref[:, D_A:D_A + D_B] = _layer_norm_silu(conv_step(stb_ref, u_b, cbw_ref, W_B), lng_ref[...], lnb_ref[...])

    unq_ref[...] = z_qkv
    qkv = _silu(conv_step(stq_ref, z_qkv, ccw_ref, W_C))
    q_ref[...] = _l2norm_heads(qkv[:, 0:D_C], DK ** -0.5)
    k_ref[...] = _l2norm_heads(qkv[:, D_C:2 * D_C], 1.0)
    v_ref[...] = qkv[:, 2 * D_C:3 * D_C]
    gate_ref[...] = z_gate
    g, beta = _decay_and_beta(z_s, alog_ref, dtb_ref)
    gb_ref[...] = g + beta


def _inproj_sample(x2d, g1, wmain, wsmall, caw, cbw, lng, lnb, ccw, alog, dtb, st_a, st_b, st_q):
    n = x2d.shape[0]
    args = (x2d, g1, wmain, wsmall, caw, cbw, lng, lnb, ccw, alog, dtb, st_a, st_b, st_q)
    out_widths = (D_A + D_B, D_C, D_C, D_C, D_C, LANES, D_A, D_B, 3 * D_C)
    return pl.pallas_call(
        _inproj_sample_kernel,
        grid=(1,),
        in_specs=[_const_spec(a.shape) for a in args],
        out_specs=tuple(_const_spec((n, w)) for w in out_widths),
        out_shape=tuple(jax.ShapeDtypeStruct((n, w), F32) for w in out_widths),
        compiler_params=pltpu.CompilerParams(dimension_semantics=("arbitrary",), vmem_limit_bytes=VMEM_LIMIT),
        name="inproj_sample",
    )(*args)


def _delta_sample_kernel(q_ref, k_ref, v_ref, gb_ref, gate_ref, ong_ref, s_ref, yc_ref, snew_ref, *, nb):
    q = q_ref[...]
    k = k_ref[...]
    rows = [k[:, h * DK:(h + 1) * DK] for h in range(H_C)] + [q[:, h * DK:(h + 1) * DK] for h in range(H_C)]
    rows.append(jnp.zeros((LANES - 2 * H_C * nb, DK), F32))
    kq_t = jnp.concatenate(rows, axis=0).T
    gb = gb_ref[...]
    inst = [(i, h) for i in range(nb) for h in range(H_C)]
    cols = lambda h: slice(h * DK, (h + 1) * DK)
    k_bc = [jnp.broadcast_to(kq_t[:, h * nb + i:h * nb + i + 1], (DK, DV)) for i, h in inst]
    q_bc = [jnp.broadcast_to(kq_t[:, (H_C + h) * nb + i:(H_C + h) * nb + i + 1], (DK, DV)) for i, h in inst]
    k_s = [jnp.sum(s_ref[i, h] * kb, axis=0, keepdims=True) for (i, h), kb in zip(inst, k_bc)]
    q_s = [jnp.sum(s_ref[i, h] * qb, axis=0, keepdims=True) for (i, h), qb in zip(inst, q_bc)]
    e_g = [jnp.exp(gb[i:i + 1, h:h + 1]) for i, h in inst]
    v_new = [gb[i:i + 1, H_C + h:H_C + h + 1] * (v_ref[i:i + 1, cols(h)] - eg * ks)
             for (i, h), eg, ks in zip(inst, e_g, k_s)]
    qk = [jnp.sum(q[i:i + 1, cols(h)] * k[i:i + 1, cols(h)], axis=-1, keepdims=True) for i, h in inst]
    for n, (i, h) in enumerate(inst):
        snew_ref[i, h] = e_g[n] * s_ref[i, h] + k_bc[n] * v_new[n]
    for n, (i, h) in enumerate(inst):
        o = e_g[n] * q_s[n] + qk[n] * v_new[n]
        yc_ref[i:i + 1, cols(h)] = _gated_norm(o, ong_ref[...], gate_ref[i:i + 1, cols(h)])


def _delta_sample(q, k, v, gb, gate, ong, state, layer):
    n = q.shape[0]
    nb = 8
    row = lambda w: pl.BlockSpec((nb, w), lambda i: (i, 0))
    st_in = pl.BlockSpec((None, nb, H_C, DK, DV), lambda i: (layer, i, 0, 0, 0))
    st = pl.BlockSpec((nb, H_C, DK, DV), lambda i: (i, 0, 0, 0))
    return pl.pallas_call(
        functools.partial(_delta_sample_kernel, nb=nb),
        grid=(n // nb,),
        in_specs=[row(D_C), row(D_C), row(D_C), row(LANES), row(D_C), _const_spec(ong.shape), st_in],
        out_specs=(row(D_C), st),
        out_shape=(jax.ShapeDtypeStruct((n, D_C), F32), jax.ShapeDtypeStruct(state.shape[1:], F32)),
        compiler_params=pltpu.CompilerParams(dimension_semantics=("arbitrary",), vmem_limit_bytes=VMEM_LIMIT),
        name="delta_sample",
    )(q, k, v, gb, gate, ong, state)


def _outproj_router_kernel(x_ref, yab_ref, yc_ref, xs_ref, yabs_ref, ycs_ref, woa_ref, woc_ref, g2_ref, wrt_ref,
                           x1_ref, h2_ref, route_ref, routet_ref, cnt_ref, *, n_prompt_tiles):
    is_prompt = pl.program_id(0) < n_prompt_tiles
    tm = x_ref.shape[0]

    def pick(prompt_ref, sample_ref):
        sample = sample_ref[...]
        sample = jnp.concatenate([sample, jnp.zeros((tm - sample.shape[0], sample.shape[1]), F32)], axis=0)
        return jnp.where(is_prompt, prompt_ref[...], sample)

    x1 = (pick(x_ref, xs_ref)
          + jnp.dot(pick(yab_ref, yabs_ref).astype(BF16), woa_ref[...], preferred_element_type=F32)
          + jnp.dot(pick(yc_ref, ycs_ref).astype(BF16), woc_ref[...], preferred_element_type=F32))
    x1_ref[...] = x1
    h2 = _rms(x1, g2_ref[...])
    h2_ref[...] = h2.astype(BF16)
    logits = _dot_3pass(h2, h2.astype(BF16), wrt_ref[...])
    lane = lax.broadcasted_iota(jnp.int32, logits.shape, 1)
    neg = -jnp.inf
    gl = jnp.where(lane < N_GROUPS, logits, neg)
    g_max = jnp.max(gl, axis=-1, keepdims=True)
    g_idx = jnp.min(jnp.where(gl == g_max, lane, LANES), axis=-1, keepdims=True)
    g_p = 1.0 / jnp.sum(jnp.exp(gl - g_max), axis=-1, keepdims=True)
    lo = EXPERT_LANE0 + g_idx * EXP_PER_GROUP
    in_group = (lane >= lo) & (lane < lo + EXP_PER_GROUP)
    el = jnp.where(in_group, logits, neg)
    e_max = jnp.max(el, axis=-1, keepdims=True)
    pe = jnp.exp(el - e_max)
    e_prob = pe / jnp.sum(pe, axis=-1, keepdims=True)
    p1 = jnp.max(e_prob, axis=-1, keepdims=True)
    i1 = jnp.min(jnp.where(e_prob == p1, lane, LANES), axis=-1, keepdims=True)
    rest = jnp.where(in_group & (lane != i1), e_prob, -1.0)
    p2 = jnp.max(rest, axis=-1, keepdims=True)
    i2 = jnp.min(jnp.where(rest == p2, lane, LANES), axis=-1, keepdims=True)
    denom = p1 + p2
    w1 = g_p * (p1 / denom)
    w2 = g_p * (p2 / denom)
    e1 = i1 - EXPERT_LANE0
    e2 = i2 - EXPERT_LANE0
    oh1 = lane == e1
    oh2 = lane == e2
    picked = jnp.where(oh1 | oh2, 1.0, 0.0)
    cnt = jnp.broadcast_to(jnp.sum(picked, axis=0, keepdims=True), (8, LANES))
    cnt_ref[0] = cnt
    ki = lax.broadcasted_iota(jnp.int32, (LANES, LANES), 0)
    kj = lax.broadcasted_iota(jnp.int32, (LANES, LANES), 1)
    pieces = jnp.ceil(cnt * (1.0 / PIECE)).astype(BF16)
    run_start = PIECE * jnp.dot(pieces, jnp.where(ki < kj, 1.0, 0.0).astype(BF16),
                                preferred_element_type=F32)[0:1, :]
    ri = lax.broadcasted_iota(jnp.int32, (tm, tm), 0)
    ci = lax.broadcasted_iota(jnp.int32, (tm, tm), 1)
    base = jnp.dot(jnp.where(ri > ci, 1.0, 0.0).astype(BF16), picked.astype(BF16),
                   preferred_element_type=F32) + run_start
    pos1 = jnp.sum(jnp.where(oh1, base, 0.0), axis=-1, keepdims=True)
    pos2 = jnp.sum(jnp.where(oh2, base, 0.0), axis=-1, keepdims=True)
    route = jnp.where(lane == 0, e1.astype(F32), jnp.where(lane == 1, e2.astype(F32), jnp.where(
        lane == 2, w1, jnp.where(lane == 3, w2, jnp.where(lane == 4, pos1, jnp.where(lane == 5, pos2, 0.0))))))
    route_ref[...] = route
    routet_ref[0] = route.T[0:8, :]


def _outproj_router(prompt, sample, woa, woc, g2, wrt):
    x2d = prompt[0]
    n, d = x2d.shape
    tm = min(TOKEN_TILE, n)
    n_prompt_tiles = n // tm
    n_all = n + tm
    assert sample[0].shape[0] <= tm and sample[0].shape[0] % 8 == 0
    p_row = lambda w: pl.BlockSpec((tm, w), lambda i: (jnp.minimum(i, n_prompt_tiles - 1), 0))
    row = lambda w: pl.BlockSpec((tm, w), lambda i: (i, 0))
    return pl.pallas_call(
        functools.partial(_outproj_router_kernel, n_prompt_tiles=n_prompt_tiles),
        grid=(n_prompt_tiles + 1,),
        in_specs=[p_row(d), p_row(D_A + D_B), p_row(D_C)] + [_const_spec(a.shape) for a in sample]
                 + [_const_spec(woa.shape), _const_spec(woc.shape), _const_spec(g2.shape), _const_spec(wrt.shape)],
        out_specs=(row(d), row(d), row(LANES), pl.BlockSpec((1, 8, tm), lambda i: (i, 0, 0)),
                   pl.BlockSpec((1, 8, LANES), lambda i: (i, 0, 0))),
        out_shape=(jax.ShapeDtypeStruct((n_all, d), F32), jax.ShapeDtypeStruct((n_all, d), BF16),
                   jax.ShapeDtypeStruct((n_all, LANES), F32),
                   jax.ShapeDtypeStruct((n_all // tm, 8, tm), F32),
                   jax.ShapeDtypeStruct((n_all // tm, 8, LANES), F32)),
        compiler_params=pltpu.CompilerParams(dimension_semantics=("arbitrary",), vmem_limit_bytes=VMEM_LIMIT),
        name="outproj_router",
    )(*prompt, *sample, woa, woc, g2, wrt)


def _route_plan(cnt, n_tok, tt):
    n_tiles = n_tok // tt
    cnt = cnt[:, 0, :N_EXPERTS].astype(jnp.int32)
    pc = (cnt + PIECE - 1) // PIECE * PIECE
    local_end = jnp.cumsum(pc, axis=1)
    local_off = local_end - pc
    seg_len = jnp.sum(pc, axis=0)
    seg_pad = (seg_len + ROW_TILE - 1) // ROW_TILE * ROW_TILE
    seg_end = jnp.cumsum(seg_pad)
    seg_start = seg_end - seg_pad
    run_off = seg_start[None, :] + jnp.cumsum(pc, axis=0) - pc
    piece_row = jnp.arange(_local_rows(tt) // PIECE, dtype=jnp.int32) * PIECE
    in_run = ((local_off[:, None, :] <= piece_row[None, :, None])
              & (piece_row[None, :, None] < local_end[:, None, :]))
    piece_dst = jnp.sum(jnp.where(in_run, (run_off - local_off)[:, None, :], 0), axis=-1) + piece_row[None, :]
    n_row_tiles = _max_sorted_rows(n_tok, tt) // ROW_TILE
    tile_row0 = jnp.arange(n_row_tiles, dtype=jnp.int32) * ROW_TILE
    tile_expert = jnp.minimum(jnp.sum((seg_end[None, :] <= tile_row0[:, None]).astype(jnp.int32), axis=1),
                              N_EXPERTS - 1)
    used = seg_pad > 0
    order = jnp.cumsum(used.astype(jnp.int32)) - used.astype(jnp.int32)
    ids = jnp.arange(N_EXPERTS, dtype=jnp.int32)
    later = jnp.where(used[None, :] & (ids[None, :] > ids[:, None]), ids[None, :], N_EXPERTS)
    next_used = jnp.min(later, axis=1)
    return dict(
        piece_dst=piece_dst.reshape(-1), tile_pieces=jnp.sum(pc, axis=1) // PIECE,
        fill_off=seg_start + seg_len, fill_pieces=(seg_pad - seg_len) // PIECE,
        tile_expert=tile_expert, n_active=(seg_end[-1] // ROW_TILE).reshape(1),
        expert_order=order, next_used=next_used,
    )


def _max_sorted_rows(n_tok, tt):
    n_tiles = n_tok // tt
    rows = 2 * n_tok + n_tiles * N_EXPERTS * (PIECE - 1) + N_EXPERTS * (ROW_TILE - 1)
    return (rows + ROW_TILE - 1) // ROW_TILE * ROW_TILE


def _local_rows(tt):
    return 2 * tt + N_EXPERTS * PIECE


def _piece_copy(src, dst, sem):
    return pltpu.make_async_copy(src, dst, sem)


def _for_each_piece(n, body):
    def four(q, c):
        for u in range(4):
            body(q * 4 + u)
        return c
    lax.fori_loop(0, lax.shift_right_logical(n, 2), four, 0)

    def one(p, c):
        body(p)
        return c
    lax.fori_loop(n & ~3, n, one, 0)


def _wait_pieces(n, copy_of_rows):
    for bit in (64, 32, 16, 8, 4, 2, 1):
        @pl.when((n & bit) != 0)
        def _(bit=bit):
            copy_of_rows(bit * PIECE).wait()


def _dispatch_kernel(piece_dst, tile_pieces, fill_off, fill_pieces, n_active,
                     h2_ref, routet_ref, xs_hbm, lbuf, zbuf, sem, *, tt, n_tiles, n_row_tiles):
    i = pl.program_id(0)
    slot = i % 2
    lrows = _local_rows(tt)
    max_pieces = lrows // PIECE

    assert max_pieces < 128

    def wait_tile(tile, s):
        _wait_pieces(tile_pieces[tile],
                     lambda rows: _piece_copy(lbuf.at[s, pl.ds(0, rows)], xs_hbm.at[pl.ds(0, rows)], sem.at[s]))

    @pl.when(i >= 2)
    def _():
        wait_tile(i - 2, slot)

    pos = routet_ref[0]
    h2 = h2_ref[...]
    for r0 in range(0, lrows, SORT_CHUNK):
        row = (lax.broadcasted_iota(jnp.int32, (SORT_CHUNK, tt), 0) + r0).astype(F32)
        perm = jnp.where((row == pos[4:5, :]) | (row == pos[5:6, :]), 1.0, 0.0).astype(BF16)
        lbuf[slot, r0:r0 + SORT_CHUNK] = jnp.dot(perm, h2, preferred_element_type=F32).astype(BF16)

    def send(p):
        dst = piece_dst[i * max_pieces + p]
        _piece_copy(lbuf.at[slot, pl.ds(pl.multiple_of(p * PIECE, PIECE), PIECE)],
                    xs_hbm.at[pl.ds(pl.multiple_of(dst, PIECE), PIECE)], sem.at[slot]).start()
    _for_each_piece(tile_pieces[i], send)

    @pl.when(i == n_tiles - 1)
    def _():
        zbuf[...] = jnp.zeros(zbuf.shape, BF16)
        zpiece = zbuf.at[pl.ds(0, PIECE)]
        for e in range(N_EXPERTS):
            def body(p, c, e=e):
                _piece_copy(zpiece, xs_hbm.at[pl.ds(pl.multiple_of(fill_off[e] + p * PIECE, PIECE), PIECE)],
                            sem.at[2]).start()
                return c
            lax.fori_loop(0, fill_pieces[e], body, 0)

        def tail_body(j, c):
            _piece_copy(zbuf, xs_hbm.at[pl.ds(pl.multiple_of(j * ROW_TILE, ROW_TILE), ROW_TILE)], sem.at[3]).start()
            return c
        lax.fori_loop(n_active[0], n_row_tiles, tail_body, 0)
        for e in range(N_EXPERTS):
            def body(p, c):
                _piece_copy(zpiece, xs_hbm.at[pl.ds(0, PIECE)], sem.at[2]).wait()
                return c
            lax.fori_loop(0, fill_pieces[e], body, 0)

        def tail_wait(j, c):
            _piece_copy(zbuf, xs_hbm.at[pl.ds(0, ROW_TILE)], sem.at[3]).wait()
            return c
        lax.fori_loop(n_active[0], n_row_tiles, tail_wait, 0)
        if n_tiles > 1:
            wait_tile(i - 1, 1 - slot)
        wait_tile(i, slot)


def _dispatch(h2, routet, plan, tt):
    n, d = h2.shape
    n_tiles = n // tt
    n_rows = _max_sorted_rows(n, tt)
    kernel_fn = functools.partial(_dispatch_kernel, tt=tt, n_tiles=n_tiles, n_row_tiles=n_rows // ROW_TILE)
    grid_spec = pltpu.PrefetchScalarGridSpec(
        num_scalar_prefetch=5,
        grid=(n_tiles,),
        in_specs=[pl.BlockSpec((tt, d), lambda i, *_: (i, 0)),
                  pl.BlockSpec((1, 8, tt), lambda i, *_: (i, 0, 0))],
        out_specs=pl.BlockSpec(memory_space=pl.ANY),
        scratch_shapes=[pltpu.VMEM((2, _local_rows(tt), d), BF16), pltpu.VMEM((ROW_TILE, d), BF16),
                        pltpu.SemaphoreType.DMA((4,))],
    )
    return pl.pallas_call(
        kernel_fn,
        grid_spec=grid_spec,
        out_shape=jax.ShapeDtypeStruct((n_rows, d), BF16),
        compiler_params=pltpu.CompilerParams(dimension_semantics=("arbitrary",), vmem_limit_bytes=VMEM_LIMIT),
        name="moe_dispatch",
    )(plan["piece_dst"], plan["tile_pieces"], plan["fill_off"], plan["fill_pieces"], plan["n_active"],
      h2, routet)


def _experts_kernel(tile_expert, n_active, expert_order, next_used,
                    xs_ref, wg_hbm, wu_hbm, wd_hbm, ys_ref, wbuf_g, wbuf_u, wbuf_d, wg_b, wu_b, wd_b, sem,
                    *, layer):
    j = pl.program_id(0)
    expert = tile_expert[j]
    previous = tile_expert[jnp.maximum(j - 1, 0)]
    active = j < n_active[0]
    half = expert_order[expert] % 2

    def weight_copies(e, s):
        return (_piece_copy(wg_hbm.at[layer, e], wbuf_g.at[s], sem.at[s, 0]),
                _piece_copy(wu_hbm.at[layer, e], wbuf_u.at[s], sem.at[s, 1]),
                _piece_copy(wd_hbm.at[layer, e], wbuf_d.at[s], sem.at[s, 2]))

    @pl.when(j == 0)
    def _():
        for c in weight_copies(expert, half):
            c.start()

    @pl.when(active & ((j == 0) | (expert != previous)))
    def _():
        for c in weight_copies(expert, half):
            c.wait()
        wg_b[...] = wbuf_g[half].astype(BF16)
        wu_b[...] = wbuf_u[half].astype(BF16)
        wd_b[...] = wbuf_d[half].astype(BF16)
        upcoming = next_used[expert]

        @pl.when(upcoming < N_EXPERTS)
        def _():
            for c in weight_copies(upcoming, 1 - half):
                c.start()

    @pl.when(active)
    def _():
        x = xs_ref[...]
        gate = jnp.dot(x, wg_b[...], preferred_element_type=F32)
        up = jnp.dot(x, wu_b[...], preferred_element_type=F32)
        act = (_silu(gate) * up).astype(BF16)
        ys_ref[...] = jnp.dot(act, wd_b[...], preferred_element_type=F32).astype(BF16)

    @pl.when(jnp.logical_not(active))
    def _():
        ys_ref[...] = jnp.zeros(ys_ref.shape, BF16)


def _experts(xs, wg, wu, wd, layer, plan):
    rows, d = xs.shape
    d_e = wg.shape[-1]
    live = lambda j, te, na, *_: (jnp.minimum(j, na[0] - 1), 0)
    grid_spec = pltpu.PrefetchScalarGridSpec(
        num_scalar_prefetch=4,
        grid=(rows // ROW_TILE,),
        in_specs=[pl.BlockSpec((ROW_TILE, d), live),
                  pl.BlockSpec(memory_space=pl.ANY), pl.BlockSpec(memory_space=pl.ANY),
                  pl.BlockSpec(memory_space=pl.ANY)],
        out_specs=pl.BlockSpec((ROW_TILE, d), lambda j, *_: (j, 0)),
        scratch_shapes=[pltpu.VMEM((2, d, d_e), F32), pltpu.VMEM((2, d, d_e), F32), pltpu.VMEM((2, d_e, d), F32),
                        pltpu.VMEM((d, d_e), BF16), pltpu.VMEM((d, d_e), BF16), pltpu.VMEM((d_e, d), BF16),
                        pltpu.SemaphoreType.DMA((2, 3))],
    )
    return pl.pallas_call(
        functools.partial(_experts_kernel, layer=layer),
        grid_spec=grid_spec,
        out_shape=jax.ShapeDtypeStruct((rows, d), BF16),
        compiler_params=pltpu.CompilerParams(dimension_semantics=("arbitrary",), vmem_limit_bytes=VMEM_LIMIT),
        name="moe_experts",
    )(plan["tile_expert"], plan["n_active"], plan["expert_order"], plan["next_used"], xs, wg, wu, wd)


def _combine_kernel(piece_src, tile_pieces, x1_ref, route_ref, fg_ref, ys_hbm, out_ref, outs_ref, lbuf, sem,
                    *, tt, n_tiles, final_norm):
    i = pl.program_id(0)
    slot = i % 2
    lrows = _local_rows(tt)
    max_pieces = lrows // PIECE

    assert max_pieces < 128

    def fetch_tile(tile, s):
        def fetch(p):
            src = piece_src[tile * max_pieces + p]
            _piece_copy(ys_hbm.at[pl.ds(pl.multiple_of(src, PIECE), PIECE)],
                        lbuf.at[s, pl.ds(pl.multiple_of(p * PIECE, PIECE), PIECE)], sem.at[s]).start()
        _for_each_piece(tile_pieces[tile], fetch)

    @pl.when(i == 0)
    def _():
        lbuf[...] = jnp.zeros(lbuf.shape, BF16)
        fetch_tile(0, 0)

    @pl.when(i + 1 < n_tiles)
    def _():
        fetch_tile(i + 1, 1 - slot)

    _wait_pieces(tile_pieces[i], lambda rows: _piece_copy(ys_hbm.at[pl.ds(0, rows)],
                                                         lbuf.at[slot, pl.ds(0, rows)], sem.at[slot]))

    route = route_ref[...]
    out = x1_ref[...]
    for r0 in range(0, lrows, SORT_CHUNK):
        col = (lax.broadcasted_iota(jnp.int32, (tt, SORT_CHUNK), 1) + r0).astype(F32)
        weights = (jnp.where(col == route[:, 4:5], route[:, 2:3], 0.0)
                   + jnp.where(col == route[:, 5:6], route[:, 3:4], 0.0)).astype(BF16)
        out = out + jnp.dot(weights, lbuf[slot, r0:r0 + SORT_CHUNK], preferred_element_type=F32)
    out = _rms(out, fg_ref[...]) if final_norm else out

    @pl.when(i < n_tiles - 1)
    def _():
        out_ref[...] = out

    @pl.when(i == n_tiles - 1)
    def _():
        outs_ref[...] = out[0:outs_ref.shape[0]]


def _combine(x1, route, ys, fg, plan, tt, n_sample, final_norm):
    n, d = x1.shape
    n_tiles = n // tt
    kernel_fn = functools.partial(_combine_kernel, tt=tt, n_tiles=n_tiles, final_norm=final_norm)
    grid_spec = pltpu.PrefetchScalarGridSpec(
        num_scalar_prefetch=2,
        grid=(n_tiles,),
        in_specs=[pl.BlockSpec((tt, d), lambda i, *_: (i, 0)),
                  pl.BlockSpec((tt, LANES), lambda i, *_: (i, 0)),
                  pl.BlockSpec(fg.shape, lambda i, *_: (0, 0)),
                  pl.BlockSpec(memory_space=pl.ANY)],
        out_specs=(pl.BlockSpec((tt, d), lambda i, *_: (jnp.minimum(i, n_tiles - 2), 0)),
                   pl.BlockSpec((n_sample, d), lambda i, *_: (0, 0))),
        scratch_shapes=[pltpu.VMEM((2, _local_rows(tt), d), BF16), pltpu.SemaphoreType.DMA((2,))],
    )
    return pl.pallas_call(
        kernel_fn,
        grid_spec=grid_spec,
        out_shape=(jax.ShapeDtypeStruct((n - tt, d), F32), jax.ShapeDtypeStruct((n_sample, d), F32)),
        compiler_params=pltpu.CompilerParams(dimension_semantics=("arbitrary",), vmem_limit_bytes=VMEM_LIMIT),
        name="moe_combine",
    )(plan["piece_dst"], plan["tile_pieces"], x1, route, fg, ys)


def _routed_moe(x1, h2, route, routet, cnt, wg, wu, wd, layer, fg, n_sample, final_norm):
    n = x1.shape[0]
    tt = n // cnt.shape[0]
    plan = _route_plan(cnt, n, tt)
    xs = _dispatch(h2, routet, plan, tt)
    ys = _experts(xs, wg, wu, wd, layer, plan)
    return _combine(x1, route, ys, fg, plan, tt, n_sample, final_norm)


def _pad_lanes(a, lane0=0):
    a = a.reshape((1, -1)) if a.ndim == 1 else a
    return jnp.pad(a, ((0, 0), (lane0, LANES - lane0 - a.shape[1])))


def kernel(x_prompt, x_sample, state_conv_a, state_conv_b, state_conv_qkv, state_delta, norm1_g, w_in, conv_a_w,
           conv_b_w, ln_b_g, ln_b_b, conv_c_w, a_log, dt_bias, o_norm_g, w_out, norm2_g, w_group, w_router,
           w_gate, w_up, w_down, final_g):
    n_b, t_len, d = x_prompt.shape
    n_s = x_sample.shape[0]
    depth = w_in.shape[0]
    n_main = 3 * D_A + 2 * D_B + 3 * D_C + D_C
    xp = x_prompt.reshape(n_b * t_len, d)
    xs = x_sample.reshape(n_s, d)
    fg = final_g.reshape(1, d)
    outs = {k: [] for k in ("pa", "pb", "pq", "pd", "sa", "sb", "sq", "sd")}
    for l in range(depth):
        g1 = norm1_g[l].reshape(1, d)
        wmain = w_in[l][:, :n_main].astype(BF16)
        wsmall = _pad_lanes(jnp.concatenate([w_in[l][:, n_main + H_C:], w_in[l][:, n_main:n_main + H_C]], axis=1))
        caw, cbw, ccw = conv_a_w[l], conv_b_w[l], conv_c_w[l]
        lng, lnb = ln_b_g[l].reshape(1, D_B), ln_b_b[l].reshape(1, D_B)
        alog, dtb = _pad_lanes(a_log[l]), _pad_lanes(dt_bias[l])
        ong = o_norm_g[l].reshape(1, DV)
        woa = w_out[l][:D_A + D_B].astype(BF16)
        woc = w_out[l][D_A + D_B:].astype(BF16)
        g2 = norm2_g[l].reshape(1, d)
        wrt = _pad_lanes(jnp.concatenate(
            [w_group[l], w_router[l].transpose(1, 0, 2).reshape(d, N_EXPERTS)], axis=1))
        last = l == depth - 1

        yab, q, k, v, gate, gb, st_a, st_b, st_q = _inproj_prompt(
            xp, n_b, t_len, g1, wmain, wsmall, caw, cbw, lng, lnb, ccw, alog, dtb)
        yc, s_fin = _delta_prompt(q, k, v, gb, gate, ong, n_b, t_len)
        outs["pa"].append(st_a); outs["pb"].append(st_b); outs["pq"].append(st_q); outs["pd"].append(s_fin)

        hist_a = state_conv_a[l].transpose(1, 0, 2)
        hist_b = state_conv_b[l].transpose(1, 0, 2)
        hist_q = state_conv_qkv[l].transpose(1, 0, 2)
        yab_s, q, k, v, gate, gb, un_a, un_b, un_q = _inproj_sample(
            xs, g1, wmain, wsmall, caw, cbw, lng, lnb, ccw, alog, dtb, hist_a, hist_b, hist_q)
        yc_s, s_new = _delta_sample(q, k, v, gb, gate, ong, state_delta, l)

        x1, h2, route, routet, cnt = _outproj_router((xp, yab, yc), (xs, yab_s, yc_s), woa, woc, g2, wrt)
        xp, xs = _routed_moe(x1, h2, route, routet, cnt, w_gate, w_up, w_down, l, fg, n_s, last)
        outs["sa"].append(jnp.concatenate([state_conv_a[l][:, 1:], un_a[:, None]], axis=1))
        outs["sb"].append(jnp.concatenate([state_conv_b[l][:, 1:], un_b[:, None]], axis=1))
        outs["sq"].append(jnp.concatenate([state_conv_qkv[l][:, 1:], un_q[:, None]], axis=1))
        outs["sd"].append(s_new)

    stack = lambda key: jnp.stack(outs[key])
    return (xp.reshape(n_b, t_len, d), xs.reshape(n_s, 1, d),
            stack("pa"), stack("pb"), stack("pq"), stack("pd"),
            stack("sa"), stack("sb"), stack("sq"), stack("sd"))
```

```python
import functools

import jax
import jax.numpy as jnp
from jax import lax
from jax.experimental import pallas as pl
from jax.experimental.pallas import tpu as pltpu

F32 = jnp.float32
BF16 = jnp.bfloat16

EPS = 1e-6
H_C = 4
DK = 128
DV = 128
D_A = 256
D_B = 256
D_C = 512
W_A = 3
W_B = 31
W_C = 4
CHUNK = 64
N_GROUPS = 4
EXP_PER_GROUP = 8
N_EXPERTS = N_GROUPS * EXP_PER_GROUP
LANES = 128
PAD_A = 8
PAD_B = 32
PAD_C = 8
EXPERT_LANE0 = N_GROUPS
ROUTER_ROWS = (N_GROUPS + N_EXPERTS + 7) // 8 * 8
VMEM_LIMIT = 56 * 1024 * 1024
TOKEN_TILE = 512
PIECE = 16
ROW_TILE = 512
SORT_CHUNK = 256


def _silu(x):
    return x * (1.0 / (1.0 + jnp.exp(-x)))


def _sigmoid(x):
    return 1.0 / (1.0 + jnp.exp(-x))


def _softplus(x):
    return jnp.maximum(x, 0.0) + jnp.log1p(jnp.exp(-jnp.abs(x)))


def _rms(x, g):
    return x * lax.rsqrt(jnp.mean(x * x, axis=-1, keepdims=True) + EPS) * g


def _conv_taps(buf, w_ref, width, pad, n_rows, n_cols, row_blk, col_blk):
    row_parts = []
    for r0 in range(0, n_rows, row_blk):
        col_parts = []
        for c0 in range(0, n_cols, col_blk):
            acc = None
            for j in range(width):
                start = pad - (width - 1) + j + r0
                term = buf[start:start + row_blk, c0:c0 + col_blk] * w_ref[j:j + 1, c0:c0 + col_blk]
                acc = term if acc is None else acc + term
            col_parts.append(acc)
        row_parts.append(col_parts[0] if len(col_parts) == 1 else jnp.concatenate(col_parts, axis=1))
    return row_parts[0] if len(row_parts) == 1 else jnp.concatenate(row_parts, axis=0)


def _conv_taps_realigned(buf, shifted, w_ref, width, pad, n_rows, row_blk):
    first = pad - (width - 1)
    n_shift = n_rows + (first + width - 1) // 8 * 8 - 8
    for r in range(1, 8):
        shifted[r - 1] = buf[r:r + n_shift, :]
    row_parts = []
    for r0 in range(0, n_rows, row_blk):
        acc = None
        for j in range(width):
            q, r = divmod(first + j, 8)
            lo = 8 * q + r0
            window = buf[lo:lo + row_blk, :] if r == 0 else shifted[r - 1, lo:lo + row_blk, :]
            term = window * w_ref[j:j + 1, :]
            acc = term if acc is None else acc + term
        row_parts.append(acc)
    return row_parts[0] if len(row_parts) == 1 else jnp.concatenate(row_parts, axis=0)


def _l2norm_heads(x, scale):
    parts = []
    for h in range(H_C):
        xh = x[:, h * DK:(h + 1) * DK]
        parts.append(xh * (lax.rsqrt(jnp.sum(xh * xh, axis=-1, keepdims=True) + 1e-6) * scale))
    return jnp.concatenate(parts, axis=1)


def _dot_3pass(a, a_hi, w):
    a_lo = (a - a_hi.astype(F32)).astype(BF16)
    w_hi = w.astype(BF16)
    w_lo = (w - w_hi.astype(F32)).astype(BF16)
    n = w.shape[1]
    both = jnp.dot(a_hi, jnp.concatenate([w_hi, w_lo], axis=1), preferred_element_type=F32)
    return both[:, :n] + both[:, n:] + jnp.dot(a_lo, w_hi, preferred_element_type=F32)


def _in_projection(x, g1_ref, wmain_ref, wsmall_ref):
    h = _rms(x, g1_ref[...])
    hb = h.astype(BF16)
    z_a = jnp.dot(hb, wmain_ref[:, 0:768], preferred_element_type=F32)
    z_b = jnp.dot(hb, wmain_ref[:, 768:1280], preferred_element_type=F32)
    z_qkv = jnp.dot(hb, wmain_ref[:, 1280:2816], preferred_element_type=F32)
    z_gate = jnp.dot(hb, wmain_ref[:, 2816:3328], preferred_element_type=F32)
    z_s = _dot_3pass(h, hb, wsmall_ref[...])
    return z_a, z_b, z_qkv, z_gate, z_s


def _layer_norm_silu(x, g, b):
    mu = jnp.mean(x, axis=-1, keepdims=True)
    xc = x - mu
    y = xc * lax.rsqrt(jnp.mean(xc * xc, axis=-1, keepdims=True) + EPS)
    return _silu(y * g + b)


def _decay_and_beta(z_s, alog_ref, dtb_ref):
    lane = lax.broadcasted_iota(jnp.int32, z_s.shape, 1)
    g = -jnp.exp(alog_ref[...]) * _softplus(z_s + dtb_ref[...])
    g = jnp.where(lane < H_C, g, 0.0)
    beta = jnp.where((lane >= H_C) & (lane < 2 * H_C), _sigmoid(z_s), 0.0)
    return g, beta


def _inproj_prompt_kernel(x_ref, g1_ref, wmain_ref, wsmall_ref, caw_ref, cbw_ref, lng_ref, lnb_ref,
                          ccw_ref, alog_ref, dtb_ref,
                          yab_ref, q_ref, k_ref, v_ref, gate_ref, gb_ref, sta_ref, stb_ref, stq_ref,
                          abuf, bbuf, cbuf, bshift, *, tt, nt):
    t = pl.program_id(1)

    @pl.when(t == 0)
    def _():
        abuf[0:PAD_A, :] = jnp.zeros((PAD_A, D_A), F32)
        bbuf[0:PAD_B, :] = jnp.zeros((PAD_B, D_B), F32)
        cbuf[0:PAD_C, :] = jnp.zeros((PAD_C, 3 * D_C), F32)

    h = _rms(x_ref[...], g1_ref[...])
    hb = h.astype(BF16)
    project = lambda lo, hi: jnp.dot(hb, wmain_ref[:, lo:hi], preferred_element_type=F32)

    z_a = project(0, 3 * D_A)
    abuf[PAD_A:PAD_A + tt, :] = z_a[:, 256:512] * z_a[:, 512:768]
    conv_a = _conv_taps(abuf, caw_ref, W_A, PAD_A, tt, D_A, 64, 256)
    yab_ref[:, 0:D_A] = z_a[:, 0:256] * conv_a

    z_b = project(3 * D_A, 3 * D_A + 2 * D_B)
    bbuf[PAD_B:PAD_B + tt, :] = z_b[:, 0:256] * _sigmoid(z_b[:, 256:512])
    conv_b = _conv_taps_realigned(bbuf, bshift, cbw_ref, W_B, PAD_B, tt, 64)
    yab_ref[:, D_A:D_A + D_B] = _layer_norm_silu(conv_b, lng_ref[...], lnb_ref[...])

    qkv0 = 3 * D_A + 2 * D_B
    for part, (out_ref, scale) in enumerate(((q_ref, DK ** -0.5), (k_ref, 1.0), (v_ref, None))):
        c0 = part * D_C
        cbuf[PAD_C:PAD_C + tt, c0:c0 + D_C] = project(qkv0 + c0, qkv0 + c0 + D_C)
        conv = _silu(_conv_taps(cbuf.at[:, c0:c0 + D_C], ccw_ref.at[:, c0:c0 + D_C], W_C, PAD_C, tt, D_C, 64, 512))
        out_ref[...] = conv if scale is None else _l2norm_heads(conv, scale)
    gate_ref[...] = project(qkv0 + 3 * D_C, qkv0 + 4 * D_C)

    z_s = _dot_3pass(h, hb, wsmall_ref[...])
    g, beta = _decay_and_beta(z_s, alog_ref, dtb_ref)
    row_in_chunk = lax.broadcasted_iota(jnp.int32, g.shape, 0) & (CHUNK - 1)
    shift = 1
    while shift < CHUNK:
        g = g + jnp.where(row_in_chunk >= shift, pltpu.roll(g, shift, axis=0), 0.0)
        shift *= 2
    gb_ref[...] = g + beta

    @pl.when(t == nt - 1)
    def _():
        sta_ref[0] = abuf[PAD_A + tt - (W_A - 1):PAD_A + tt, :]
        stb_ref[0] = bbuf[PAD_B + tt - (W_B - 1):PAD_B + tt, :]
        stq_ref[0] = cbuf[PAD_C + tt - (W_C - 1):PAD_C + tt, :]

    abuf[0:PAD_A, :] = abuf[tt:tt + PAD_A, :]
    bbuf[0:PAD_B, :] = bbuf[tt:tt + PAD_B, :]
    cbuf[0:PAD_C, :] = cbuf[tt:tt + PAD_C, :]


def _const_spec(shape):
    nd = len(shape)
    return pl.BlockSpec(shape, lambda *_: (0,) * nd)


def _inproj_prompt(x2d, n_b, t_len, g1, wmain, wsmall, caw, cbw, lng, lnb, ccw, alog, dtb):
    tt = min(512, t_len)
    nt = t_len // tt
    n_tok = n_b * t_len
    d = x2d.shape[1]
    row = lambda w: pl.BlockSpec((tt, w), lambda b, t: (b * nt + t, 0))
    st = lambda r, w: pl.BlockSpec((1, r, w), lambda b, t: (b, 0, 0))
    out_shape = (
        jax.ShapeDtypeStruct((n_tok, D_A + D_B), F32),
        jax.ShapeDtypeStruct((n_tok, D_C), F32),
        jax.ShapeDtypeStruct((n_tok, D_C), F32),
        jax.ShapeDtypeStruct((n_tok, D_C), F32),
        jax.ShapeDtypeStruct((n_tok, D_C), F32),
        jax.ShapeDtypeStruct((n_tok, LANES), F32),
        jax.ShapeDtypeStruct((n_b, W_A - 1, D_A), F32),
        jax.ShapeDtypeStruct((n_b, W_B - 1, D_B), F32),
        jax.ShapeDtypeStruct((n_b, W_C - 1, 3 * D_C), F32),
    )
    return pl.pallas_call(
        functools.partial(_inproj_prompt_kernel, tt=tt, nt=nt),
        grid=(n_b, nt),
        in_specs=[row(d), _const_spec(g1.shape), _const_spec(wmain.shape), _const_spec(wsmall.shape),
                  _const_spec(caw.shape), _const_spec(cbw.shape), _const_spec(lng.shape), _const_spec(lnb.shape),
                  _const_spec(ccw.shape), _const_spec(alog.shape), _const_spec(dtb.shape)],
        out_specs=(row(D_A + D_B), row(D_C), row(D_C), row(D_C), row(D_C), row(LANES),
                   st(W_A - 1, D_A), st(W_B - 1, D_B), st(W_C - 1, 3 * D_C)),
        out_shape=out_shape,
        scratch_shapes=[pltpu.VMEM((PAD_A + tt, D_A), F32), pltpu.VMEM((PAD_B + tt, D_B), F32),
                        pltpu.VMEM((PAD_C + tt, 3 * D_C), F32), pltpu.VMEM((7, PAD_B + tt - 8, D_B), F32)],
        compiler_params=pltpu.CompilerParams(dimension_semantics=("arbitrary", "arbitrary"),
                                             vmem_limit_bytes=VMEM_LIMIT),
        name="inproj_prompt",
    )(x2d, g1, wmain, wsmall, caw, cbw, lng, lnb, ccw, alog, dtb)


def _dot_nt(a, b):
    return lax.dot_general(a, b, (((1,), (1,)), ((), ())), preferred_element_type=F32)


def _dot_tn(a, b):
    return lax.dot_general(a, b, (((0,), (0,)), ((), ())), preferred_element_type=F32)


def _bdot(a, b):
    return jnp.dot(a.astype(BF16), b.astype(BF16), preferred_element_type=F32)


def _gated_norm(o, on_g, gate):
    return o * lax.rsqrt(jnp.mean(o * o, axis=-1, keepdims=True) + EPS) * on_g * _silu(gate)


def _delta_prompt_kernel(q_ref, k_ref, v_ref, gb_ref, gate_ref, ong_ref, yc_ref, sfin_ref, s_scr, *, tq, nt):
    t = pl.program_id(1)

    @pl.when(t == 0)
    def _():
        s_scr[...] = jnp.zeros(s_scr.shape, F32)

    c = CHUNK
    n_chunks = tq // c
    n_double = c.bit_length() - 2
    inst = [(ch, h) for ch in range(n_chunks) for h in range(H_C)]
    rows = lambda ch: slice(ch * c, (ch + 1) * c)
    cols = lambda h: slice(h * DK, (h + 1) * DK)
    ri = lax.broadcasted_iota(jnp.int32, (c, c), 0)
    ci = lax.broadcasted_iota(jnp.int32, (c, c), 1)
    gb = gb_ref[...]

    q = [q_ref[rows(ch), cols(h)] for ch, h in inst]
    k = [k_ref[rows(ch), cols(h)] for ch, h in inst]
    v = [v_ref[rows(ch), cols(h)] for ch, h in inst]
    g_col = [gb[rows(ch), h:h + 1] for ch, h in inst]
    b_col = [gb[rows(ch), H_C + h:H_C + h + 1] for ch, h in inst]
    g_row = [jnp.sum(jnp.where(ri == ci, g, 0.0), axis=0, keepdims=True) for g in g_col]
    diff = [gc - gr for gc, gr in zip(g_col, g_row)]
    dec_strict = [jnp.exp(jnp.where(ri > ci, d, -jnp.inf)) for d in diff]
    dec_incl = [jnp.exp(jnp.where(ri >= ci, d, -jnp.inf)) for d in diff]
    e_g = [jnp.exp(g) for g in g_col]
    g_last = [g[c - 1:c, :] for g in g_col]

    qkk = [_dot_nt(jnp.concatenate([qi, ki], axis=0).astype(BF16), ki.astype(BF16)) for qi, ki in zip(q, k)]
    qk = [m[:c] * d for m, d in zip(qkk, dec_incl)]
    p = [-(b * m[c:] * d) for b, m, d in zip(b_col, qkk, dec_strict)]
    sol = [jnp.concatenate([vi * b, ki * (b * eg)], axis=1) for vi, ki, b, eg in zip(v, k, b_col, e_g)]
    for level in range(n_double + 1):
        sol = [s + _bdot(pi, s) for pi, s in zip(p, sol)]
        if level < n_double:
            p = [_bdot(pi, pi) for pi in p]
    kd = [ki * jnp.exp(gl - g) for ki, gl, g in zip(k, g_last, g_col)]
    qk_uw = [_bdot(m, s) for m, s in zip(qk, sol)]
    kd_uw = [_dot_tn(x.astype(BF16), s.astype(BF16)) for x, s in zip(kd, sol)]
    lhs = [jnp.concatenate([qi * eg - a[:, DV:], -b[:, DV:]], axis=0).astype(BF16)
           for qi, eg, a, b in zip(q, e_g, qk_uw, kd_uw)]
    decay = [jnp.exp(gl) for gl in g_last]

    state = [s_scr[h] for h in range(H_C)]
    for ch in range(n_chunks):
        base = ch * H_C
        r = [jnp.dot(lhs[base + h], state[h].astype(BF16), preferred_element_type=F32) for h in range(H_C)]
        for h in range(H_C):
            o = r[h][:c] + qk_uw[base + h][:, :DV]
            yc_ref[rows(ch), cols(h)] = _gated_norm(o, ong_ref[...], gate_ref[rows(ch), cols(h)])
        state = [decay[base + h] * state[h] + r[h][c:] + kd_uw[base + h][:, :DV] for h in range(H_C)]
    for h in range(H_C):
        s_scr[h] = state[h]

    @pl.when(t == nt - 1)
    def _():
        sfin_ref[0] = s_scr[...]


def _delta_prompt(q, k, v, gb, gate, ong, n_b, t_len):
    tq = min(512, t_len)
    nt = t_len // tq
    row = lambda w: pl.BlockSpec((tq, w), lambda b, t: (b * nt + t, 0))
    return pl.pallas_call(
        functools.partial(_delta_prompt_kernel, tq=tq, nt=nt),
        grid=(n_b, nt),
        in_specs=[row(D_C), row(D_C), row(D_C), row(LANES), row(D_C), _const_spec(ong.shape)],
        out_specs=(row(D_C), pl.BlockSpec((1, H_C, DK, DV), lambda b, t: (b, 0, 0, 0))),
        out_shape=(jax.ShapeDtypeStruct((n_b * t_len, D_C), F32),
                   jax.ShapeDtypeStruct((n_b, H_C, DK, DV), F32)),
        scratch_shapes=[pltpu.VMEM((H_C, DK, DV), F32)],
        compiler_params=pltpu.CompilerParams(dimension_semantics=("arbitrary", "arbitrary"),
                                             vmem_limit_bytes=VMEM_LIMIT),
        name="delta_prompt",
    )(q, k, v, gb, gate, ong)


def _inproj_sample_kernel(x_ref, g1_ref, wmain_ref, wsmall_ref, caw_ref, cbw_ref, lng_ref, lnb_ref,
                          ccw_ref, alog_ref, dtb_ref, sta_ref, stb_ref, stq_ref,
                          yab_ref, q_ref, k_ref, v_ref, gate_ref, gb_ref, una_ref, unb_ref, unq_ref):
    z_a, z_b, z_qkv, z_gate, z_s = _in_projection(x_ref[...], g1_ref, wmain_ref, wsmall_ref)

    def conv_step(state_ref, new, w_ref, width):
        acc = new * w_ref[width - 1:width, :]
        for j in range(width - 1):
            acc = acc + state_ref[j] * w_ref[j:j + 1, :]
        return acc

    u_a = z_a[:, 256:512] * z_a[:, 512:768]
    una_ref[...] = u_a
    yab_ref[:, 0:D_A] = z_a[:, 0:256] * conv_step(sta_ref, u_a, caw_ref, W_A)

    u_b = z_b[:, 0:256] * _sigmoid(z_b[:, 256:512])
    unb_ref[...] = u_b
    yab_ref[:, D_A:D_A + D_B] = _layer_norm_silu(conv_step(stb_ref, u_b, cbw_ref, W_B), lng_ref[...], lnb_ref[...])

    unq_ref[...] = z_qkv
    qkv = _silu(conv_step(stq_ref, z_qkv, ccw_ref, W_C))
    q_ref[...] = _l2norm_heads(qkv[:, 0:D_C], DK ** -0.5)
    k_ref[...] = _l2norm_heads(qkv[:, D_C:2 * D_C], 1.0)
    v_ref[...] = qkv[:, 2 * D_C:3 * D_C]
    gate_ref[...] = z_gate
    g, beta = _decay_and_beta(z_s, alog_ref, dtb_ref)
    gb_ref[...] = g + beta


def _inproj_sample(x2d, g1, wmain, wsmall, caw, cbw, lng, lnb, ccw, alog, dtb, st_a, st_b, st_q):
    n = x2d.shape[0]
    args = (x2d, g1, wmain, wsmall, caw, cbw, lng, lnb, ccw, alog, dtb, st_a, st_b, st_q)
    out_widths = (D_A + D_B, D_C, D_C, D_C, D_C, LANES, D_A, D_B, 3 * D_C)
    return pl.pallas_call(
        _inproj_sample_kernel,
        grid=(1,),
        in_specs=[_const_spec(a.shape) for a in args],
        out_specs=tuple(_const_spec((n, w)) for w in out_widths),
        out_shape=tuple(jax.ShapeDtypeStruct((n, w), F32) for w in out_widths),
        compiler_params=pltpu.CompilerParams(dimension_semantics=("arbitrary",), vmem_limit_bytes=VMEM_LIMIT),
        name="inproj_sample",
    )(*args)


def _delta_sample_kernel(q_ref, k_ref, v_ref, gb_ref, gate_ref, ong_ref, s_ref, yc_ref, snew_ref, *, nb):
    q = q_ref[...]
    k = k_ref[...]
    rows = [k[:, h * DK:(h + 1) * DK] for h in range(H_C)] + [q[:, h * DK:(h + 1) * DK] for h in range(H_C)]
    rows.append(jnp.zeros((LANES - 2 * H_C * nb, DK), F32))
    kq_t = jnp.concatenate(rows, axis=0).T
    gb = gb_ref[...]
    inst = [(i, h) for i in range(nb) for h in range(H_C)]
    cols = lambda h: slice(h * DK, (h + 1) * DK)
    k_bc = [jnp.broadcast_to(kq_t[:, h * nb + i:h * nb + i + 1], (DK, DV)) for i, h in inst]
    q_bc = [jnp.broadcast_to(kq_t[:, (H_C + h) * nb + i:(H_C + h) * nb + i + 1], (DK, DV)) for i, h in inst]
    k_s = [jnp.sum(s_ref[i, h] * kb, axis=0, keepdims=True) for (i, h), kb in zip(inst, k_bc)]
    q_s = [jnp.sum(s_ref[i, h] * qb, axis=0, keepdims=True) for (i, h), qb in zip(inst, q_bc)]
    e_g = [jnp.exp(gb[i:i + 1, h:h + 1]) for i, h in inst]
    v_new = [gb[i:i + 1, H_C + h:H_C + h + 1] * (v_ref[i:i + 1, cols(h)] - eg * ks)
             for (i, h), eg, ks in zip(inst, e_g, k_s)]
    qk = [jnp.sum(q[i:i + 1, cols(h)] * k[i:i + 1, cols(h)], axis=-1, keepdims=True) for i, h in inst]
    for n, (i, h) in enumerate(inst):
        snew_ref[i, h] = e_g[n] * s_ref[i, h] + k_bc[n] * v_new[n]
    for n, (i, h) in enumerate(inst):
        o = e_g[n] * q_s[n] + qk[n] * v_new[n]
        yc_ref[i:i + 1, cols(h)] = _gated_norm(o, ong_ref[...], gate_ref[i:i + 1, cols(h)])


def _delta_sample(q, k, v, gb, gate, ong, state, layer):
    n = q.shape[0]
    nb = 8
    row = lambda w: pl.BlockSpec((nb, w), lambda i: (i, 0))
    st_in = pl.BlockSpec((None, nb, H_C, DK, DV), lambda i: (layer, i, 0, 0, 0))
    st = pl.BlockSpec((nb, H_C, DK, DV), lambda i: (i, 0, 0, 0))
    return pl.pallas_call(
        functools.partial(_delta_sample_kernel, nb=nb),
        grid=(n // nb,),
        in_specs=[row(D_C), row(D_C), row(D_C), row(LANES), row(D_C), _const_spec(ong.shape), st_in],
        out_specs=(row(D_C), st),
        out_shape=(jax.ShapeDtypeStruct((n, D_C), F32), jax.ShapeDtypeStruct(state.shape[1:], F32)),
        compiler_params=pltpu.CompilerParams(dimension_semantics=("arbitrary",), vmem_limit_bytes=VMEM_LIMIT),
        name="delta_sample",
    )(q, k, v, gb, gate, ong, state)


def _outproj_router_kernel(x_ref, yab_ref, yc_ref, xs_ref, yabs_ref, ycs_ref, woa_ref, woc_ref, g2_ref, wrt_ref,
                           x1_ref, h2_ref, route_ref, routet_ref, cnt_ref, *, n_prompt_tiles):
    is_prompt = pl.program_id(0) < n_prompt_tiles
    tm = x_ref.shape[0]

    def pick(prompt_ref, sample_ref):
        sample = sample_ref[...]
        sample = jnp.concatenate([sample, jnp.zeros((tm - sample.shape[0], sample.shape[1]), F32)], axis=0)
        return jnp.where(is_prompt, prompt_ref[...], sample)

    x1 = (pick(x_ref, xs_ref)
          + jnp.dot(pick(yab_ref, yabs_ref).astype(BF16), woa_ref[...], preferred_element_type=F32)
          + jnp.dot(pick(yc_ref, ycs_ref).astype(BF16), woc_ref[...], preferred_element_type=F32))
    x1_ref[...] = x1
    h2 = _rms(x1, g2_ref[...])
    h2_ref[...] = h2.astype(BF16)
    logits = _dot_3pass(h2, h2.astype(BF16), wrt_ref[...]).T[0:ROUTER_ROWS, :]
    row = lax.broadcasted_iota(jnp.int32, logits.shape, 0)
    neg = -jnp.inf
    gl = jnp.where(row < N_GROUPS, logits, neg)
    g_max = jnp.max(gl, axis=0, keepdims=True)
    g_idx = jnp.min(jnp.where(gl == g_max, row, LANES), axis=0, keepdims=True)
    g_p = 1.0 / jnp.sum(jnp.exp(gl - g_max), axis=0, keepdims=True)
    lo = EXPERT_LANE0 + g_idx * EXP_PER_GROUP
    in_group = (row >= lo) & (row < lo + EXP_PER_GROUP)
    el = jnp.where(in_group, logits, neg)
    e_max = jnp.max(el, axis=0, keepdims=True)
    pe = jnp.exp(el - e_max)
    e_prob = pe / jnp.sum(pe, axis=0, keepdims=True)
    p1 = jnp.max(e_prob, axis=0, keepdims=True)
    i1 = jnp.min(jnp.where(e_prob == p1, row, LANES), axis=0, keepdims=True)
    rest = jnp.where(in_group & (row != i1), e_prob, -1.0)
    p2 = jnp.max(rest, axis=0, keepdims=True)
    i2 = jnp.min(jnp.where(rest == p2, row, LANES), axis=0, keepdims=True)
    denom = p1 + p2
    w1 = g_p * (p1 / denom)
    w2 = g_p * (p2 / denom)
    e1 = i1 - EXPERT_LANE0
    e2 = i2 - EXPERT_LANE0
    oh1 = row == e1
    oh2 = row == e2
    picked = jnp.where(oh1 | oh2, 1.0, 0.0).astype(BF16)
    cnt = _dot_nt(jnp.ones((8, tm), BF16), picked)
    cnt_ref[0] = jnp.concatenate([cnt, jnp.zeros((8, LANES - ROUTER_ROWS), F32)], axis=1)
    ki = lax.broadcasted_iota(jnp.int32, (ROUTER_ROWS, ROUTER_ROWS), 0)
    kj = lax.broadcasted_iota(jnp.int32, (ROUTER_ROWS, ROUTER_ROWS), 1)
    cnt_col = jnp.sum(picked.astype(F32), axis=1, keepdims=True)
    pieces = jnp.broadcast_to(jnp.ceil(cnt_col * (1.0 / PIECE)), (ROUTER_ROWS, LANES)).astype(BF16)
    run_start = PIECE * jnp.dot(jnp.where(kj < ki, 1.0, 0.0).astype(BF16), pieces,
                                preferred_element_type=F32)[:, 0:1]
    ri = lax.broadcasted_iota(jnp.int32, (tm, tm), 0)
    ci = lax.broadcasted_iota(jnp.int32, (tm, tm), 1)
    base = jnp.dot(picked, jnp.where(ri < ci, 1.0, 0.0).astype(BF16), preferred_element_type=F32) + run_start
    pos1 = jnp.sum(jnp.where(oh1, base, 0.0), axis=0, keepdims=True)
    pos2 = jnp.sum(jnp.where(oh2, base, 0.0), axis=0, keepdims=True)
    routet = jnp.concatenate([e1.astype(F32), e2.astype(F32), w1, w2, pos1, pos2, jnp.zeros((2, tm), F32)], axis=0)
    routet_ref[0] = routet
    route_ref[...] = jnp.concatenate([routet, jnp.zeros((LANES - 8, tm), F32)], axis=0).T


def _outproj_router(prompt, sample, woa, woc, g2, wrt):
    x2d = prompt[0]
    n, d = x2d.shape
    tm = min(TOKEN_TILE, n)
    n_prompt_tiles = n // tm
    n_all = n + tm
    assert sample[0].shape[0] <= tm and sample[0].shape[0] % 8 == 0
    p_row = lambda w: pl.BlockSpec((tm, w), lambda i: (jnp.minimum(i, n_prompt_tiles - 1), 0))
    row = lambda w: pl.BlockSpec((tm, w), lambda i: (i, 0))
    return pl.pallas_call(
        functools.partial(_outproj_router_kernel, n_prompt_tiles=n_prompt_tiles),
        grid=(n_prompt_tiles + 1,),
        in_specs=[p_row(d), p_row(D_A + D_B), p_row(D_C)] + [_const_spec(a.shape) for a in sample]
                 + [_const_spec(woa.shape), _const_spec(woc.shape), _const_spec(g2.shape), _const_spec(wrt.shape)],
        out_specs=(row(d), row(d), row(LANES), pl.BlockSpec((1, 8, tm), lambda i: (i, 0, 0)),
                   pl.BlockSpec((1, 8, LANES), lambda i: (i, 0, 0))),
        out_shape=(jax.ShapeDtypeStruct((n_all, d), F32), jax.ShapeDtypeStruct((n_all, d), BF16),
                   jax.ShapeDtypeStruct((n_all, LANES), F32),
                   jax.ShapeDtypeStruct((n_all // tm, 8, tm), F32),
                   jax.ShapeDtypeStruct((n_all // tm, 8, LANES), F32)),
        compiler_params=pltpu.CompilerParams(dimension_semantics=("arbitrary",), vmem_limit_bytes=VMEM_LIMIT),
        name="outproj_router",
    )(*prompt, *sample, woa, woc, g2, wrt)


def _route_plan(cnt, n_tok, tt):
    n_tiles = n_tok // tt
    cnt = cnt[:, 0, :N_EXPERTS].astype(jnp.int32)
    pc = (cnt + PIECE - 1) // PIECE * PIECE
    local_end = jnp.cumsum(pc, axis=1)
    local_off = local_end - pc
    seg_len = jnp.sum(pc, axis=0)
    seg_pad = (seg_len + ROW_TILE - 1) // ROW_TILE * ROW_TILE
    seg_end = jnp.cumsum(seg_pad)
    seg_start = seg_end - seg_pad
    run_off = seg_start[None, :] + jnp.cumsum(pc, axis=0) - pc
    piece_row = jnp.arange(_local_rows(tt) // PIECE, dtype=jnp.int32) * PIECE
    in_run = ((local_off[:, None, :] <= piece_row[None, :, None])
              & (piece_row[None, :, None] < local_end[:, None, :]))
    piece_dst = jnp.sum(jnp.where(in_run, (run_off - local_off)[:, None, :], 0), axis=-1) + piece_row[None, :]
    n_row_tiles = _max_sorted_rows(n_tok, tt) // ROW_TILE
    tile_row0 = jnp.arange(n_row_tiles, dtype=jnp.int32) * ROW_TILE
    tile_expert = jnp.minimum(jnp.sum((seg_end[None, :] <= tile_row0[:, None]).astype(jnp.int32), axis=1),
                              N_EXPERTS - 1)
    used = seg_pad > 0
    order = jnp.cumsum(used.astype(jnp.int32)) - used.astype(jnp.int32)
    ids = jnp.arange(N_EXPERTS, dtype=jnp.int32)
    later = jnp.where(used[None, :] & (ids[None, :] > ids[:, None]), ids[None, :], N_EXPERTS)
    next_used = jnp.min(later, axis=1)
    return dict(
        piece_dst=piece_dst.reshape(-1), tile_pieces=jnp.sum(pc, axis=1) // PIECE,
        fill_off=seg_start + seg_len, fill_pieces=(seg_pad - seg_len) // PIECE,
        tile_expert=tile_expert, n_active=(seg_end[-1] // ROW_TILE).reshape(1),
        expert_order=order, next_used=next_used,
    )


def _max_sorted_rows(n_tok, tt):
    n_tiles = n_tok // tt
    rows = 2 * n_tok + n_tiles * N_EXPERTS * (PIECE - 1) + N_EXPERTS * (ROW_TILE - 1)
    return (rows + ROW_TILE - 1) // ROW_TILE * ROW_TILE


def _local_rows(tt):
    return 2 * tt + N_EXPERTS * PIECE


def _piece_copy(src, dst, sem):
    return pltpu.make_async_copy(src, dst, sem)


def _for_each_piece(n, body):
    def four(q, c):
        for u in range(4):
            body(q * 4 + u)
        return c
    lax.fori_loop(0, lax.shift_right_logical(n, 2), four, 0)

    def one(p, c):
        body(p)
        return c
    lax.fori_loop(n & ~3, n, one, 0)


def _wait_pieces(n, copy_of_rows):
    for bit in (64, 32, 16, 8, 4, 2, 1):
        @pl.when((n & bit) != 0)
        def _(bit=bit):
            copy_of_rows(bit * PIECE).wait()


def _dispatch_kernel(piece_dst, tile_pieces, fill_off, fill_pieces, n_active,
                     h2_ref, routet_ref, xs_hbm, lbuf, zbuf, sem, *, tt, n_tiles, n_row_tiles):
    i = pl.program_id(0)
    slot = i % 2
    lrows = _local_rows(tt)
    max_pieces = lrows // PIECE

    assert max_pieces < 128

    def wait_tile(tile, s):
        _wait_pieces(tile_pieces[tile],
                     lambda rows: _piece_copy(lbuf.at[s, pl.ds(0, rows)], xs_hbm.at[pl.ds(0, rows)], sem.at[s]))

    @pl.when(i >= 2)
    def _():
        wait_tile(i - 2, slot)

    pos = routet_ref[0]
    h2 = h2_ref[...]
    for r0 in range(0, lrows, SORT_CHUNK):
        row = (lax.broadcasted_iota(jnp.int32, (SORT_CHUNK, tt), 0) + r0).astype(F32)
        perm = jnp.where((row == pos[4:5, :]) | (row == pos[5:6, :]), 1.0, 0.0).astype(BF16)
        lbuf[slot, r0:r0 + SORT_CHUNK] = jnp.dot(perm, h2, preferred_element_type=F32).astype(BF16)

    def send(p):
        dst = piece_dst[i * max_pieces + p]
        _piece_copy(lbuf.at[slot, pl.ds(pl.multiple_of(p * PIECE, PIECE), PIECE)],
                    xs_hbm.at[pl.ds(pl.multiple_of(dst, PIECE), PIECE)], sem.at[slot]).start()
    _for_each_piece(tile_pieces[i], send)

    @pl.when(i == n_tiles - 1)
    def _():
        zbuf[...] = jnp.zeros(zbuf.shape, BF16)
        zpiece = zbuf.at[pl.ds(0, PIECE)]
        for e in range(N_EXPERTS):
            def body(p, c, e=e):
                _piece_copy(zpiece, xs_hbm.at[pl.ds(pl.multiple_of(fill_off[e] + p * PIECE, PIECE), PIECE)],
                            sem.at[2]).start()
                return c
            lax.fori_loop(0, fill_pieces[e], body, 0)

        def tail_body(j, c):
            _piece_copy(zbuf, xs_hbm.at[pl.ds(pl.multiple_of(j * ROW_TILE, ROW_TILE), ROW_TILE)], sem.at[3]).start()
            return c
        lax.fori_loop(n_active[0], n_row_tiles, tail_body, 0)
        for e in range(N_EXPERTS):
            def body(p, c):
                _piece_copy(zpiece, xs_hbm.at[pl.ds(0, PIECE)], sem.at[2]).wait()
                return c
            lax.fori_loop(0, fill_pieces[e], body, 0)

        def tail_wait(j, c):
            _piece_copy(zbuf, xs_hbm.at[pl.ds(0, ROW_TILE)], sem.at[3]).wait()
            return c
        lax.fori_loop(n_active[0], n_row_tiles, tail_wait, 0)
        if n_tiles > 1:
            wait_tile(i - 1, 1 - slot)
        wait_tile(i, slot)


def _dispatch(h2, routet, plan, tt):
    n, d = h2.shape
    n_tiles = n // tt
    n_rows = _max_sorted_rows(n, tt)
    kernel_fn = functools.partial(_dispatch_kernel, tt=tt, n_tiles=n_tiles, n_row_tiles=n_rows // ROW_TILE)
    grid_spec = pltpu.PrefetchScalarGridSpec(
        num_scalar_prefetch=5,
        grid=(n_tiles,),
        in_specs=[pl.BlockSpec((tt, d), lambda i, *_: (i, 0)),
                  pl.BlockSpec((1, 8, tt), lambda i, *_: (i, 0, 0))],
        out_specs=pl.BlockSpec(memory_space=pl.ANY),
        scratch_shapes=[pltpu.VMEM((2, _local_rows(tt), d), BF16), pltpu.VMEM((ROW_TILE, d), BF16),
                        pltpu.SemaphoreType.DMA((4,))],
    )
    return pl.pallas_call(
        kernel_fn,
        grid_spec=grid_spec,
        out_shape=jax.ShapeDtypeStruct((n_rows, d), BF16),
        compiler_params=pltpu.CompilerParams(dimension_semantics=("arbitrary",), vmem_limit_bytes=VMEM_LIMIT),
        name="moe_dispatch",
    )(plan["piece_dst"], plan["tile_pieces"], plan["fill_off"], plan["fill_pieces"], plan["n_active"],
      h2, routet)


def _experts_kernel(tile_expert, n_active, expert_order, next_used,
                    xs_ref, wg_hbm, wu_hbm, wd_hbm, ys_ref, wbuf_g, wbuf_u, wbuf_d, wg_b, wu_b, wd_b, sem,
                    *, layer):
    j = pl.program_id(0)
    expert = tile_expert[j]
    previous = tile_expert[jnp.maximum(j - 1, 0)]
    active = j < n_active[0]
    half = expert_order[expert] % 2

    def weight_copies(e, s):
        return (_piece_copy(wg_hbm.at[layer, e], wbuf_g.at[s], sem.at[s, 0]),
                _piece_copy(wu_hbm.at[layer, e], wbuf_u.at[s], sem.at[s, 1]),
                _piece_copy(wd_hbm.at[layer, e], wbuf_d.at[s], sem.at[s, 2]))

    @pl.when(j == 0)
    def _():
        for c in weight_copies(expert, half):
            c.start()

    @pl.when(active & ((j == 0) | (expert != previous)))
    def _():
        for c in weight_copies(expert, half):
            c.wait()
        wg_b[...] = wbuf_g[half].astype(BF16)
        wu_b[...] = wbuf_u[half].astype(BF16)
        wd_b[...] = wbuf_d[half].astype(BF16)
        upcoming = next_used[expert]

        @pl.when(upcoming < N_EXPERTS)
        def _():
            for c in weight_copies(upcoming, 1 - half):
                c.start()

    @pl.when(active)
    def _():
        x = xs_ref[...]
        gate = jnp.dot(x, wg_b[...], preferred_element_type=F32)
        up = jnp.dot(x, wu_b[...], preferred_element_type=F32)
        act = (_silu(gate) * up).astype(BF16)
        ys_ref[...] = jnp.dot(act, wd_b[...], preferred_element_type=F32).astype(BF16)

    @pl.when(jnp.logical_not(active))
    def _():
        ys_ref[...] = jnp.zeros(ys_ref.shape, BF16)


def _experts(xs, wg, wu, wd, layer, plan):
    rows, d = xs.shape
    d_e = wg.shape[-1]
    live = lambda j, te, na, *_: (jnp.minimum(j, na[0] - 1), 0)
    grid_spec = pltpu.PrefetchScalarGridSpec(
        num_scalar_prefetch=4,
        grid=(rows // ROW_TILE,),
        in_specs=[pl.BlockSpec((ROW_TILE, d), live),
                  pl.BlockSpec(memory_space=pl.ANY), pl.BlockSpec(memory_space=pl.ANY),
                  pl.BlockSpec(memory_space=pl.ANY)],
        out_specs=pl.BlockSpec((ROW_TILE, d), lambda j, *_: (j, 0)),
        scratch_shapes=[pltpu.VMEM((2, d, d_e), F32), pltpu.VMEM((2, d, d_e), F32), pltpu.VMEM((2, d_e, d), F32),
                        pltpu.VMEM((d, d_e), BF16), pltpu.VMEM((d, d_e), BF16), pltpu.VMEM((d_e, d), BF16),
                        pltpu.SemaphoreType.DMA((2, 3))],
    )
    return pl.pallas_call(
        functools.partial(_experts_kernel, layer=layer),
        grid_spec=grid_spec,
        out_shape=jax.ShapeDtypeStruct((rows, d), BF16),
        compiler_params=pltpu.CompilerParams(dimension_semantics=("arbitrary",), vmem_limit_bytes=VMEM_LIMIT),
        name="moe_experts",
    )(plan["tile_expert"], plan["n_active"], plan["expert_order"], plan["next_used"], xs, wg, wu, wd)


def _combine_kernel(piece_src, tile_pieces, x1_ref, route_ref, fg_ref, ys_hbm, out_ref, outs_ref, lbuf, sem,
                    *, tt, n_tiles, final_norm):
    i = pl.program_id(0)
    slot = i % 2
    lrows = _local_rows(tt)
    max_pieces = lrows // PIECE

    assert max_pieces < 128

    def fetch_tile(tile, s):
        def fetch(p):
            src = piece_src[tile * max_pieces + p]
            _piece_copy(ys_hbm.at[pl.ds(pl.multiple_of(src, PIECE), PIECE)],
                        lbuf.at[s, pl.ds(pl.multiple_of(p * PIECE, PIECE), PIECE)], sem.at[s]).start()
        _for_each_piece(tile_pieces[tile], fetch)

    @pl.when(i == 0)
    def _():
        lbuf[...] = jnp.zeros(lbuf.shape, BF16)
        fetch_tile(0, 0)

    @pl.when(i + 1 < n_tiles)
    def _():
        fetch_tile(i + 1, 1 - slot)

    _wait_pieces(tile_pieces[i], lambda rows: _piece_copy(ys_hbm.at[pl.ds(0, rows)],
                                                         lbuf.at[slot, pl.ds(0, rows)], sem.at[slot]))

    route = route_ref[...]
    out = x1_ref[...]
    for r0 in range(0, lrows, SORT_CHUNK):
        col = (lax.broadcasted_iota(jnp.int32, (tt, SORT_CHUNK), 1) + r0).astype(F32)
        weights = (jnp.where(col == route[:, 4:5], route[:, 2:3], 0.0)
                   + jnp.where(col == route[:, 5:6], route[:, 3:4], 0.0)).astype(BF16)
        out = out + jnp.dot(weights, lbuf[slot, r0:r0 + SORT_CHUNK], preferred_element_type=F32)
    out = _rms(out, fg_ref[...]) if final_norm else out

    @pl.when(i < n_tiles - 1)
    def _():
        out_ref[...] = out

    @pl.when(i == n_tiles - 1)
    def _():
        outs_ref[...] = out[0:outs_ref.shape[0]]


def _combine(x1, route, ys, fg, plan, tt, n_sample, final_norm):
    n, d = x1.shape
    n_tiles = n // tt
    kernel_fn = functools.partial(_combine_kernel, tt=tt, n_tiles=n_tiles, final_norm=final_norm)
    grid_spec = pltpu.PrefetchScalarGridSpec(
        num_scalar_prefetch=2,
        grid=(n_tiles,),
        in_specs=[pl.BlockSpec((tt, d), lambda i, *_: (i, 0)),
                  pl.BlockSpec((tt, LANES), lambda i, *_: (i, 0)),
                  pl.BlockSpec(fg.shape, lambda i, *_: (0, 0)),
                  pl.BlockSpec(memory_space=pl.ANY)],
        out_specs=(pl.BlockSpec((tt, d), lambda i, *_: (jnp.minimum(i, n_tiles - 2), 0)),
                   pl.BlockSpec((n_sample, d), lambda i, *_: (0, 0))),
        scratch_shapes=[pltpu.VMEM((2, _local_rows(tt), d), BF16), pltpu.SemaphoreType.DMA((2,))],
    )
    return pl.pallas_call(
        kernel_fn,
        grid_spec=grid_spec,
        out_shape=(jax.ShapeDtypeStruct((n - tt, d), F32), jax.ShapeDtypeStruct((n_sample, d), F32)),
        compiler_params=pltpu.CompilerParams(dimension_semantics=("arbitrary",), vmem_limit_bytes=VMEM_LIMIT),
        name="moe_combine",
    )(plan["piece_dst"], plan["tile_pieces"], x1, route, fg, ys)


def _routed_moe(x1, h2, route, routet, cnt, wg, wu, wd, layer, fg, n_sample, final_norm):
    n = x1.shape[0]
    tt = n // cnt.shape[0]
    plan = _route_plan(cnt, n, tt)
    xs = _dispatch(h2, routet, plan, tt)
    ys = _experts(xs, wg, wu, wd, layer, plan)
    return _combine(x1, route, ys, fg, plan, tt, n_sample, final_norm)


def _pad_lanes(a, lane0=0):
    a = a.reshape((1, -1)) if a.ndim == 1 else a
    return jnp.pad(a, ((0, 0), (lane0, LANES - lane0 - a.shape[1])))


def kernel(x_prompt, x_sample, state_conv_a, state_conv_b, state_conv_qkv, state_delta, norm1_g, w_in, conv_a_w,
           conv_b_w, ln_b_g, ln_b_b, conv_c_w, a_log, dt_bias, o_norm_g, w_out, norm2_g, w_group, w_router,
           w_gate, w_up, w_down, final_g):
    n_b, t_len, d = x_prompt.shape
    n_s = x_sample.shape[0]
    depth = w_in.shape[0]
    n_main = 3 * D_A + 2 * D_B + 3 * D_C + D_C
    xp = x_prompt.reshape(n_b * t_len, d)
    xs = x_sample.reshape(n_s, d)
    fg = final_g.reshape(1, d)
    outs = {k: [] for k in ("pa", "pb", "pq", "pd", "sa", "sb", "sq", "sd")}
    for l in range(depth):
        g1 = norm1_g[l].reshape(1, d)
        wmain = w_in[l, :, :n_main].astype(BF16)
        wsmall = _pad_lanes(jnp.concatenate([w_in[l, :, n_main + H_C:], w_in[l, :, n_main:n_main + H_C]], axis=1))
        caw, cbw, ccw = conv_a_w[l], conv_b_w[l], conv_c_w[l]
        lng, lnb = ln_b_g[l].reshape(1, D_B), ln_b_b[l].reshape(1, D_B)
        alog, dtb = _pad_lanes(a_log[l]), _pad_lanes(dt_bias[l])
        ong = o_norm_g[l].reshape(1, DV)
        woa = w_out[l][:D_A + D_B].astype(BF16)
        woc = w_out[l][D_A + D_B:].astype(BF16)
        g2 = norm2_g[l].reshape(1, d)
        wrt = _pad_lanes(jnp.concatenate(
            [w_group[l], w_router[l].transpose(1, 0, 2).reshape(d, N_EXPERTS)], axis=1))
        last = l == depth - 1

        yab, q, k, v, gate, gb, st_a, st_b, st_q = _inproj_prompt(
            xp, n_b, t_len, g1, wmain, wsmall, caw, cbw, lng, lnb, ccw, alog, dtb)
        yc, s_fin = _delta_prompt(q, k, v, gb, gate, ong, n_b, t_len)
        outs["pa"].append(st_a); outs["pb"].append(st_b); outs["pq"].append(st_q); outs["pd"].append(s_fin)

        hist_a = state_conv_a[l].transpose(1, 0, 2)
        hist_b = state_conv_b[l].transpose(1, 0, 2)
        hist_q = state_conv_qkv[l].transpose(1, 0, 2)
        yab_s, q, k, v, gate, gb, un_a, un_b, un_q = _inproj_sample(
            xs, g1, wmain, wsmall, caw, cbw, lng, lnb, ccw, alog, dtb, hist_a, hist_b, hist_q)
        yc_s, s_new = _delta_sample(q, k, v, gb, gate, ong, state_delta, l)

        x1, h2, route, routet, cnt = _outproj_router((xp, yab, yc), (xs, yab_s, yc_s), woa, woc, g2, wrt)
        xp, xs = _routed_moe(x1, h2, route, routet, cnt, w_gate, w_up, w_down, l, fg, n_s, last)
        outs["sa"].append(jnp.concatenate([state_conv_a[l][:, 1:], un_a[:, None]], axis=1))
        outs["sb"].append(jnp.concatenate([state_conv_b[l][:, 1:], un_b[:, None]], axis=1))
        outs["sq"].append(jnp.concatenate([state_conv_qkv[l][:, 1:], un_q[:, None]], axis=1))
        outs["sd"].append(s_new)

    stack = lambda key: jnp.stack(outs[key])
    return (xp.reshape(n_b, t_len, d), xs.reshape(n_s, 1, d),
            stack("pa"), stack("pb"), stack("pq"), stack("pd"),
            stack("sa"), stack("sb"), stack("sq"), stack("sd"))
```

```python
import functools

import jax
import jax.numpy as jnp
from jax import lax
from jax.experimental import pallas as pl
from jax.experimental.pallas import tpu as pltpu

F32 = jnp.float32
BF16 = jnp.bfloat16

EPS = 1e-6
H_C = 4
DK = 128
DV = 128
D_A = 256
D_B = 256
D_C = 512
W_A = 3
W_B = 31
W_C = 4
CHUNK = 64
N_GROUPS = 4
EXP_PER_GROUP = 8
N_EXPERTS = N_GROUPS * EXP_PER_GROUP
LANES = 128
PAD_A = 8
PAD_B = 32
PAD_C = 8
EXPERT_LANE0 = N_GROUPS
ROUTER_ROWS = (N_GROUPS + N_EXPERTS + 7) // 8 * 8
VMEM_LIMIT = 56 * 1024 * 1024
TOKEN_TILE = 512
PIECE = 16
ROW_TILE = 512
SORT_CHUNK = 256


def _silu(x):
    return x * (1.0 / (1.0 + jnp.exp(-x)))


def _sigmoid(x):
    return 1.0 / (1.0 + jnp.exp(-x))


def _softplus(x):
    return jnp.maximum(x, 0.0) + jnp.log1p(jnp.exp(-jnp.abs(x)))


def _rms(x, g):
    return x * lax.rsqrt(jnp.mean(x * x, axis=-1, keepdims=True) + EPS) * g


def _conv_taps(buf, w_ref, width, pad, n_rows, n_cols, row_blk, col_blk):
    row_parts = []
    for r0 in range(0, n_rows, row_blk):
        col_parts = []
        for c0 in range(0, n_cols, col_blk):
            acc = None
            for j in range(width):
                start = pad - (width - 1) + j + r0
                term = buf[start:start + row_blk, c0:c0 + col_blk] * w_ref[j:j + 1, c0:c0 + col_blk]
                acc = term if acc is None else acc + term
            col_parts.append(acc)
        row_parts.append(col_parts[0] if len(col_parts) == 1 else jnp.concatenate(col_parts, axis=1))
    return row_parts[0] if len(row_parts) == 1 else jnp.concatenate(row_parts, axis=0)


def _conv_taps_realigned(buf, shifted, w_ref, width, pad, n_rows, row_blk):
    first = pad - (width - 1)
    n_shift = n_rows + (first + width - 1) // 8 * 8 - 8
    for r in range(1, 8):
        shifted[r - 1] = buf[r:r + n_shift, :]
    row_parts = []
    for r0 in range(0, n_rows, row_blk):
        acc = None
        for j in range(width):
            q, r = divmod(first + j, 8)
            lo = 8 * q + r0
            window = buf[lo:lo + row_blk, :] if r == 0 else shifted[r - 1, lo:lo + row_blk, :]
            term = window * w_ref[j:j + 1, :]
            acc = term if acc is None else acc + term
        row_parts.append(acc)
    return row_parts[0] if len(row_parts) == 1 else jnp.concatenate(row_parts, axis=0)


def _l2norm_heads(x, scale):
    parts = []
    for h in range(H_C):
        xh = x[:, h * DK:(h + 1) * DK]
        parts.append(xh * (lax.rsqrt(jnp.sum(xh * xh, axis=-1, keepdims=True) + 1e-6) * scale))
    return jnp.concatenate(parts, axis=1)


def _dot_3pass(a, a_hi, w):
    a_lo = (a - a_hi.astype(F32)).astype(BF16)
    w_hi = w.astype(BF16)
    w_lo = (w - w_hi.astype(F32)).astype(BF16)
    n = w.shape[1]
    both = jnp.dot(a_hi, jnp.concatenate([w_hi, w_lo], axis=1), preferred_element_type=F32)
    return both[:, :n] + both[:, n:] + jnp.dot(a_lo, w_hi, preferred_element_type=F32)


def _in_projection(x, g1_ref, wmain_ref, wsmall_ref):
    h = _rms(x, g1_ref[...])
    hb = h.astype(BF16)
    z_a = jnp.dot(hb, wmain_ref[:, 0:768], preferred_element_type=F32)
    z_b = jnp.dot(hb, wmain_ref[:, 768:1280], preferred_element_type=F32)
    z_qkv = jnp.dot(hb, wmain_ref[:, 1280:2816], preferred_element_type=F32)
    z_gate = jnp.dot(hb, wmain_ref[:, 2816:3328], preferred_element_type=F32)
    z_s = _dot_3pass(h, hb, wsmall_ref[...])
    return z_a, z_b, z_qkv, z_gate, z_s


def _layer_norm_silu(x, g, b):
    mu = jnp.mean(x, axis=-1, keepdims=True)
    xc = x - mu
    y = xc * lax.rsqrt(jnp.mean(xc * xc, axis=-1, keepdims=True) + EPS)
    return _silu(y * g + b)


def _decay_and_beta(z_s, alog_ref, dtb_ref):
    lane = lax.broadcasted_iota(jnp.int32, z_s.shape, 1)
    g = -jnp.exp(alog_ref[...]) * _softplus(z_s + dtb_ref[...])
    g = jnp.where(lane < H_C, g, 0.0)
    beta = jnp.where((lane >= H_C) & (lane < 2 * H_C), _sigmoid(z_s), 0.0)
    return g, beta


def _inproj_prompt_kernel(x_ref, g1_ref, wmain_ref, wsmall_ref, caw_ref, cbw_ref, lng_ref, lnb_ref,
                          ccw_ref, alog_ref, dtb_ref,
                          yab_ref, q_ref, k_ref, v_ref, gate_ref, gb_ref, sta_ref, stb_ref, stq_ref,
                          abuf, bbuf, cbuf, bshift, *, tt, nt):
    t = pl.program_id(1)

    @pl.when(t == 0)
    def _():
        abuf[0:PAD_A, :] = jnp.zeros((PAD_A, D_A), F32)
        bbuf[0:PAD_B, :] = jnp.zeros((PAD_B, D_B), F32)
        cbuf[0:PAD_C, :] = jnp.zeros((PAD_C, 3 * D_C), F32)

    h = _rms(x_ref[...], g1_ref[...])
    hb = h.astype(BF16)
    project = lambda lo, hi: jnp.dot(hb, wmain_ref[:, lo:hi], preferred_element_type=F32)

    z_a = project(0, 3 * D_A)
    abuf[PAD_A:PAD_A + tt, :] = z_a[:, 256:512] * z_a[:, 512:768]
    conv_a = _conv_taps(abuf, caw_ref, W_A, PAD_A, tt, D_A, 64, 256)
    yab_ref[:, 0:D_A] = z_a[:, 0:256] * conv_a

    z_b = project(3 * D_A, 3 * D_A + 2 * D_B)
    bbuf[PAD_B:PAD_B + tt, :] = z_b[:, 0:256] * _sigmoid(z_b[:, 256:512])
    conv_b = _conv_taps_realigned(bbuf, bshift, cbw_ref, W_B, PAD_B, tt, 64)
    yab_ref[:, D_A:D_A + D_B] = _layer_norm_silu(conv_b, lng_ref[...], lnb_ref[...])

    qkv0 = 3 * D_A + 2 * D_B
    for part, (out_ref, scale) in enumerate(((q_ref, DK ** -0.5), (k_ref, 1.0), (v_ref, None))):
        c0 = part * D_C
        cbuf[PAD_C:PAD_C + tt, c0:c0 + D_C] = project(qkv0 + c0, qkv0 + c0 + D_C)
        conv = _silu(_conv_taps(cbuf.at[:, c0:c0 + D_C], ccw_ref.at[:, c0:c0 + D_C], W_C, PAD_C, tt, D_C, 64, 512))
        out_ref[...] = conv if scale is None else _l2norm_heads(conv, scale)
    gate_ref[...] = project(qkv0 + 3 * D_C, qkv0 + 4 * D_C)

    z_s = _dot_3pass(h, hb, wsmall_ref[...])
    g, beta = _decay_and_beta(z_s, alog_ref, dtb_ref)
    row_in_chunk = lax.broadcasted_iota(jnp.int32, g.shape, 0) & (CHUNK - 1)
    shift = 1
    while shift < CHUNK:
        g = g + jnp.where(row_in_chunk >= shift, pltpu.roll(g, shift, axis=0), 0.0)
        shift *= 2
    gb_ref[...] = g + beta

    @pl.when(t == nt - 1)
    def _():
        sta_ref[0] = abuf[PAD_A + tt - (W_A - 1):PAD_A + tt, :]
        stb_ref[0] = bbuf[PAD_B + tt - (W_B - 1):PAD_B + tt, :]
        stq_ref[0] = cbuf[PAD_C + tt - (W_C - 1):PAD_C + tt, :]

    abuf[0:PAD_A, :] = abuf[tt:tt + PAD_A, :]
    bbuf[0:PAD_B, :] = bbuf[tt:tt + PAD_B, :]
    cbuf[0:PAD_C, :] = cbuf[tt:tt + PAD_C, :]


def _const_spec(shape):
    nd = len(shape)
    return pl.BlockSpec(shape, lambda *_: (0,) * nd)


def _inproj_prompt(x2d, n_b, t_len, g1, wmain, wsmall, caw, cbw, lng, lnb, ccw, alog, dtb):
    tt = min(512, t_len)
    nt = t_len // tt
    n_tok = n_b * t_len
    d = x2d.shape[1]
    row = lambda w: pl.BlockSpec((tt, w), lambda b, t: (b * nt + t, 0))
    st = lambda r, w: pl.BlockSpec((1, r, w), lambda b, t: (b, 0, 0))
    out_shape = (
        jax.ShapeDtypeStruct((n_tok, D_A + D_B), F32),
        jax.ShapeDtypeStruct((n_tok, D_C), F32),
        jax.ShapeDtypeStruct((n_tok, D_C), F32),
        jax.ShapeDtypeStruct((n_tok, D_C), F32),
        jax.ShapeDtypeStruct((n_tok, D_C), F32),
        jax.ShapeDtypeStruct((n_tok, LANES), F32),
        jax.ShapeDtypeStruct((n_b, W_A - 1, D_A), F32),
        jax.ShapeDtypeStruct((n_b, W_B - 1, D_B), F32),
        jax.ShapeDtypeStruct((n_b, W_C - 1, 3 * D_C), F32),
    )
    return pl.pallas_call(
        functools.partial(_inproj_prompt_kernel, tt=tt, nt=nt),
        grid=(n_b, nt),
        in_specs=[row(d), _const_spec(g1.shape), _const_spec(wmain.shape), _const_spec(wsmall.shape),
                  _const_spec(caw.shape), _const_spec(cbw.shape), _const_spec(lng.shape), _const_spec(lnb.shape),
                  _const_spec(ccw.shape), _const_spec(alog.shape), _const_spec(dtb.shape)],
        out_specs=(row(D_A + D_B), row(D_C), row(D_C), row(D_C), row(D_C), row(LANES),
                   st(W_A - 1, D_A), st(W_B - 1, D_B), st(W_C - 1, 3 * D_C)),
        out_shape=out_shape,
        scratch_shapes=[pltpu.VMEM((PAD_A + tt, D_A), F32), pltpu.VMEM((PAD_B + tt, D_B), F32),
                        pltpu.VMEM((PAD_C + tt, 3 * D_C), F32), pltpu.VMEM((7, PAD_B + tt - 8, D_B), F32)],
        compiler_params=pltpu.CompilerParams(dimension_semantics=("arbitrary", "arbitrary"),
                                             vmem_limit_bytes=VMEM_LIMIT),
        name="inproj_prompt",
    )(x2d, g1, wmain, wsmall, caw, cbw, lng, lnb, ccw, alog, dtb)


def _dot_nt(a, b):
    return lax.dot_general(a, b, (((1,), (1,)), ((), ())), preferred_element_type=F32)


def _dot_tn(a, b):
    return lax.dot_general(a, b, (((0,), (0,)), ((), ())), preferred_element_type=F32)


def _bdot(a, b):
    return jnp.dot(a.astype(BF16), b.astype(BF16), preferred_element_type=F32)


def _gated_norm(o, on_g, gate):
    return o * lax.rsqrt(jnp.mean(o * o, axis=-1, keepdims=True) + EPS) * on_g * _silu(gate)


def _delta_prompt_kernel(q_ref, k_ref, v_ref, gb_ref, gate_ref, ong_ref, yc_ref, sfin_ref, s_scr, *, tq, nt):
    t = pl.program_id(1)

    @pl.when(t == 0)
    def _():
        s_scr[...] = jnp.zeros(s_scr.shape, F32)

    c = CHUNK
    n_chunks = tq // c
    n_double = c.bit_length() - 2
    inst = [(ch, h) for ch in range(n_chunks) for h in range(H_C)]
    rows = lambda ch: slice(ch * c, (ch + 1) * c)
    cols = lambda h: slice(h * DK, (h + 1) * DK)
    ri = lax.broadcasted_iota(jnp.int32, (c, c), 0)
    ci = lax.broadcasted_iota(jnp.int32, (c, c), 1)
    gb = gb_ref[...]

    q = [q_ref[rows(ch), cols(h)] for ch, h in inst]
    k = [k_ref[rows(ch), cols(h)] for ch, h in inst]
    v = [v_ref[rows(ch), cols(h)] for ch, h in inst]
    g_col = [gb[rows(ch), h:h + 1] for ch, h in inst]
    b_col = [gb[rows(ch), H_C + h:H_C + h + 1] for ch, h in inst]
    g_row = [jnp.sum(jnp.where(ri == ci, g, 0.0), axis=0, keepdims=True) for g in g_col]
    diff = [gc - gr for gc, gr in zip(g_col, g_row)]
    dec_strict = [jnp.exp(jnp.where(ri > ci, d, -jnp.inf)) for d in diff]
    dec_incl = [jnp.exp(jnp.where(ri >= ci, d, -jnp.inf)) for d in diff]
    e_g = [jnp.exp(g) for g in g_col]
    g_last = [g[c - 1:c, :] for g in g_col]

    qkk = [_dot_nt(jnp.concatenate([qi, ki], axis=0).astype(BF16), ki.astype(BF16)) for qi, ki in zip(q, k)]
    qk = [m[:c] * d for m, d in zip(qkk, dec_incl)]
    p = [-(b * m[c:] * d) for b, m, d in zip(b_col, qkk, dec_strict)]
    sol = [jnp.concatenate([vi * b, ki * (b * eg)], axis=1) for vi, ki, b, eg in zip(v, k, b_col, e_g)]
    for level in range(n_double + 1):
        sol = [s + _bdot(pi, s) for pi, s in zip(p, sol)]
        if level < n_double:
            p = [_bdot(pi, pi) for pi in p]
    kd = [ki * jnp.exp(gl - g) for ki, gl, g in zip(k, g_last, g_col)]
    qk_uw = [_bdot(m, s) for m, s in zip(qk, sol)]
    kd_uw = [_dot_tn(x.astype(BF16), s.astype(BF16)) for x, s in zip(kd, sol)]
    lhs = [jnp.concatenate([qi * eg - a[:, DV:], -b[:, DV:]], axis=0).astype(BF16)
           for qi, eg, a, b in zip(q, e_g, qk_uw, kd_uw)]
    decay = [jnp.exp(gl) for gl in g_last]

    state = [s_scr[h] for h in range(H_C)]
    for ch in range(n_chunks):
        base = ch * H_C
        r = [jnp.dot(lhs[base + h], state[h].astype(BF16), preferred_element_type=F32) for h in range(H_C)]
        for h in range(H_C):
            o = r[h][:c] + qk_uw[base + h][:, :DV]
            yc_ref[rows(ch), cols(h)] = _gated_norm(o, ong_ref[...], gate_ref[rows(ch), cols(h)])
        state = [decay[base + h] * state[h] + r[h][c:] + kd_uw[base + h][:, :DV] for h in range(H_C)]
    for h in range(H_C):
        s_scr[h] = state[h]

    @pl.when(t == nt - 1)
    def _():
        sfin_ref[0] = s_scr[...]


def _delta_prompt(q, k, v, gb, gate, ong, n_b, t_len):
    tq = min(512, t_len)
    nt = t_len // tq
    row = lambda w: pl.BlockSpec((tq, w), lambda b, t: (b * nt + t, 0))
    return pl.pallas_call(
        functools.partial(_delta_prompt_kernel, tq=tq, nt=nt),
        grid=(n_b, nt),
        in_specs=[row(D_C), row(D_C), row(D_C), row(LANES), row(D_C), _const_spec(ong.shape)],
        out_specs=(row(D_C), pl.BlockSpec((1, H_C, DK, DV), lambda b, t: (b, 0, 0, 0))),
        out_shape=(jax.ShapeDtypeStruct((n_b * t_len, D_C), F32),
                   jax.ShapeDtypeStruct((n_b, H_C, DK, DV), F32)),
        scratch_shapes=[pltpu.VMEM((H_C, DK, DV), F32)],
        compiler_params=pltpu.CompilerParams(dimension_semantics=("arbitrary", "arbitrary"),
                                             vmem_limit_bytes=VMEM_LIMIT),
        name="delta_prompt",
    )(q, k, v, gb, gate, ong)


def _inproj_sample_kernel(x_ref, g1_ref, wmain_ref, wsmall_ref, caw_ref, cbw_ref, lng_ref, lnb_ref,
                          ccw_ref, alog_ref, dtb_ref, sta_ref, stb_ref, stq_ref,
                          yab_ref, q_ref, k_ref, v_ref, gate_ref, gb_ref, nsta_ref, nstb_ref, nstq_ref):
    z_a, z_b, z_qkv, z_gate, z_s = _in_projection(x_ref[...], g1_ref, wmain_ref, wsmall_ref)

    def conv_step(state_ref, new_state_ref, new, w_ref, width):
        acc = new * w_ref[width - 1:width, :]
        for j in range(width - 1):
            acc = acc + state_ref[j] * w_ref[j:j + 1, :]
        for j in range(width - 2):
            new_state_ref[j] = state_ref[j + 1]
        new_state_ref[width - 2] = new
        return acc

    u_a = z_a[:, 256:512] * z_a[:, 512:768]
    yab_ref[:, 0:D_A] = z_a[:, 0:256] * conv_step(sta_ref, nsta_ref, u_a, caw_ref, W_A)

    u_b = z_b[:, 0:256] * _sigmoid(z_b[:, 256:512])
    yab_ref[:, D_A:D_A + D_B] = _layer_norm_silu(conv_step(stb_ref, nstb_ref, u_b, cbw_ref, W_B),
                                                 lng_ref[...], lnb_ref[...])

    qkv = _silu(conv_step(stq_ref, nstq_ref, z_qkv, ccw_ref, W_C))
    q_ref[...] = _l2norm_heads(qkv[:, 0:D_C], DK ** -0.5)
    k_ref[...] = _l2norm_heads(qkv[:, D_C:2 * D_C], 1.0)
    v_ref[...] = qkv[:, 2 * D_C:3 * D_C]
    gate_ref[...] = z_gate
    g, beta = _decay_and_beta(z_s, alog_ref, dtb_ref)
    gb_ref[...] = g + beta


def _inproj_sample(x2d, g1, wmain, wsmall, caw, cbw, lng, lnb, ccw, alog, dtb, st_a, st_b, st_q):
    n = x2d.shape[0]
    args = (x2d, g1, wmain, wsmall, caw, cbw, lng, lnb, ccw, alog, dtb, st_a, st_b, st_q)
    out_shapes = tuple((n, w) for w in (D_A + D_B, D_C, D_C, D_C, D_C, LANES)) + (st_a.shape, st_b.shape, st_q.shape)
    return pl.pallas_call(
        _inproj_sample_kernel,
        grid=(1,),
        in_specs=[_const_spec(a.shape) for a in args],
        out_specs=tuple(_const_spec(s) for s in out_shapes),
        out_shape=tuple(jax.ShapeDtypeStruct(s, F32) for s in out_shapes),
        compiler_params=pltpu.CompilerParams(dimension_semantics=("arbitrary",), vmem_limit_bytes=VMEM_LIMIT),
        name="inproj_sample",
    )(*args)


def _delta_sample_kernel(q_ref, k_ref, v_ref, gb_ref, gate_ref, ong_ref, s_ref, *rest, nb, layer):
    if layer:
        earlier_ref, yc_ref, stack_ref = rest
        stack_ref[0:layer] = earlier_ref[...]
    else:
        yc_ref, stack_ref = rest
    snew_ref = stack_ref.at[layer]
    q = q_ref[...]
    k = k_ref[...]
    rows = [k[:, h * DK:(h + 1) * DK] for h in range(H_C)] + [q[:, h * DK:(h + 1) * DK] for h in range(H_C)]
    rows.append(jnp.zeros((LANES - 2 * H_C * nb, DK), F32))
    kq_t = jnp.concatenate(rows, axis=0).T
    gb = gb_ref[...]
    inst = [(i, h) for i in range(nb) for h in range(H_C)]
    cols = lambda h: slice(h * DK, (h + 1) * DK)
    k_bc = [jnp.broadcast_to(kq_t[:, h * nb + i:h * nb + i + 1], (DK, DV)) for i, h in inst]
    q_bc = [jnp.broadcast_to(kq_t[:, (H_C + h) * nb + i:(H_C + h) * nb + i + 1], (DK, DV)) for i, h in inst]
    k_s = [jnp.sum(s_ref[i, h] * kb, axis=0, keepdims=True) for (i, h), kb in zip(inst, k_bc)]
    q_s = [jnp.sum(s_ref[i, h] * qb, axis=0, keepdims=True) for (i, h), qb in zip(inst, q_bc)]
    e_g = [jnp.exp(gb[i:i + 1, h:h + 1]) for i, h in inst]
    v_new = [gb[i:i + 1, H_C + h:H_C + h + 1] * (v_ref[i:i + 1, cols(h)] - eg * ks)
             for (i, h), eg, ks in zip(inst, e_g, k_s)]
    qk = [jnp.sum(q[i:i + 1, cols(h)] * k[i:i + 1, cols(h)], axis=-1, keepdims=True) for i, h in inst]
    for n, (i, h) in enumerate(inst):
        snew_ref[i, h] = e_g[n] * s_ref[i, h] + k_bc[n] * v_new[n]
    for n, (i, h) in enumerate(inst):
        o = e_g[n] * q_s[n] + qk[n] * v_new[n]
        yc_ref[i:i + 1, cols(h)] = _gated_norm(o, ong_ref[...], gate_ref[i:i + 1, cols(h)])


def _delta_sample(q, k, v, gb, gate, ong, state, layer, earlier):
    n = q.shape[0]
    nb = 8
    row = lambda w: pl.BlockSpec((nb, w), lambda i: (i, 0))
    st_in = pl.BlockSpec((None, nb, H_C, DK, DV), lambda i: (layer, i, 0, 0, 0))
    stack = lambda depth: pl.BlockSpec((depth, nb, H_C, DK, DV), lambda i: (0, i, 0, 0, 0))
    carried = [] if earlier is None else [earlier]
    return pl.pallas_call(
        functools.partial(_delta_sample_kernel, nb=nb, layer=layer),
        grid=(n // nb,),
        in_specs=[row(D_C), row(D_C), row(D_C), row(LANES), row(D_C), _const_spec(ong.shape), st_in]
                 + [stack(layer) for _ in carried],
        out_specs=(row(D_C), stack(layer + 1)),
        out_shape=(jax.ShapeDtypeStruct((n, D_C), F32),
                   jax.ShapeDtypeStruct((layer + 1,) + state.shape[1:], F32)),
        compiler_params=pltpu.CompilerParams(dimension_semantics=("arbitrary",), vmem_limit_bytes=VMEM_LIMIT),
        name="delta_sample",
    )(q, k, v, gb, gate, ong, state, *carried)


def _outproj_router_kernel(x_ref, yab_ref, yc_ref, xs_ref, yabs_ref, ycs_ref, woa_ref, woc_ref, g2_ref, wrt_ref,
                           x1_ref, h2_ref, route_ref, routet_ref, cnt_ref, *, n_prompt_tiles):
    is_prompt = pl.program_id(0) < n_prompt_tiles
    tm = x_ref.shape[0]

    def pick(prompt_ref, sample_ref):
        sample = sample_ref[...]
        sample = jnp.concatenate([sample, jnp.zeros((tm - sample.shape[0], sample.shape[1]), F32)], axis=0)
        return jnp.where(is_prompt, prompt_ref[...], sample)

    x1 = (pick(x_ref, xs_ref)
          + jnp.dot(pick(yab_ref, yabs_ref).astype(BF16), woa_ref[...], preferred_element_type=F32)
          + jnp.dot(pick(yc_ref, ycs_ref).astype(BF16), woc_ref[...], preferred_element_type=F32))
    x1_ref[...] = x1
    h2 = _rms(x1, g2_ref[...])
    h2_ref[...] = h2.astype(BF16)
    logits = _dot_3pass(h2, h2.astype(BF16), wrt_ref[...]).T[0:ROUTER_ROWS, :]
    row = lax.broadcasted_iota(jnp.int32, logits.shape, 0)
    neg = -jnp.inf
    gl = jnp.where(row < N_GROUPS, logits, neg)
    g_max = jnp.max(gl, axis=0, keepdims=True)
    g_idx = jnp.min(jnp.where(gl == g_max, row, LANES), axis=0, keepdims=True)
    g_p = 1.0 / jnp.sum(jnp.exp(gl - g_max), axis=0, keepdims=True)
    lo = EXPERT_LANE0 + g_idx * EXP_PER_GROUP
    in_group = (row >= lo) & (row < lo + EXP_PER_GROUP)
    el = jnp.where(in_group, logits, neg)
    e_max = jnp.max(el, axis=0, keepdims=True)
    pe = jnp.exp(el - e_max)
    e_prob = pe / jnp.sum(pe, axis=0, keepdims=True)
    p1 = jnp.max(e_prob, axis=0, keepdims=True)
    i1 = jnp.min(jnp.where(e_prob == p1, row, LANES), axis=0, keepdims=True)
    rest = jnp.where(in_group & (row != i1), e_prob, -1.0)
    p2 = jnp.max(rest, axis=0, keepdims=True)
    i2 = jnp.min(jnp.where(rest == p2, row, LANES), axis=0, keepdims=True)
    denom = p1 + p2
    w1 = g_p * (p1 / denom)
    w2 = g_p * (p2 / denom)
    e1 = i1 - EXPERT_LANE0
    e2 = i2 - EXPERT_LANE0
    oh1 = row == e1
    oh2 = row == e2
    picked = jnp.where(oh1 | oh2, 1.0, 0.0).astype(BF16)
    cnt = _dot_nt(jnp.ones((8, tm), BF16), picked)
    cnt_ref[0] = jnp.concatenate([cnt, jnp.zeros((8, LANES - ROUTER_ROWS), F32)], axis=1)
    ki = lax.broadcasted_iota(jnp.int32, (ROUTER_ROWS, ROUTER_ROWS), 0)
    kj = lax.broadcasted_iota(jnp.int32, (ROUTER_ROWS, ROUTER_ROWS), 1)
    cnt_col = jnp.sum(picked.astype(F32), axis=1, keepdims=True)
    pieces = jnp.broadcast_to(jnp.ceil(cnt_col * (1.0 / PIECE)), (ROUTER_ROWS, LANES)).astype(BF16)
    run_start = PIECE * jnp.dot(jnp.where(kj < ki, 1.0, 0.0).astype(BF16), pieces,
                                preferred_element_type=F32)[:, 0:1]
    ri = lax.broadcasted_iota(jnp.int32, (tm, tm), 0)
    ci = lax.broadcasted_iota(jnp.int32, (tm, tm), 1)
    base = jnp.dot(picked, jnp.where(ri < ci, 1.0, 0.0).astype(BF16), preferred_element_type=F32) + run_start
    pos1 = jnp.sum(jnp.where(oh1, base, 0.0), axis=0, keepdims=True)
    pos2 = jnp.sum(jnp.where(oh2, base, 0.0), axis=0, keepdims=True)
    routet = jnp.concatenate([e1.astype(F32), e2.astype(F32), w1, w2, pos1, pos2, jnp.zeros((2, tm), F32)], axis=0)
    routet_ref[0] = routet
    route_ref[...] = jnp.concatenate([routet, jnp.zeros((LANES - 8, tm), F32)], axis=0).T


def _outproj_router(prompt, sample, woa, woc, g2, wrt):
    x2d = prompt[0]
    n, d = x2d.shape
    tm = min(TOKEN_TILE, n)
    n_prompt_tiles = n // tm
    n_all = n + tm
    assert sample[0].shape[0] <= tm and sample[0].shape[0] % 8 == 0
    p_row = lambda w: pl.BlockSpec((tm, w), lambda i: (jnp.minimum(i, n_prompt_tiles - 1), 0))
    row = lambda w: pl.BlockSpec((tm, w), lambda i: (i, 0))
    return pl.pallas_call(
        functools.partial(_outproj_router_kernel, n_prompt_tiles=n_prompt_tiles),
        grid=(n_prompt_tiles + 1,),
        in_specs=[p_row(d), p_row(D_A + D_B), p_row(D_C)] + [_const_spec(a.shape) for a in sample]
                 + [_const_spec(woa.shape), _const_spec(woc.shape), _const_spec(g2.shape), _const_spec(wrt.shape)],
        out_specs=(row(d), row(d), row(LANES), pl.BlockSpec((1, 8, tm), lambda i: (i, 0, 0)),
                   pl.BlockSpec((1, 8, LANES), lambda i: (i, 0, 0))),
        out_shape=(jax.ShapeDtypeStruct((n_all, d), F32), jax.ShapeDtypeStruct((n_all, d), BF16),
                   jax.ShapeDtypeStruct((n_all, LANES), F32),
                   jax.ShapeDtypeStruct((n_all // tm, 8, tm), F32),
                   jax.ShapeDtypeStruct((n_all // tm, 8, LANES), F32)),
        compiler_params=pltpu.CompilerParams(dimension_semantics=("arbitrary",), vmem_limit_bytes=VMEM_LIMIT),
        name="outproj_router",
    )(*prompt, *sample, woa, woc, g2, wrt)


def _route_plan(cnt, n_tok, tt):
    n_tiles = n_tok // tt
    cnt = cnt[:, 0, :N_EXPERTS].astype(jnp.int32)
    pc = (cnt + PIECE - 1) // PIECE * PIECE
    local_end = jnp.cumsum(pc, axis=1)
    local_off = local_end - pc
    seg_len = jnp.sum(pc, axis=0)
    seg_pad = (seg_len + ROW_TILE - 1) // ROW_TILE * ROW_TILE
    seg_end = jnp.cumsum(seg_pad)
    seg_start = seg_end - seg_pad
    run_off = seg_start[None, :] + jnp.cumsum(pc, axis=0) - pc
    piece_row = jnp.arange(_local_rows(tt) // PIECE, dtype=jnp.int32) * PIECE
    in_run = ((local_off[:, None, :] <= piece_row[None, :, None])
              & (piece_row[None, :, None] < local_end[:, None, :]))
    piece_dst = jnp.sum(jnp.where(in_run, (run_off - local_off)[:, None, :], 0), axis=-1) + piece_row[None, :]
    n_row_tiles = _max_sorted_rows(n_tok, tt) // ROW_TILE
    tile_row0 = jnp.arange(n_row_tiles, dtype=jnp.int32) * ROW_TILE
    tile_expert = jnp.minimum(jnp.sum((seg_end[None, :] <= tile_row0[:, None]).astype(jnp.int32), axis=1),
                              N_EXPERTS - 1)
    used = seg_pad > 0
    order = jnp.cumsum(used.astype(jnp.int32)) - used.astype(jnp.int32)
    ids = jnp.arange(N_EXPERTS, dtype=jnp.int32)
    later = jnp.where(used[None, :] & (ids[None, :] > ids[:, None]), ids[None, :], N_EXPERTS)
    next_used = jnp.min(later, axis=1)
    return dict(
        piece_dst=piece_dst.reshape(-1), tile_pieces=jnp.sum(pc, axis=1) // PIECE,
        fill_off=seg_start + seg_len, fill_pieces=(seg_pad - seg_len) // PIECE,
        tile_expert=tile_expert, n_active=(seg_end[-1] // ROW_TILE).reshape(1),
        expert_order=order, next_used=next_used,
    )


def _max_sorted_rows(n_tok, tt):
    n_tiles = n_tok // tt
    rows = 2 * n_tok + n_tiles * N_EXPERTS * (PIECE - 1) + N_EXPERTS * (ROW_TILE - 1)
    return (rows + ROW_TILE - 1) // ROW_TILE * ROW_TILE


def _local_rows(tt):
    return 2 * tt + N_EXPERTS * PIECE


def _piece_copy(src, dst, sem):
    return pltpu.make_async_copy(src, dst, sem)


def _for_each_piece(n, body):
    def four(q, c):
        for u in range(4):
            body(q * 4 + u)
        return c
    lax.fori_loop(0, lax.shift_right_logical(n, 2), four, 0)

    def one(p, c):
        body(p)
        return c
    lax.fori_loop(n & ~3, n, one, 0)


def _wait_pieces(n, copy_of_rows):
    for bit in (64, 32, 16, 8, 4, 2, 1):
        @pl.when((n & bit) != 0)
        def _(bit=bit):
            copy_of_rows(bit * PIECE).wait()


def _dispatch_kernel(piece_dst, tile_pieces, fill_off, fill_pieces, n_active,
                     h2_ref, routet_ref, xs_hbm, lbuf, zbuf, sem, *, tt, n_tiles, n_row_tiles):
    i = pl.program_id(0)
    slot = i % 2
    lrows = _local_rows(tt)
    max_pieces = lrows // PIECE

    assert max_pieces < 128

    def wait_tile(tile, s):
        _wait_pieces(tile_pieces[tile],
                     lambda rows: _piece_copy(lbuf.at[s, pl.ds(0, rows)], xs_hbm.at[pl.ds(0, rows)], sem.at[s]))

    @pl.when(i >= 2)
    def _():
        wait_tile(i - 2, slot)

    pos = routet_ref[0]
    h2 = h2_ref[...]
    for r0 in range(0, lrows, SORT_CHUNK):
        row = (lax.broadcasted_iota(jnp.int32, (SORT_CHUNK, tt), 0) + r0).astype(F32)
        perm = jnp.where((row == pos[4:5, :]) | (row == pos[5:6, :]), 1.0, 0.0).astype(BF16)
        lbuf[slot, r0:r0 + SORT_CHUNK] = jnp.dot(perm, h2, preferred_element_type=F32).astype(BF16)

    def send(p):
        dst = piece_dst[i * max_pieces + p]
        _piece_copy(lbuf.at[slot, pl.ds(pl.multiple_of(p * PIECE, PIECE), PIECE)],
                    xs_hbm.at[pl.ds(pl.multiple_of(dst, PIECE), PIECE)], sem.at[slot]).start()
    _for_each_piece(tile_pieces[i], send)

    @pl.when(i == n_tiles - 1)
    def _():
        zbuf[...] = jnp.zeros(zbuf.shape, BF16)
        zpiece = zbuf.at[pl.ds(0, PIECE)]
        for e in range(N_EXPERTS):
            def body(p, c, e=e):
                _piece_copy(zpiece, xs_hbm.at[pl.ds(pl.multiple_of(fill_off[e] + p * PIECE, PIECE), PIECE)],
                            sem.at[2]).start()
                return c
            lax.fori_loop(0, fill_pieces[e], body, 0)

        def tail_body(j, c):
            _piece_copy(zbuf, xs_hbm.at[pl.ds(pl.multiple_of(j * ROW_TILE, ROW_TILE), ROW_TILE)], sem.at[3]).start()
            return c
        lax.fori_loop(n_active[0], n_row_tiles, tail_body, 0)
        for e in range(N_EXPERTS):
            def body(p, c):
                _piece_copy(zpiece, xs_hbm.at[pl.ds(0, PIECE)], sem.at[2]).wait()
                return c
            lax.fori_loop(0, fill_pieces[e], body, 0)

        def tail_wait(j, c):
            _piece_copy(zbuf, xs_hbm.at[pl.ds(0, ROW_TILE)], sem.at[3]).wait()
            return c
        lax.fori_loop(n_active[0], n_row_tiles, tail_wait, 0)
        if n_tiles > 1:
            wait_tile(i - 1, 1 - slot)
        wait_tile(i, slot)


def _dispatch(h2, routet, plan, tt):
    n, d = h2.shape
    n_tiles = n // tt
    n_rows = _max_sorted_rows(n, tt)
    kernel_fn = functools.partial(_dispatch_kernel, tt=tt, n_tiles=n_tiles, n_row_tiles=n_rows // ROW_TILE)
    grid_spec = pltpu.PrefetchScalarGridSpec(
        num_scalar_prefetch=5,
        grid=(n_tiles,),
        in_specs=[pl.BlockSpec((tt, d), lambda i, *_: (i, 0)),
                  pl.BlockSpec((1, 8, tt), lambda i, *_: (i, 0, 0))],
        out_specs=pl.BlockSpec(memory_space=pl.ANY),
        scratch_shapes=[pltpu.VMEM((2, _local_rows(tt), d), BF16), pltpu.VMEM((ROW_TILE, d), BF16),
                        pltpu.SemaphoreType.DMA((4,))],
    )
    return pl.pallas_call(
        kernel_fn,
        grid_spec=grid_spec,
        out_shape=jax.ShapeDtypeStruct((n_rows, d), BF16),
        compiler_params=pltpu.CompilerParams(dimension_semantics=("arbitrary",), vmem_limit_bytes=VMEM_LIMIT),
        name="moe_dispatch",
    )(plan["piece_dst"], plan["tile_pieces"], plan["fill_off"], plan["fill_pieces"], plan["n_active"],
      h2, routet)


def _experts_kernel(tile_expert, n_active, expert_order, next_used,
                    xs_ref, wg_hbm, wu_hbm, wd_hbm, ys_ref, wbuf_g, wbuf_u, wbuf_d, wg_b, wu_b, wd_b, sem,
                    *, layer):
    j = pl.program_id(0)
    expert = tile_expert[j]
    previous = tile_expert[jnp.maximum(j - 1, 0)]
    active = j < n_active[0]
    half = expert_order[expert] % 2

    def weight_copies(e, s):
        return (_piece_copy(wg_hbm.at[layer, e], wbuf_g.at[s], sem.at[s, 0]),
                _piece_copy(wu_hbm.at[layer, e], wbuf_u.at[s], sem.at[s, 1]),
                _piece_copy(wd_hbm.at[layer, e], wbuf_d.at[s], sem.at[s, 2]))

    @pl.when(j == 0)
    def _():
        for c in weight_copies(expert, half):
            c.start()

    @pl.when(active & ((j == 0) | (expert != previous)))
    def _():
        for c in weight_copies(expert, half):
            c.wait()
        wg_b[...] = wbuf_g[half].astype(BF16)
        wu_b[...] = wbuf_u[half].astype(BF16)
        wd_b[...] = wbuf_d[half].astype(BF16)
        upcoming = next_used[expert]

        @pl.when(upcoming < N_EXPERTS)
        def _():
            for c in weight_copies(upcoming, 1 - half):
                c.start()

    @pl.when(active)
    def _():
        x = xs_ref[...]
        gate = jnp.dot(x, wg_b[...], preferred_element_type=F32)
        up = jnp.dot(x, wu_b[...], preferred_element_type=F32)
        act = (_silu(gate) * up).astype(BF16)
        ys_ref[...] = jnp.dot(act, wd_b[...], preferred_element_type=F32).astype(BF16)

    @pl.when(jnp.logical_not(active))
    def _():
        ys_ref[...] = jnp.zeros(ys_ref.shape, BF16)


def _experts(xs, wg, wu, wd, layer, plan):
    rows, d = xs.shape
    d_e = wg.shape[-1]
    live = lambda j, te, na, *_: (jnp.minimum(j, na[0] - 1), 0)
    grid_spec = pltpu.PrefetchScalarGridSpec(
        num_scalar_prefetch=4,
        grid=(rows // ROW_TILE,),
        in_specs=[pl.BlockSpec((ROW_TILE, d), live),
                  pl.BlockSpec(memory_space=pl.ANY), pl.BlockSpec(memory_space=pl.ANY),
                  pl.BlockSpec(memory_space=pl.ANY)],
        out_specs=pl.BlockSpec((ROW_TILE, d), lambda j, *_: (j, 0)),
        scratch_shapes=[pltpu.VMEM((2, d, d_e), F32), pltpu.VMEM((2, d, d_e), F32), pltpu.VMEM((2, d_e, d), F32),
                        pltpu.VMEM((d, d_e), BF16), pltpu.VMEM((d, d_e), BF16), pltpu.VMEM((d_e, d), BF16),
                        pltpu.SemaphoreType.DMA((2, 3))],
    )
    return pl.pallas_call(
        functools.partial(_experts_kernel, layer=layer),
        grid_spec=grid_spec,
        out_shape=jax.ShapeDtypeStruct((rows, d), BF16),
        compiler_params=pltpu.CompilerParams(dimension_semantics=("arbitrary",), vmem_limit_bytes=VMEM_LIMIT),
        name="moe_experts",
    )(plan["tile_expert"], plan["n_active"], plan["expert_order"], plan["next_used"], xs, wg, wu, wd)


def _combine_kernel(piece_src, tile_pieces, x1_ref, route_ref, fg_ref, ys_hbm, out_ref, outs_ref, lbuf, sem,
                    *, tt, n_tiles, final_norm):
    i = pl.program_id(0)
    slot = i % 2
    lrows = _local_rows(tt)
    max_pieces = lrows // PIECE

    assert max_pieces < 128

    def fetch_tile(tile, s):
        def fetch(p):
            src = piece_src[tile * max_pieces + p]
            _piece_copy(ys_hbm.at[pl.ds(pl.multiple_of(src, PIECE), PIECE)],
                        lbuf.at[s, pl.ds(pl.multiple_of(p * PIECE, PIECE), PIECE)], sem.at[s]).start()
        _for_each_piece(tile_pieces[tile], fetch)

    @pl.when(i == 0)
    def _():
        lbuf[...] = jnp.zeros(lbuf.shape, BF16)
        fetch_tile(0, 0)

    @pl.when(i + 1 < n_tiles)
    def _():
        fetch_tile(i + 1, 1 - slot)

    _wait_pieces(tile_pieces[i], lambda rows: _piece_copy(ys_hbm.at[pl.ds(0, rows)],
                                                         lbuf.at[slot, pl.ds(0, rows)], sem.at[slot]))

    route = route_ref[...]
    out = x1_ref[...]
    for r0 in range(0, lrows, SORT_CHUNK):
        col = (lax.broadcasted_iota(jnp.int32, (tt, SORT_CHUNK), 1) + r0).astype(F32)
        weights = (jnp.where(col == route[:, 4:5], route[:, 2:3], 0.0)
                   + jnp.where(col == route[:, 5:6], route[:, 3:4], 0.0)).astype(BF16)
        out = out + jnp.dot(weights, lbuf[slot, r0:r0 + SORT_CHUNK], preferred_element_type=F32)
    out = _rms(out, fg_ref[...]) if final_norm else out

    @pl.when(i < n_tiles - 1)
    def _():
        out_ref[...] = out

    @pl.when(i == n_tiles - 1)
    def _():
        outs_ref[...] = out[0:outs_ref.shape[0]]


def _combine(x1, route, ys, fg, plan, tt, n_sample, final_norm):
    n, d = x1.shape
    n_tiles = n // tt
    kernel_fn = functools.partial(_combine_kernel, tt=tt, n_tiles=n_tiles, final_norm=final_norm)
    grid_spec = pltpu.PrefetchScalarGridSpec(
        num_scalar_prefetch=2,
        grid=(n_tiles,),
        in_specs=[pl.BlockSpec((tt, d), lambda i, *_: (i, 0)),
                  pl.BlockSpec((tt, LANES), lambda i, *_: (i, 0)),
                  pl.BlockSpec(fg.shape, lambda i, *_: (0, 0)),
                  pl.BlockSpec(memory_space=pl.ANY)],
        out_specs=(pl.BlockSpec((tt, d), lambda i, *_: (jnp.minimum(i, n_tiles - 2), 0)),
                   pl.BlockSpec((n_sample, d), lambda i, *_: (0, 0))),
        scratch_shapes=[pltpu.VMEM((2, _local_rows(tt), d), BF16), pltpu.SemaphoreType.DMA((2,))],
    )
    return pl.pallas_call(
        kernel_fn,
        grid_spec=grid_spec,
        out_shape=(jax.ShapeDtypeStruct((n - tt, d), F32), jax.ShapeDtypeStruct((n_sample, d), F32)),
        compiler_params=pltpu.CompilerParams(dimension_semantics=("arbitrary",), vmem_limit_bytes=VMEM_LIMIT),
        name="moe_combine",
    )(plan["piece_dst"], plan["tile_pieces"], x1, route, fg, ys)


def _routed_moe(x1, h2, route, routet, cnt, wg, wu, wd, layer, fg, n_sample, final_norm):
    n = x1.shape[0]
    tt = n // cnt.shape[0]
    plan = _route_plan(cnt, n, tt)
    xs = _dispatch(h2, routet, plan, tt)
    ys = _experts(xs, wg, wu, wd, layer, plan)
    return _combine(x1, route, ys, fg, plan, tt, n_sample, final_norm)


def _pad_lanes(a):
    a = a.reshape((1, -1)) if a.ndim == 1 else a
    return jnp.pad(a, ((0, 0), (0, LANES - a.shape[1])))


def kernel(x_prompt, x_sample, state_conv_a, state_conv_b, state_conv_qkv, state_delta, norm1_g, w_in, conv_a_w,
           conv_b_w, ln_b_g, ln_b_b, conv_c_w, a_log, dt_bias, o_norm_g, w_out, norm2_g, w_group, w_router,
           w_gate, w_up, w_down, final_g):
    n_b, t_len, d = x_prompt.shape
    n_s = x_sample.shape[0]
    depth = w_in.shape[0]
    n_main = 3 * D_A + 2 * D_B + 3 * D_C + D_C
    xp = x_prompt.reshape(n_b * t_len, d)
    xs = x_sample.reshape(n_s, d)
    fg = final_g.reshape(1, d)
    outs = {k: [] for k in ("pa", "pb", "pq", "pd", "sa", "sb", "sq")}
    sample_states = None
    for l in range(depth):
        g1 = norm1_g[l].reshape(1, d)
        wmain = w_in[l, :, :n_main].astype(BF16)
        wsmall = _pad_lanes(jnp.concatenate([w_in[l, :, n_main + H_C:], w_in[l, :, n_main:n_main + H_C]], axis=1))
        caw, cbw, ccw = conv_a_w[l], conv_b_w[l], conv_c_w[l]
        lng, lnb = ln_b_g[l].reshape(1, D_B), ln_b_b[l].reshape(1, D_B)
        alog, dtb = _pad_lanes(a_log[l]), _pad_lanes(dt_bias[l])
        ong = o_norm_g[l].reshape(1, DV)
        woa = w_out[l][:D_A + D_B].astype(BF16)
        woc = w_out[l][D_A + D_B:].astype(BF16)
        g2 = norm2_g[l].reshape(1, d)
        wrt = _pad_lanes(jnp.concatenate(
            [w_group[l], w_router[l].transpose(1, 0, 2).reshape(d, N_EXPERTS)], axis=1))
        last = l == depth - 1

        yab, q, k, v, gate, gb, st_a, st_b, st_q = _inproj_prompt(
            xp, n_b, t_len, g1, wmain, wsmall, caw, cbw, lng, lnb, ccw, alog, dtb)
        yc, s_fin = _delta_prompt(q, k, v, gb, gate, ong, n_b, t_len)
        outs["pa"].append(st_a); outs["pb"].append(st_b); outs["pq"].append(st_q); outs["pd"].append(s_fin)

        hist_a = state_conv_a[l].transpose(1, 0, 2)
        hist_b = state_conv_b[l].transpose(1, 0, 2)
        hist_q = state_conv_qkv[l].transpose(1, 0, 2)
        yab_s, q, k, v, gate, gb, hist_a, hist_b, hist_q = _inproj_sample(
            xs, g1, wmain, wsmall, caw, cbw, lng, lnb, ccw, alog, dtb, hist_a, hist_b, hist_q)
        yc_s, sample_states = _delta_sample(q, k, v, gb, gate, ong, state_delta, l, sample_states)

        x1, h2, route, routet, cnt = _outproj_router((xp, yab, yc), (xs, yab_s, yc_s), woa, woc, g2, wrt)
        xp, xs = _routed_moe(x1, h2, route, routet, cnt, w_gate, w_up, w_down, l, fg, n_s, last)
        outs["sa"].append(hist_a.transpose(1, 0, 2))
        outs["sb"].append(hist_b.transpose(1, 0, 2))
        outs["sq"].append(hist_q.transpose(1, 0, 2))

    stack = lambda key: jnp.stack(outs[key])
    return (xp.reshape(n_b, t_len, d), xs.reshape(n_s, 1, d),
            stack("pa"), stack("pb"), stack("pq"), stack("pd"),
            stack("sa"), stack("sb"), stack("sq"), sample_states)
```

```python
import functools

import jax
import jax.numpy as jnp
from jax import lax
from jax.experimental import pallas as pl
from jax.experimental.pallas import tpu as pltpu

F32 = jnp.float32
BF16 = jnp.bfloat16

EPS = 1e-6
H_C = 4
DK = 128
DV = 128
D_A = 256
D_B = 256
D_C = 512
W_A = 3
W_B = 31
W_C = 4
CHUNK = 64
N_GROUPS = 4
EXP_PER_GROUP = 8
N_EXPERTS = N_GROUPS * EXP_PER_GROUP
LANES = 128
PAD_A = 8
PAD_B = 32
PAD_C = 8
EXPERT_LANE0 = N_GROUPS
ROUTER_ROWS = (N_GROUPS + N_EXPERTS + 7) // 8 * 8
VMEM_LIMIT = 56 * 1024 * 1024
TIME_TILE = 512
TOKEN_TILE = 512
PIECE = 16
ROW_TILE = 512
SORT_CHUNK = 256


def _silu(x):
    return x * (1.0 / (1.0 + jnp.exp(-x)))


def _sigmoid(x):
    return 1.0 / (1.0 + jnp.exp(-x))


def _softplus(x):
    return jnp.maximum(x, 0.0) + jnp.log1p(jnp.exp(-jnp.abs(x)))


def _rms(x, g):
    return x * lax.rsqrt(jnp.mean(x * x, axis=-1, keepdims=True) + EPS) * g


def _conv_taps(buf, w_ref, width, pad, n_rows, n_cols, row_blk, col_blk):
    row_parts = []
    for r0 in range(0, n_rows, row_blk):
        col_parts = []
        for c0 in range(0, n_cols, col_blk):
            acc = None
            for j in range(width):
                start = pad - (width - 1) + j + r0
                term = buf[start:start + row_blk, c0:c0 + col_blk] * w_ref[j:j + 1, c0:c0 + col_blk]
                acc = term if acc is None else acc + term
            col_parts.append(acc)
        row_parts.append(col_parts[0] if len(col_parts) == 1 else jnp.concatenate(col_parts, axis=1))
    return row_parts[0] if len(row_parts) == 1 else jnp.concatenate(row_parts, axis=0)


def _conv_taps_realigned(buf, shifted, w_ref, width, pad, n_rows, row_blk):
    first = pad - (width - 1)
    n_shift = n_rows + (first + width - 1) // 8 * 8 - 8
    for r in range(1, 8):
        shifted[r - 1] = buf[r:r + n_shift, :]
    row_parts = []
    for r0 in range(0, n_rows, row_blk):
        acc = None
        for j in range(width):
            q, r = divmod(first + j, 8)
            lo = 8 * q + r0
            window = buf[lo:lo + row_blk, :] if r == 0 else shifted[r - 1, lo:lo + row_blk, :]
            term = window * w_ref[j:j + 1, :]
            acc = term if acc is None else acc + term
        row_parts.append(acc)
    return row_parts[0] if len(row_parts) == 1 else jnp.concatenate(row_parts, axis=0)


def _l2norm_heads(x, scale):
    parts = []
    for h in range(H_C):
        xh = x[:, h * DK:(h + 1) * DK]
        parts.append(xh * (lax.rsqrt(jnp.sum(xh * xh, axis=-1, keepdims=True) + 1e-6) * scale))
    return jnp.concatenate(parts, axis=1)


def _dot_3pass(a, a_hi, w):
    a_lo = (a - a_hi.astype(F32)).astype(BF16)
    w_hi = w.astype(BF16)
    w_lo = (w - w_hi.astype(F32)).astype(BF16)
    n = w.shape[1]
    both = jnp.dot(a_hi, jnp.concatenate([w_hi, w_lo], axis=1), preferred_element_type=F32)
    return both[:, :n] + both[:, n:] + jnp.dot(a_lo, w_hi, preferred_element_type=F32)


def _in_projection(x, g1_ref, wmain_ref, wsmall_ref):
    h = _rms(x, g1_ref[...])
    hb = h.astype(BF16)
    z_a = jnp.dot(hb, wmain_ref[:, 0:768], preferred_element_type=F32)
    z_b = jnp.dot(hb, wmain_ref[:, 768:1280], preferred_element_type=F32)
    z_qkv = jnp.dot(hb, wmain_ref[:, 1280:2816], preferred_element_type=F32)
    z_gate = jnp.dot(hb, wmain_ref[:, 2816:3328], preferred_element_type=F32)
    z_s = _dot_3pass(h, hb, wsmall_ref[...])
    return z_a, z_b, z_qkv, z_gate, z_s


def _layer_norm_silu(x, g, b):
    mu = jnp.mean(x, axis=-1, keepdims=True)
    xc = x - mu
    y = xc * lax.rsqrt(jnp.mean(xc * xc, axis=-1, keepdims=True) + EPS)
    return _silu(y * g + b)


def _decay_and_beta(z_s, alog_ref, dtb_ref):
    lane = lax.broadcasted_iota(jnp.int32, z_s.shape, 1)
    g = -jnp.exp(alog_ref[...]) * _softplus(z_s + dtb_ref[...])
    g = jnp.where(lane < H_C, g, 0.0)
    beta = jnp.where((lane >= H_C) & (lane < 2 * H_C), _sigmoid(z_s), 0.0)
    return g, beta


def _inproj_prompt_kernel(x_ref, g1_ref, wmain_ref, wsmall_ref, caw_ref, cbw_ref, lng_ref, lnb_ref,
                          ccw_ref, alog_ref, dtb_ref,
                          yab_ref, q_ref, k_ref, v_ref, gate_ref, gb_ref, sta_ref, stb_ref, stq_ref,
                          abuf, bbuf, cbuf, bshift, *, tt, nt):
    t = pl.program_id(1)

    @pl.when(t == 0)
    def _():
        abuf[0:PAD_A, :] = jnp.zeros((PAD_A, D_A), F32)
        bbuf[0:PAD_B, :] = jnp.zeros((PAD_B, D_B), F32)
        cbuf[0:PAD_C, :] = jnp.zeros((PAD_C, 3 * D_C), F32)

    h = _rms(x_ref[...], g1_ref[...])
    hb = h.astype(BF16)
    project = lambda lo, hi: jnp.dot(hb, wmain_ref[:, lo:hi], preferred_element_type=F32)

    z_a = project(0, 3 * D_A)
    abuf[PAD_A:PAD_A + tt, :] = z_a[:, 256:512] * z_a[:, 512:768]
    conv_a = _conv_taps(abuf, caw_ref, W_A, PAD_A, tt, D_A, 64, 256)
    yab_ref[:, 0:D_A] = z_a[:, 0:256] * conv_a

    z_b = project(3 * D_A, 3 * D_A + 2 * D_B)
    bbuf[PAD_B:PAD_B + tt, :] = z_b[:, 0:256] * _sigmoid(z_b[:, 256:512])
    conv_b = _conv_taps_realigned(bbuf, bshift, cbw_ref, W_B, PAD_B, tt, 64)
    yab_ref[:, D_A:D_A + D_B] = _layer_norm_silu(conv_b, lng_ref[...], lnb_ref[...])

    qkv0 = 3 * D_A + 2 * D_B
    for part, (out_ref, scale) in enumerate(((q_ref, DK ** -0.5), (k_ref, 1.0), (v_ref, None))):
        c0 = part * D_C
        cbuf[PAD_C:PAD_C + tt, c0:c0 + D_C] = project(qkv0 + c0, qkv0 + c0 + D_C)
        conv = _silu(_conv_taps(cbuf.at[:, c0:c0 + D_C], ccw_ref.at[:, c0:c0 + D_C], W_C, PAD_C, tt, D_C, 64, 512))
        out_ref[...] = conv if scale is None else _l2norm_heads(conv, scale)
    gate_ref[...] = project(qkv0 + 3 * D_C, qkv0 + 4 * D_C)

    z_s = _dot_3pass(h, hb, wsmall_ref[...])
    g, beta = _decay_and_beta(z_s, alog_ref, dtb_ref)
    row_in_chunk = lax.broadcasted_iota(jnp.int32, g.shape, 0) & (CHUNK - 1)
    shift = 1
    while shift < CHUNK:
        g = g + jnp.where(row_in_chunk >= shift, pltpu.roll(g, shift, axis=0), 0.0)
        shift *= 2
    gb_ref[...] = g + beta

    @pl.when(t == nt - 1)
    def _():
        sta_ref[0] = abuf[PAD_A + tt - (W_A - 1):PAD_A + tt, :]
        stb_ref[0] = bbuf[PAD_B + tt - (W_B - 1):PAD_B + tt, :]
        stq_ref[0] = cbuf[PAD_C + tt - (W_C - 1):PAD_C + tt, :]

    abuf[0:PAD_A, :] = abuf[tt:tt + PAD_A, :]
    bbuf[0:PAD_B, :] = bbuf[tt:tt + PAD_B, :]
    cbuf[0:PAD_C, :] = cbuf[tt:tt + PAD_C, :]


def _const_spec(shape):
    nd = len(shape)
    return pl.BlockSpec(shape, lambda *_: (0,) * nd)


def _layer_spec(stacked, layer):
    nd = stacked.ndim - 1
    return pl.BlockSpec((None,) + stacked.shape[1:], lambda *_: (layer,) + (0,) * nd)


def _inproj_prompt(x2d, n_b, t_len, g1, wmain, layer, wsmall, caw, cbw, lng, lnb, ccw, alog, dtb):
    tt = min(TIME_TILE, t_len)
    nt = t_len // tt
    n_tok = n_b * t_len
    d = x2d.shape[1]
    row = lambda w: pl.BlockSpec((tt, w), lambda b, t: (b * nt + t, 0))
    st = lambda r, w: pl.BlockSpec((1, r, w), lambda b, t: (b, 0, 0))
    out_shape = (
        jax.ShapeDtypeStruct((n_tok, D_A + D_B), F32),
        jax.ShapeDtypeStruct((n_tok, D_C), F32),
        jax.ShapeDtypeStruct((n_tok, D_C), F32),
        jax.ShapeDtypeStruct((n_tok, D_C), F32),
        jax.ShapeDtypeStruct((n_tok, D_C), F32),
        jax.ShapeDtypeStruct((n_tok, LANES), F32),
        jax.ShapeDtypeStruct((n_b, W_A - 1, D_A), F32),
        jax.ShapeDtypeStruct((n_b, W_B - 1, D_B), F32),
        jax.ShapeDtypeStruct((n_b, W_C - 1, 3 * D_C), F32),
    )
    return pl.pallas_call(
        functools.partial(_inproj_prompt_kernel, tt=tt, nt=nt),
        grid=(n_b, nt),
        in_specs=[row(d), _const_spec(g1.shape), _layer_spec(wmain, layer), _const_spec(wsmall.shape),
                  _const_spec(caw.shape), _const_spec(cbw.shape), _const_spec(lng.shape), _const_spec(lnb.shape),
                  _const_spec(ccw.shape), _const_spec(alog.shape), _const_spec(dtb.shape)],
        out_specs=(row(D_A + D_B), row(D_C), row(D_C), row(D_C), row(D_C), row(LANES),
                   st(W_A - 1, D_A), st(W_B - 1, D_B), st(W_C - 1, 3 * D_C)),
        out_shape=out_shape,
        scratch_shapes=[pltpu.VMEM((PAD_A + tt, D_A), F32), pltpu.VMEM((PAD_B + tt, D_B), F32),
                        pltpu.VMEM((PAD_C + tt, 3 * D_C), F32), pltpu.VMEM((7, PAD_B + tt - 8, D_B), F32)],
        compiler_params=pltpu.CompilerParams(dimension_semantics=("arbitrary", "arbitrary"),
                                             vmem_limit_bytes=VMEM_LIMIT),
        name="inproj_prompt",
    )(x2d, g1, wmain, wsmall, caw, cbw, lng, lnb, ccw, alog, dtb)


def _dot_nt(a, b):
    return lax.dot_general(a, b, (((1,), (1,)), ((), ())), preferred_element_type=F32)


def _dot_tn(a, b):
    return lax.dot_general(a, b, (((0,), (0,)), ((), ())), preferred_element_type=F32)


def _bdot(a, b):
    return jnp.dot(a.astype(BF16), b.astype(BF16), preferred_element_type=F32)


def _gated_norm(o, on_g, gate):
    return o * lax.rsqrt(jnp.mean(o * o, axis=-1, keepdims=True) + EPS) * on_g * _silu(gate)


def _delta_prompt_kernel(q_ref, k_ref, v_ref, gb_ref, gate_ref, ong_ref, yc_ref, sfin_ref, s_scr, *, tq, nt):
    t = pl.program_id(1)

    @pl.when(t == 0)
    def _():
        s_scr[...] = jnp.zeros(s_scr.shape, F32)

    c = CHUNK
    n_chunks = tq // c
    n_double = c.bit_length() - 2
    inst = [(ch, h) for ch in range(n_chunks) for h in range(H_C)]
    rows = lambda ch: slice(ch * c, (ch + 1) * c)
    cols = lambda h: slice(h * DK, (h + 1) * DK)
    ri = lax.broadcasted_iota(jnp.int32, (c, c), 0)
    ci = lax.broadcasted_iota(jnp.int32, (c, c), 1)
    gb = gb_ref[...]

    q = [q_ref[rows(ch), cols(h)] for ch, h in inst]
    k = [k_ref[rows(ch), cols(h)] for ch, h in inst]
    v = [v_ref[rows(ch), cols(h)] for ch, h in inst]
    g_col = [gb[rows(ch), h:h + 1] for ch, h in inst]
    b_col = [gb[rows(ch), H_C + h:H_C + h + 1] for ch, h in inst]
    g_row = [jnp.sum(jnp.where(ri == ci, g, 0.0), axis=0, keepdims=True) for g in g_col]
    diff = [gc - gr for gc, gr in zip(g_col, g_row)]
    dec_strict = [jnp.exp(jnp.where(ri > ci, d, -jnp.inf)) for d in diff]
    dec_incl = [jnp.exp(jnp.where(ri >= ci, d, -jnp.inf)) for d in diff]
    e_g = [jnp.exp(g) for g in g_col]
    g_last = [g[c - 1:c, :] for g in g_col]

    qkk = [_dot_nt(jnp.concatenate([qi, ki], axis=0).astype(BF16), ki.astype(BF16)) for qi, ki in zip(q, k)]
    qk = [m[:c] * d for m, d in zip(qkk, dec_incl)]
    p = [-(b * m[c:] * d) for b, m, d in zip(b_col, qkk, dec_strict)]
    sol = [jnp.concatenate([vi * b, ki * (b * eg)], axis=1) for vi, ki, b, eg in zip(v, k, b_col, e_g)]
    for level in range(n_double + 1):
        sol = [s + _bdot(pi, s) for pi, s in zip(p, sol)]
        if level < n_double:
            p = [_bdot(pi, pi) for pi in p]
    kd = [ki * jnp.exp(gl - g) for ki, gl, g in zip(k, g_last, g_col)]
    qk_uw = [_bdot(m, s) for m, s in zip(qk, sol)]
    kd_uw = [_dot_tn(x.astype(BF16), s.astype(BF16)) for x, s in zip(kd, sol)]
    lhs = [jnp.concatenate([qi * eg - a[:, DV:], -b[:, DV:]], axis=0).astype(BF16)
           for qi, eg, a, b in zip(q, e_g, qk_uw, kd_uw)]
    decay = [jnp.exp(gl) for gl in g_last]

    state = [s_scr[h] for h in range(H_C)]
    for ch in range(n_chunks):
        base = ch * H_C
        r = [jnp.dot(lhs[base + h], state[h].astype(BF16), preferred_element_type=F32) for h in range(H_C)]
        for h in range(H_C):
            o = r[h][:c] + qk_uw[base + h][:, :DV]
            yc_ref[rows(ch), cols(h)] = _gated_norm(o, ong_ref[...], gate_ref[rows(ch), cols(h)])
        state = [decay[base + h] * state[h] + r[h][c:] + kd_uw[base + h][:, :DV] for h in range(H_C)]
    for h in range(H_C):
        s_scr[h] = state[h]

    @pl.when(t == nt - 1)
    def _():
        sfin_ref[0] = s_scr[...]


def _delta_prompt(q, k, v, gb, gate, ong, n_b, t_len):
    tq = min(TIME_TILE, t_len)
    nt = t_len // tq
    row = lambda w: pl.BlockSpec((tq, w), lambda b, t: (b * nt + t, 0))
    return pl.pallas_call(
        functools.partial(_delta_prompt_kernel, tq=tq, nt=nt),
        grid=(n_b, nt),
        in_specs=[row(D_C), row(D_C), row(D_C), row(LANES), row(D_C), _const_spec(ong.shape)],
        out_specs=(row(D_C), pl.BlockSpec((1, H_C, DK, DV), lambda b, t: (b, 0, 0, 0))),
        out_shape=(jax.ShapeDtypeStruct((n_b * t_len, D_C), F32),
                   jax.ShapeDtypeStruct((n_b, H_C, DK, DV), F32)),
        scratch_shapes=[pltpu.VMEM((H_C, DK, DV), F32)],
        compiler_params=pltpu.CompilerParams(dimension_semantics=("arbitrary", "arbitrary"),
                                             vmem_limit_bytes=VMEM_LIMIT),
        name="delta_prompt",
    )(q, k, v, gb, gate, ong)


def _inproj_sample_kernel(x_ref, g1_ref, wmain_ref, wsmall_ref, caw_ref, cbw_ref, lng_ref, lnb_ref,
                          ccw_ref, alog_ref, dtb_ref, sta_ref, stb_ref, stq_ref,
                          yab_ref, q_ref, k_ref, v_ref, gate_ref, gb_ref, nsta_ref, nstb_ref, nstq_ref):
    z_a, z_b, z_qkv, z_gate, z_s = _in_projection(x_ref[...], g1_ref, wmain_ref, wsmall_ref)

    def conv_step(state_ref, new_state_ref, new, w_ref, width):
        acc = new * w_ref[width - 1:width, :]
        for j in range(width - 1):
            acc = acc + state_ref[j] * w_ref[j:j + 1, :]
        for j in range(width - 2):
            new_state_ref[j] = state_ref[j + 1]
        new_state_ref[width - 2] = new
        return acc

    u_a = z_a[:, 256:512] * z_a[:, 512:768]
    yab_ref[:, 0:D_A] = z_a[:, 0:256] * conv_step(sta_ref, nsta_ref, u_a, caw_ref, W_A)

    u_b = z_b[:, 0:256] * _sigmoid(z_b[:, 256:512])
    yab_ref[:, D_A:D_A + D_B] = _layer_norm_silu(conv_step(stb_ref, nstb_ref, u_b, cbw_ref, W_B),
                                                 lng_ref[...], lnb_ref[...])

    qkv = _silu(conv_step(stq_ref, nstq_ref, z_qkv, ccw_ref, W_C))
    q_ref[...] = _l2norm_heads(qkv[:, 0:D_C], DK ** -0.5)
    k_ref[...] = _l2norm_heads(qkv[:, D_C:2 * D_C], 1.0)
    v_ref[...] = qkv[:, 2 * D_C:3 * D_C]
    gate_ref[...] = z_gate
    g, beta = _decay_and_beta(z_s, alog_ref, dtb_ref)
    gb_ref[...] = g + beta


def _inproj_sample(x2d, g1, wmain, layer, wsmall, caw, cbw, lng, lnb, ccw, alog, dtb, st_a, st_b, st_q):
    n = x2d.shape[0]
    args = (x2d, g1, wmain, wsmall, caw, cbw, lng, lnb, ccw, alog, dtb, st_a, st_b, st_q)
    out_shapes = tuple((n, w) for w in (D_A + D_B, D_C, D_C, D_C, D_C, LANES)) + (st_a.shape, st_b.shape, st_q.shape)
    return pl.pallas_call(
        _inproj_sample_kernel,
        grid=(1,),
        in_specs=[_layer_spec(a, layer) if a is wmain else _const_spec(a.shape) for a in args],
        out_specs=tuple(_const_spec(s) for s in out_shapes),
        out_shape=tuple(jax.ShapeDtypeStruct(s, F32) for s in out_shapes),
        compiler_params=pltpu.CompilerParams(dimension_semantics=("arbitrary",), vmem_limit_bytes=VMEM_LIMIT),
        name="inproj_sample",
    )(*args)


def _delta_sample_kernel(q_ref, k_ref, v_ref, gb_ref, gate_ref, ong_ref, s_ref, *rest, nb, layer):
    if layer:
        earlier_ref, yc_ref, stack_ref = rest
        stack_ref[0:layer] = earlier_ref[...]
    else:
        yc_ref, stack_ref = rest
    snew_ref = stack_ref.at[layer]
    q = q_ref[...]
    k = k_ref[...]
    rows = [k[:, h * DK:(h + 1) * DK] for h in range(H_C)] + [q[:, h * DK:(h + 1) * DK] for h in range(H_C)]
    rows.append(jnp.zeros((LANES - 2 * H_C * nb, DK), F32))
    kq_t = jnp.concatenate(rows, axis=0).T
    gb = gb_ref[...]
    inst = [(i, h) for i in range(nb) for h in range(H_C)]
    cols = lambda h: slice(h * DK, (h + 1) * DK)
    k_bc = [jnp.broadcast_to(kq_t[:, h * nb + i:h * nb + i + 1], (DK, DV)) for i, h in inst]
    q_bc = [jnp.broadcast_to(kq_t[:, (H_C + h) * nb + i:(H_C + h) * nb + i + 1], (DK, DV)) for i, h in inst]
    k_s = [jnp.sum(s_ref[i, h] * kb, axis=0, keepdims=True) for (i, h), kb in zip(inst, k_bc)]
    q_s = [jnp.sum(s_ref[i, h] * qb, axis=0, keepdims=True) for (i, h), qb in zip(inst, q_bc)]
    e_g = [jnp.exp(gb[i:i + 1, h:h + 1]) for i, h in inst]
    v_new = [gb[i:i + 1, H_C + h:H_C + h + 1] * (v_ref[i:i + 1, cols(h)] - eg * ks)
             for (i, h), eg, ks in zip(inst, e_g, k_s)]
    qk = [jnp.sum(q[i:i + 1, cols(h)] * k[i:i + 1, cols(h)], axis=-1, keepdims=True) for i, h in inst]
    for n, (i, h) in enumerate(inst):
        snew_ref[i, h] = e_g[n] * s_ref[i, h] + k_bc[n] * v_new[n]
    for n, (i, h) in enumerate(inst):
        o = e_g[n] * q_s[n] + qk[n] * v_new[n]
        yc_ref[i:i + 1, cols(h)] = _gated_norm(o, ong_ref[...], gate_ref[i:i + 1, cols(h)])


def _delta_sample(q, k, v, gb, gate, ong, state, layer, earlier):
    n = q.shape[0]
    nb = 8
    row = lambda w: pl.BlockSpec((nb, w), lambda i: (i, 0))
    st_in = pl.BlockSpec((None, nb, H_C, DK, DV), lambda i: (layer, i, 0, 0, 0))
    stack = lambda depth: pl.BlockSpec((depth, nb, H_C, DK, DV), lambda i: (0, i, 0, 0, 0))
    carried = [] if earlier is None else [earlier]
    return pl.pallas_call(
        functools.partial(_delta_sample_kernel, nb=nb, layer=layer),
        grid=(n // nb,),
        in_specs=[row(D_C), row(D_C), row(D_C), row(LANES), row(D_C), _const_spec(ong.shape), st_in]
                 + [stack(layer) for _ in carried],
        out_specs=(row(D_C), stack(layer + 1)),
        out_shape=(jax.ShapeDtypeStruct((n, D_C), F32),
                   jax.ShapeDtypeStruct((layer + 1,) + state.shape[1:], F32)),
        compiler_params=pltpu.CompilerParams(dimension_semantics=("arbitrary",), vmem_limit_bytes=VMEM_LIMIT),
        name="delta_sample",
    )(q, k, v, gb, gate, ong, state, *carried)


def _outproj_router_kernel(x_ref, yab_ref, yc_ref, xs_ref, yabs_ref, ycs_ref, woa_ref, woc_ref, g2_ref, wrt_ref,
                           x1_ref, h2_ref, route_ref, routet_ref, cnt_ref, *, n_prompt_tiles):
    is_prompt = pl.program_id(0) < n_prompt_tiles
    tm = x_ref.shape[0]

    def pick(prompt_ref, sample_ref):
        sample = sample_ref[...]
        sample = jnp.concatenate([sample, jnp.zeros((tm - sample.shape[0], sample.shape[1]), F32)], axis=0)
        return jnp.where(is_prompt, prompt_ref[...], sample)

    x1 = (pick(x_ref, xs_ref)
          + jnp.dot(pick(yab_ref, yabs_ref).astype(BF16), woa_ref[...], preferred_element_type=F32)
          + jnp.dot(pick(yc_ref, ycs_ref).astype(BF16), woc_ref[...], preferred_element_type=F32))
    x1_ref[...] = x1
    h2 = _rms(x1, g2_ref[...])
    h2_ref[...] = h2.astype(BF16)
    logits = _dot_3pass(h2, h2.astype(BF16), wrt_ref[...]).T[0:ROUTER_ROWS, :]
    row = lax.broadcasted_iota(jnp.int32, logits.shape, 0)
    neg = -jnp.inf
    gl = jnp.where(row < N_GROUPS, logits, neg)
    g_max = jnp.max(gl, axis=0, keepdims=True)
    g_idx = jnp.min(jnp.where(gl == g_max, row, LANES), axis=0, keepdims=True)
    g_p = 1.0 / jnp.sum(jnp.exp(gl - g_max), axis=0, keepdims=True)
    lo = EXPERT_LANE0 + g_idx * EXP_PER_GROUP
    in_group = (row >= lo) & (row < lo + EXP_PER_GROUP)
    el = jnp.where(in_group, logits, neg)
    e_max = jnp.max(el, axis=0, keepdims=True)
    pe = jnp.exp(el - e_max)
    e_prob = pe / jnp.sum(pe, axis=0, keepdims=True)
    p1 = jnp.max(e_prob, axis=0, keepdims=True)
    i1 = jnp.min(jnp.where(e_prob == p1, row, LANES), axis=0, keepdims=True)
    rest = jnp.where(in_group & (row != i1), e_prob, -1.0)
    p2 = jnp.max(rest, axis=0, keepdims=True)
    i2 = jnp.min(jnp.where(rest == p2, row, LANES), axis=0, keepdims=True)
    denom = p1 + p2
    w1 = g_p * (p1 / denom)
    w2 = g_p * (p2 / denom)
    e1 = i1 - EXPERT_LANE0
    e2 = i2 - EXPERT_LANE0
    oh1 = row == e1
    oh2 = row == e2
    picked = jnp.where(oh1 | oh2, 1.0, 0.0).astype(BF16)
    cnt = _dot_nt(jnp.ones((8, tm), BF16), picked)
    cnt_ref[0] = jnp.concatenate([cnt, jnp.zeros((8, LANES - ROUTER_ROWS), F32)], axis=1)
    ki = lax.broadcasted_iota(jnp.int32, (ROUTER_ROWS, ROUTER_ROWS), 0)
    kj = lax.broadcasted_iota(jnp.int32, (ROUTER_ROWS, ROUTER_ROWS), 1)
    cnt_col = jnp.sum(picked.astype(F32), axis=1, keepdims=True)
    pieces = jnp.broadcast_to(jnp.ceil(cnt_col * (1.0 / PIECE)), (ROUTER_ROWS, LANES)).astype(BF16)
    run_start = PIECE * jnp.dot(jnp.where(kj < ki, 1.0, 0.0).astype(BF16), pieces,
                                preferred_element_type=F32)[:, 0:1]
    ri = lax.broadcasted_iota(jnp.int32, (tm, tm), 0)
    ci = lax.broadcasted_iota(jnp.int32, (tm, tm), 1)
    base = jnp.dot(picked, jnp.where(ri < ci, 1.0, 0.0).astype(BF16), preferred_element_type=F32) + run_start
    pos1 = jnp.sum(jnp.where(oh1, base, 0.0), axis=0, keepdims=True)
    pos2 = jnp.sum(jnp.where(oh2, base, 0.0), axis=0, keepdims=True)
    routet = jnp.concatenate([e1.astype(F32), e2.astype(F32), w1, w2, pos1, pos2, jnp.zeros((2, tm), F32)], axis=0)
    routet_ref[0] = routet
    route_ref[...] = jnp.concatenate([routet, jnp.zeros((LANES - 8, tm), F32)], axis=0).T


def _outproj_router(prompt, sample, woa, woc, g2, wrt):
    x2d = prompt[0]
    n, d = x2d.shape
    tm = min(TOKEN_TILE, n)
    n_prompt_tiles = n // tm
    n_all = n + tm
    assert sample[0].shape[0] <= tm and sample[0].shape[0] % 8 == 0
    p_row = lambda w: pl.BlockSpec((tm, w), lambda i: (jnp.minimum(i, n_prompt_tiles - 1), 0))
    row = lambda w: pl.BlockSpec((tm, w), lambda i: (i, 0))
    return pl.pallas_call(
        functools.partial(_outproj_router_kernel, n_prompt_tiles=n_prompt_tiles),
        grid=(n_prompt_tiles + 1,),
        in_specs=[p_row(d), p_row(D_A + D_B), p_row(D_C)] + [_const_spec(a.shape) for a in sample]
                 + [_const_spec(woa.shape), _const_spec(woc.shape), _const_spec(g2.shape), _const_spec(wrt.shape)],
        out_specs=(row(d), row(d), row(LANES), pl.BlockSpec((1, 8, tm), lambda i: (i, 0, 0)),
                   pl.BlockSpec((1, 8, LANES), lambda i: (i, 0, 0))),
        out_shape=(jax.ShapeDtypeStruct((n_all, d), F32), jax.ShapeDtypeStruct((n_all, d), BF16),
                   jax.ShapeDtypeStruct((n_all, LANES), F32),
                   jax.ShapeDtypeStruct((n_all // tm, 8, tm), F32),
                   jax.ShapeDtypeStruct((n_all // tm, 8, LANES), F32)),
        compiler_params=pltpu.CompilerParams(dimension_semantics=("arbitrary",), vmem_limit_bytes=VMEM_LIMIT),
        name="outproj_router",
    )(*prompt, *sample, woa, woc, g2, wrt)


def _route_plan(cnt, n_tok, tt):
    n_tiles = n_tok // tt
    cnt = cnt[:, 0, :N_EXPERTS].astype(jnp.int32)
    pc = (cnt + PIECE - 1) // PIECE * PIECE
    local_end = jnp.cumsum(pc, axis=1)
    local_off = local_end - pc
    seg_len = jnp.sum(pc, axis=0)
    seg_pad = (seg_len + ROW_TILE - 1) // ROW_TILE * ROW_TILE
    seg_end = jnp.cumsum(seg_pad)
    seg_start = seg_end - seg_pad
    run_off = seg_start[None, :] + jnp.cumsum(pc, axis=0) - pc
    piece_row = jnp.arange(_local_rows(tt) // PIECE, dtype=jnp.int32) * PIECE
    in_run = ((local_off[:, None, :] <= piece_row[None, :, None])
              & (piece_row[None, :, None] < local_end[:, None, :]))
    piece_dst = jnp.sum(jnp.where(in_run, (run_off - local_off)[:, None, :], 0), axis=-1) + piece_row[None, :]
    n_row_tiles = _max_sorted_rows(n_tok, tt) // ROW_TILE
    tile_row0 = jnp.arange(n_row_tiles, dtype=jnp.int32) * ROW_TILE
    tile_expert = jnp.minimum(jnp.sum((seg_end[None, :] <= tile_row0[:, None]).astype(jnp.int32), axis=1),
                              N_EXPERTS - 1)
    used = seg_pad > 0
    order = jnp.cumsum(used.astype(jnp.int32)) - used.astype(jnp.int32)
    ids = jnp.arange(N_EXPERTS, dtype=jnp.int32)
    later = jnp.where(used[None, :] & (ids[None, :] > ids[:, None]), ids[None, :], N_EXPERTS)
    next_used = jnp.min(later, axis=1)
    return dict(
        piece_dst=piece_dst.reshape(-1), tile_pieces=jnp.sum(pc, axis=1) // PIECE,
        fill_off=seg_start + seg_len, fill_pieces=(seg_pad - seg_len) // PIECE,
        tile_expert=tile_expert, n_active=(seg_end[-1] // ROW_TILE).reshape(1),
        expert_order=order, next_used=next_used,
    )


def _max_sorted_rows(n_tok, tt):
    n_tiles = n_tok // tt
    rows = 2 * n_tok + n_tiles * N_EXPERTS * (PIECE - 1) + N_EXPERTS * (ROW_TILE - 1)
    return (rows + ROW_TILE - 1) // ROW_TILE * ROW_TILE


def _local_rows(tt):
    return 2 * tt + N_EXPERTS * PIECE


def _piece_copy(src, dst, sem):
    return pltpu.make_async_copy(src, dst, sem)


def _for_each_piece(n, body):
    def four(q, c):
        for u in range(4):
            body(q * 4 + u)
        return c
    lax.fori_loop(0, lax.shift_right_logical(n, 2), four, 0)

    def one(p, c):
        body(p)
        return c
    lax.fori_loop(n & ~3, n, one, 0)


def _wait_pieces(n, copy_of_rows):
    for bit in (64, 32, 16, 8, 4, 2, 1):
        @pl.when((n & bit) != 0)
        def _(bit=bit):
            copy_of_rows(bit * PIECE).wait()


def _dispatch_kernel(piece_dst, tile_pieces, fill_off, fill_pieces, n_active,
                     h2_ref, routet_ref, xs_hbm, lbuf, zbuf, sem, *, tt, n_tiles, n_row_tiles):
    i = pl.program_id(0)
    slot = i % 2
    lrows = _local_rows(tt)
    max_pieces = lrows // PIECE

    assert max_pieces < 128

    def wait_tile(tile, s):
        _wait_pieces(tile_pieces[tile],
                     lambda rows: _piece_copy(lbuf.at[s, pl.ds(0, rows)], xs_hbm.at[pl.ds(0, rows)], sem.at[s]))

    @pl.when(i >= 2)
    def _():
        wait_tile(i - 2, slot)

    pos = routet_ref[0]
    h2 = h2_ref[...]
    for r0 in range(0, lrows, SORT_CHUNK):
        row = (lax.broadcasted_iota(jnp.int32, (SORT_CHUNK, tt), 0) + r0).astype(F32)
        perm = jnp.where((row == pos[4:5, :]) | (row == pos[5:6, :]), 1.0, 0.0).astype(BF16)
        lbuf[slot, r0:r0 + SORT_CHUNK] = jnp.dot(perm, h2, preferred_element_type=F32).astype(BF16)

    def send(p):
        dst = piece_dst[i * max_pieces + p]
        _piece_copy(lbuf.at[slot, pl.ds(pl.multiple_of(p * PIECE, PIECE), PIECE)],
                    xs_hbm.at[pl.ds(pl.multiple_of(dst, PIECE), PIECE)], sem.at[slot]).start()
    _for_each_piece(tile_pieces[i], send)

    @pl.when(i == n_tiles - 1)
    def _():
        zbuf[...] = jnp.zeros(zbuf.shape, BF16)
        zpiece = zbuf.at[pl.ds(0, PIECE)]
        for e in range(N_EXPERTS):
            def body(p, c, e=e):
                _piece_copy(zpiece, xs_hbm.at[pl.ds(pl.multiple_of(fill_off[e] + p * PIECE, PIECE), PIECE)],
                            sem.at[2]).start()
                return c
            lax.fori_loop(0, fill_pieces[e], body, 0)

        def tail_body(j, c):
            _piece_copy(zbuf, xs_hbm.at[pl.ds(pl.multiple_of(j * ROW_TILE, ROW_TILE), ROW_TILE)], sem.at[3]).start()
            return c
        lax.fori_loop(n_active[0], n_row_tiles, tail_body, 0)
        for e in range(N_EXPERTS):
            def body(p, c):
                _piece_copy(zpiece, xs_hbm.at[pl.ds(0, PIECE)], sem.at[2]).wait()
                return c
            lax.fori_loop(0, fill_pieces[e], body, 0)

        def tail_wait(j, c):
            _piece_copy(zbuf, xs_hbm.at[pl.ds(0, ROW_TILE)], sem.at[3]).wait()
            return c
        lax.fori_loop(n_active[0], n_row_tiles, tail_wait, 0)
        if n_tiles > 1:
            wait_tile(i - 1, 1 - slot)
        wait_tile(i, slot)


def _dispatch(h2, routet, plan, tt):
    n, d = h2.shape
    n_tiles = n // tt
    n_rows = _max_sorted_rows(n, tt)
    kernel_fn = functools.partial(_dispatch_kernel, tt=tt, n_tiles=n_tiles, n_row_tiles=n_rows // ROW_TILE)
    grid_spec = pltpu.PrefetchScalarGridSpec(
        num_scalar_prefetch=5,
        grid=(n_tiles,),
        in_specs=[pl.BlockSpec((tt, d), lambda i, *_: (i, 0)),
                  pl.BlockSpec((1, 8, tt), lambda i, *_: (i, 0, 0))],
        out_specs=pl.BlockSpec(memory_space=pl.ANY),
        scratch_shapes=[pltpu.VMEM((2, _local_rows(tt), d), BF16), pltpu.VMEM((ROW_TILE, d), BF16),
                        pltpu.SemaphoreType.DMA((4,))],
    )
    return pl.pallas_call(
        kernel_fn,
        grid_spec=grid_spec,
        out_shape=jax.ShapeDtypeStruct((n_rows, d), BF16),
        compiler_params=pltpu.CompilerParams(dimension_semantics=("arbitrary",), vmem_limit_bytes=VMEM_LIMIT),
        name="moe_dispatch",
    )(plan["piece_dst"], plan["tile_pieces"], plan["fill_off"], plan["fill_pieces"], plan["n_active"],
      h2, routet)


def _experts_kernel(tile_expert, n_active, expert_order, next_used,
                    xs_ref, wg_hbm, wu_hbm, wd_hbm, ys_ref, wbuf_g, wbuf_u, wbuf_d, wg_b, wu_b, wd_b, sem,
                    *, layer):
    j = pl.program_id(0)
    expert = tile_expert[j]
    previous = tile_expert[jnp.maximum(j - 1, 0)]
    active = j < n_active[0]
    half = expert_order[expert] % 2

    def weight_copies(e, s):
        return (_piece_copy(wg_hbm.at[layer, e], wbuf_g.at[s], sem.at[s, 0]),
                _piece_copy(wu_hbm.at[layer, e], wbuf_u.at[s], sem.at[s, 1]),
                _piece_copy(wd_hbm.at[layer, e], wbuf_d.at[s], sem.at[s, 2]))

    @pl.when(j == 0)
    def _():
        for c in weight_copies(expert, half):
            c.start()

    @pl.when(active & ((j == 0) | (expert != previous)))
    def _():
        for c in weight_copies(expert, half):
            c.wait()
        wg_b[...] = wbuf_g[half].astype(BF16)
        wu_b[...] = wbuf_u[half].astype(BF16)
        wd_b[...] = wbuf_d[half].astype(BF16)
        upcoming = next_used[expert]

        @pl.when(upcoming < N_EXPERTS)
        def _():
            for c in weight_copies(upcoming, 1 - half):
                c.start()

    @pl.when(active)
    def _():
        x = xs_ref[...]
        gate = jnp.dot(x, wg_b[...], preferred_element_type=F32)
        up = jnp.dot(x, wu_b[...], preferred_element_type=F32)
        act = (_silu(gate) * up).astype(BF16)
        ys_ref[...] = jnp.dot(act, wd_b[...], preferred_element_type=F32).astype(BF16)

    @pl.when(jnp.logical_not(active))
    def _():
        ys_ref[...] = jnp.zeros(ys_ref.shape, BF16)


def _experts(xs, wg, wu, wd, layer, plan):
    rows, d = xs.shape
    d_e = wg.shape[-1]
    live = lambda j, te, na, *_: (jnp.minimum(j, na[0] - 1), 0)
    grid_spec = pltpu.PrefetchScalarGridSpec(
        num_scalar_prefetch=4,
        grid=(rows // ROW_TILE,),
        in_specs=[pl.BlockSpec((ROW_TILE, d), live),
                  pl.BlockSpec(memory_space=pl.ANY), pl.BlockSpec(memory_space=pl.ANY),
                  pl.BlockSpec(memory_space=pl.ANY)],
        out_specs=pl.BlockSpec((ROW_TILE, d), lambda j, *_: (j, 0)),
        scratch_shapes=[pltpu.VMEM((2, d, d_e), F32), pltpu.VMEM((2, d, d_e), F32), pltpu.VMEM((2, d_e, d), F32),
                        pltpu.VMEM((d, d_e), BF16), pltpu.VMEM((d, d_e), BF16), pltpu.VMEM((d_e, d), BF16),
                        pltpu.SemaphoreType.DMA((2, 3))],
    )
    return pl.pallas_call(
        functools.partial(_experts_kernel, layer=layer),
        grid_spec=grid_spec,
        out_shape=jax.ShapeDtypeStruct((rows, d), BF16),
        compiler_params=pltpu.CompilerParams(dimension_semantics=("arbitrary",), vmem_limit_bytes=VMEM_LIMIT),
        name="moe_experts",
    )(plan["tile_expert"], plan["n_active"], plan["expert_order"], plan["next_used"], xs, wg, wu, wd)


def _combine_kernel(piece_src, tile_pieces, x1_ref, route_ref, fg_ref, ys_hbm, out_ref, outs_ref, lbuf, sem,
                    *, tt, n_tiles, final_norm):
    i = pl.program_id(0)
    slot = i % 2
    lrows = _local_rows(tt)
    max_pieces = lrows // PIECE

    assert max_pieces < 128

    def fetch_tile(tile, s):
        def fetch(p):
            src = piece_src[tile * max_pieces + p]
            _piece_copy(ys_hbm.at[pl.ds(pl.multiple_of(src, PIECE), PIECE)],
                        lbuf.at[s, pl.ds(pl.multiple_of(p * PIECE, PIECE), PIECE)], sem.at[s]).start()
        _for_each_piece(tile_pieces[tile], fetch)

    @pl.when(i == 0)
    def _():
        lbuf[...] = jnp.zeros(lbuf.shape, BF16)
        fetch_tile(0, 0)

    @pl.when(i + 1 < n_tiles)
    def _():
        fetch_tile(i + 1, 1 - slot)

    _wait_pieces(tile_pieces[i], lambda rows: _piece_copy(ys_hbm.at[pl.ds(0, rows)],
                                                         lbuf.at[slot, pl.ds(0, rows)], sem.at[slot]))

    route = route_ref[...]
    out = x1_ref[...]
    for r0 in range(0, lrows, SORT_CHUNK):
        col = (lax.broadcasted_iota(jnp.int32, (tt, SORT_CHUNK), 1) + r0).astype(F32)
        weights = (jnp.where(col == route[:, 4:5], route[:, 2:3], 0.0)
                   + jnp.where(col == route[:, 5:6], route[:, 3:4], 0.0)).astype(BF16)
        out = out + jnp.dot(weights, lbuf[slot, r0:r0 + SORT_CHUNK], preferred_element_type=F32)
    out = _rms(out, fg_ref[...]) if final_norm else out

    @pl.when(i < n_tiles - 1)
    def _():
        out_ref[...] = out

    @pl.when(i == n_tiles - 1)
    def _():
        outs_ref[...] = out[0:outs_ref.shape[0]]


def _combine(x1, route, ys, fg, plan, tt, n_sample, final_norm):
    n, d = x1.shape
    n_tiles = n // tt
    kernel_fn = functools.partial(_combine_kernel, tt=tt, n_tiles=n_tiles, final_norm=final_norm)
    grid_spec = pltpu.PrefetchScalarGridSpec(
        num_scalar_prefetch=2,
        grid=(n_tiles,),
        in_specs=[pl.BlockSpec((tt, d), lambda i, *_: (i, 0)),
                  pl.BlockSpec((tt, LANES), lambda i, *_: (i, 0)),
                  pl.BlockSpec(fg.shape, lambda i, *_: (0, 0)),
                  pl.BlockSpec(memory_space=pl.ANY)],
        out_specs=(pl.BlockSpec((tt, d), lambda i, *_: (jnp.minimum(i, n_tiles - 2), 0)),
                   pl.BlockSpec((n_sample, d), lambda i, *_: (0, 0))),
        scratch_shapes=[pltpu.VMEM((2, _local_rows(tt), d), BF16), pltpu.SemaphoreType.DMA((2,))],
    )
    return pl.pallas_call(
        kernel_fn,
        grid_spec=grid_spec,
        out_shape=(jax.ShapeDtypeStruct((n - tt, d), F32), jax.ShapeDtypeStruct((n_sample, d), F32)),
        compiler_params=pltpu.CompilerParams(dimension_semantics=("arbitrary",), vmem_limit_bytes=VMEM_LIMIT),
        name="moe_combine",
    )(plan["piece_dst"], plan["tile_pieces"], x1, route, fg, ys)


def _routed_moe(x1, h2, route, routet, cnt, wg, wu, wd, layer, fg, n_sample, final_norm):
    n = x1.shape[0]
    tt = n // cnt.shape[0]
    plan = _route_plan(cnt, n, tt)
    xs = _dispatch(h2, routet, plan, tt)
    ys = _experts(xs, wg, wu, wd, layer, plan)
    return _combine(x1, route, ys, fg, plan, tt, n_sample, final_norm)


def _pad_lanes(a):
    a = a.reshape((1, -1)) if a.ndim == 1 else a
    return jnp.pad(a, ((0, 0), (0, LANES - a.shape[1])))


def kernel(x_prompt, x_sample, state_conv_a, state_conv_b, state_conv_qkv, state_delta, norm1_g, w_in, conv_a_w,
           conv_b_w, ln_b_g, ln_b_b, conv_c_w, a_log, dt_bias, o_norm_g, w_out, norm2_g, w_group, w_router,
           w_gate, w_up, w_down, final_g):
    n_b, t_len, d = x_prompt.shape
    n_s = x_sample.shape[0]
    depth = w_in.shape[0]
    n_main = 3 * D_A + 2 * D_B + 3 * D_C + D_C
    xp = x_prompt.reshape(n_b * t_len, d)
    xs = x_sample.reshape(n_s, d)
    fg = final_g.reshape(1, d)
    outs = {k: [] for k in ("pa", "pb", "pq", "pd", "sa", "sb", "sq")}
    sample_states = None
    wmain = w_in.astype(BF16)
    for l in range(depth):
        g1 = norm1_g[l].reshape(1, d)
        wsmall = _pad_lanes(jnp.concatenate([w_in[l, :, n_main + H_C:], w_in[l, :, n_main:n_main + H_C]], axis=1))
        caw, cbw, ccw = conv_a_w[l], conv_b_w[l], conv_c_w[l]
        lng, lnb = ln_b_g[l].reshape(1, D_B), ln_b_b[l].reshape(1, D_B)
        alog, dtb = _pad_lanes(a_log[l]), _pad_lanes(dt_bias[l])
        ong = o_norm_g[l].reshape(1, DV)
        woa = w_out[l][:D_A + D_B].astype(BF16)
        woc = w_out[l][D_A + D_B:].astype(BF16)
        g2 = norm2_g[l].reshape(1, d)
        wrt = _pad_lanes(jnp.concatenate(
            [w_group[l], w_router[l].transpose(1, 0, 2).reshape(d, N_EXPERTS)], axis=1))
        last = l == depth - 1

        yab, q, k, v, gate, gb, st_a, st_b, st_q = _inproj_prompt(
            xp, n_b, t_len, g1, wmain, l, wsmall, caw, cbw, lng, lnb, ccw, alog, dtb)
        yc, s_fin = _delta_prompt(q, k, v, gb, gate, ong, n_b, t_len)
        outs["pa"].append(st_a); outs["pb"].append(st_b); outs["pq"].append(st_q); outs["pd"].append(s_fin)

        hist_a = state_conv_a[l].transpose(1, 0, 2)
        hist_b = state_conv_b[l].transpose(1, 0, 2)
        hist_q = state_conv_qkv[l].transpose(1, 0, 2)
        yab_s, q, k, v, gate, gb, hist_a, hist_b, hist_q = _inproj_sample(
            xs, g1, wmain, l, wsmall, caw, cbw, lng, lnb, ccw, alog, dtb, hist_a, hist_b, hist_q)
        yc_s, sample_states = _delta_sample(q, k, v, gb, gate, ong, state_delta, l, sample_states)

        x1, h2, route, routet, cnt = _outproj_router((xp, yab, yc), (xs, yab_s, yc_s), woa, woc, g2, wrt)
        xp, xs = _routed_moe(x1, h2, route, routet, cnt, w_gate, w_up, w_down, l, fg, n_s, last)
        outs["sa"].append(hist_a.transpose(1, 0, 2))
        outs["sb"].append(hist_b.transpose(1, 0, 2))
        outs["sq"].append(hist_q.transpose(1, 0, 2))

    stack = lambda key: jnp.stack(outs[key])
    return (xp.reshape(n_b, t_len, d), xs.reshape(n_s, 1, d),
            stack("pa"), stack("pb"), stack("pq"), stack("pd"),
            stack("sa"), stack("sb"), stack("sq"), sample_states)
```

```python
import functools

import jax
import jax.numpy as jnp
from jax import lax
from jax.experimental import pallas as pl
from jax.experimental.pallas import tpu as pltpu

F32 = jnp.float32
BF16 = jnp.bfloat16

EPS = 1e-6
H_C = 4
DK = 128
DV = 128
D_A = 256
D_B = 256
D_C = 512
W_A = 3
W_B = 31
W_C = 4
CHUNK = 64
N_GROUPS = 4
EXP_PER_GROUP = 8
N_EXPERTS = N_GROUPS * EXP_PER_GROUP
LANES = 128
PAD_A = 8
PAD_B = 32
PAD_C = 8
EXPERT_LANE0 = N_GROUPS
ROUTER_ROWS = (N_GROUPS + N_EXPERTS + 7) // 8 * 8
VMEM_LIMIT = 56 * 1024 * 1024
TIME_TILE = 512
TOKEN_TILE = 512
PIECE = 16
ROW_TILE = 512
SORT_CHUNK = 256


def _silu(x):
    return x * (1.0 / (1.0 + jnp.exp(-x)))


def _sigmoid(x):
    return 1.0 / (1.0 + jnp.exp(-x))


def _softplus(x):
    return jnp.maximum(x, 0.0) + jnp.log1p(jnp.exp(-jnp.abs(x)))


def _rms(x, g):
    return x * lax.rsqrt(jnp.mean(x * x, axis=-1, keepdims=True) + EPS) * g


def _conv_taps(buf, w_ref, width, pad, n_rows, n_cols, row_blk, col_blk):
    row_parts = []
    for r0 in range(0, n_rows, row_blk):
        col_parts = []
        for c0 in range(0, n_cols, col_blk):
            acc = None
            for j in range(width):
                start = pad - (width - 1) + j + r0
                term = buf[start:start + row_blk, c0:c0 + col_blk] * w_ref[j:j + 1, c0:c0 + col_blk]
                acc = term if acc is None else acc + term
            col_parts.append(acc)
        row_parts.append(col_parts[0] if len(col_parts) == 1 else jnp.concatenate(col_parts, axis=1))
    return row_parts[0] if len(row_parts) == 1 else jnp.concatenate(row_parts, axis=0)


def _conv_taps_realigned(buf, shifted, w_ref, width, pad, n_rows, row_blk):
    first = pad - (width - 1)
    n_shift = n_rows + (first + width - 1) // 8 * 8 - 8
    for r in range(1, 8):
        shifted[r - 1] = buf[r:r + n_shift, :]
    row_parts = []
    for r0 in range(0, n_rows, row_blk):
        acc = None
        for j in range(width):
            q, r = divmod(first + j, 8)
            lo = 8 * q + r0
            window = buf[lo:lo + row_blk, :] if r == 0 else shifted[r - 1, lo:lo + row_blk, :]
            term = window * w_ref[j:j + 1, :]
            acc = term if acc is None else acc + term
        row_parts.append(acc)
    return row_parts[0] if len(row_parts) == 1 else jnp.concatenate(row_parts, axis=0)


def _l2norm_heads(x, scale):
    parts = []
    for h in range(H_C):
        xh = x[:, h * DK:(h + 1) * DK]
        parts.append(xh * (lax.rsqrt(jnp.sum(xh * xh, axis=-1, keepdims=True) + 1e-6) * scale))
    return jnp.concatenate(parts, axis=1)


def _dot_3pass(a, a_hi, w):
    a_lo = (a - a_hi.astype(F32)).astype(BF16)
    w_hi = w.astype(BF16)
    w_lo = (w - w_hi.astype(F32)).astype(BF16)
    n = w.shape[1]
    both = jnp.dot(a_hi, jnp.concatenate([w_hi, w_lo], axis=1), preferred_element_type=F32)
    return both[:, :n] + both[:, n:] + jnp.dot(a_lo, w_hi, preferred_element_type=F32)


def _in_projection(x, g1_ref, wmain_ref, wsmall_ref):
    h = _rms(x, g1_ref[...])
    hb = h.astype(BF16)
    z_a = jnp.dot(hb, wmain_ref[:, 0:768], preferred_element_type=F32)
    z_b = jnp.dot(hb, wmain_ref[:, 768:1280], preferred_element_type=F32)
    z_qkv = jnp.dot(hb, wmain_ref[:, 1280:2816], preferred_element_type=F32)
    z_gate = jnp.dot(hb, wmain_ref[:, 2816:3328], preferred_element_type=F32)
    z_s = _dot_3pass(h, hb, wsmall_ref[...])
    return z_a, z_b, z_qkv, z_gate, z_s


def _layer_norm_silu(x, g, b):
    mu = jnp.mean(x, axis=-1, keepdims=True)
    xc = x - mu
    y = xc * lax.rsqrt(jnp.mean(xc * xc, axis=-1, keepdims=True) + EPS)
    return _silu(y * g + b)


def _decay_and_beta(z_s, alog_ref, dtb_ref):
    lane = lax.broadcasted_iota(jnp.int32, z_s.shape, 1)
    g = -jnp.exp(alog_ref[...]) * _softplus(z_s + dtb_ref[...])
    g = jnp.where(lane < H_C, g, 0.0)
    beta = jnp.where((lane >= H_C) & (lane < 2 * H_C), _sigmoid(z_s), 0.0)
    return g, beta


def _inproj_prompt_kernel(x_ref, g1_ref, wmain_ref, wsmall_ref, caw_ref, cbw_ref, lng_ref, lnb_ref,
                          ccw_ref, alog_ref, dtb_ref,
                          yab_ref, q_ref, k_ref, v_ref, gate_ref, gb_ref, sta_ref, stb_ref, stq_ref,
                          abuf, bbuf, cbuf, bshift, *, tt, nt):
    t = pl.program_id(1)

    @pl.when(t == 0)
    def _():
        abuf[0:PAD_A, :] = jnp.zeros((PAD_A, D_A), F32)
        bbuf[0:PAD_B, :] = jnp.zeros((PAD_B, D_B), F32)
        cbuf[0:PAD_C, :] = jnp.zeros((PAD_C, 3 * D_C), F32)

    h = _rms(x_ref[...], g1_ref[...])
    hb = h.astype(BF16)
    project = lambda lo, hi: jnp.dot(hb, wmain_ref[:, lo:hi], preferred_element_type=F32)

    z_a = project(0, 3 * D_A)
    abuf[PAD_A:PAD_A + tt, :] = z_a[:, 256:512] * z_a[:, 512:768]
    conv_a = _conv_taps(abuf, caw_ref, W_A, PAD_A, tt, D_A, 64, 256)
    yab_ref[:, 0:D_A] = z_a[:, 0:256] * conv_a

    z_b = project(3 * D_A, 3 * D_A + 2 * D_B)
    bbuf[PAD_B:PAD_B + tt, :] = z_b[:, 0:256] * _sigmoid(z_b[:, 256:512])
    conv_b = _conv_taps_realigned(bbuf, bshift, cbw_ref, W_B, PAD_B, tt, 64)
    yab_ref[:, D_A:D_A + D_B] = _layer_norm_silu(conv_b, lng_ref[...], lnb_ref[...])

    qkv0 = 3 * D_A + 2 * D_B
    for part, (out_ref, scale) in enumerate(((q_ref, DK ** -0.5), (k_ref, 1.0), (v_ref, None))):
        c0 = part * D_C
        cbuf[PAD_C:PAD_C + tt, c0:c0 + D_C] = project(qkv0 + c0, qkv0 + c0 + D_C)
        conv = _silu(_conv_taps(cbuf.at[:, c0:c0 + D_C], ccw_ref.at[:, c0:c0 + D_C], W_C, PAD_C, tt, D_C, 64, 512))
        out_ref[...] = conv if scale is None else _l2norm_heads(conv, scale)
    gate_ref[...] = project(qkv0 + 3 * D_C, qkv0 + 4 * D_C)

    z_s = _dot_3pass(h, hb, wsmall_ref[...])
    g, beta = _decay_and_beta(z_s, alog_ref, dtb_ref)
    row_in_chunk = lax.broadcasted_iota(jnp.int32, g.shape, 0) & (CHUNK - 1)
    shift = 1
    while shift < CHUNK:
        g = g + jnp.where(row_in_chunk >= shift, pltpu.roll(g, shift, axis=0), 0.0)
        shift *= 2
    gb_ref[...] = g + beta

    @pl.when(t == nt - 1)
    def _():
        sta_ref[0] = abuf[PAD_A + tt - (W_A - 1):PAD_A + tt, :]
        stb_ref[0] = bbuf[PAD_B + tt - (W_B - 1):PAD_B + tt, :]
        stq_ref[0] = cbuf[PAD_C + tt - (W_C - 1):PAD_C + tt, :]

    abuf[0:PAD_A, :] = abuf[tt:tt + PAD_A, :]
    bbuf[0:PAD_B, :] = bbuf[tt:tt + PAD_B, :]
    cbuf[0:PAD_C, :] = cbuf[tt:tt + PAD_C, :]


def _const_spec(shape):
    nd = len(shape)
    return pl.BlockSpec(shape, lambda *_: (0,) * nd)


def _layer_spec(stacked, layer):
    nd = stacked.ndim - 1
    return pl.BlockSpec((None,) + stacked.shape[1:], lambda *_: (layer,) + (0,) * nd)


def _inproj_prompt(x2d, n_b, t_len, g1, wmain, layer, wsmall, caw, cbw, lng, lnb, ccw, alog, dtb):
    tt = min(TIME_TILE, t_len)
    nt = t_len // tt
    n_tok = n_b * t_len
    d = x2d.shape[1]
    row = lambda w: pl.BlockSpec((tt, w), lambda b, t: (b * nt + t, 0))
    st = lambda r, w: pl.BlockSpec((1, r, w), lambda b, t: (b, 0, 0))
    out_shape = (
        jax.ShapeDtypeStruct((n_tok, D_A + D_B), F32),
        jax.ShapeDtypeStruct((n_tok, D_C), F32),
        jax.ShapeDtypeStruct((n_tok, D_C), F32),
        jax.ShapeDtypeStruct((n_tok, D_C), F32),
        jax.ShapeDtypeStruct((n_tok, D_C), F32),
        jax.ShapeDtypeStruct((n_tok, LANES), F32),
        jax.ShapeDtypeStruct((n_b, W_A - 1, D_A), F32),
        jax.ShapeDtypeStruct((n_b, W_B - 1, D_B), F32),
        jax.ShapeDtypeStruct((n_b, W_C - 1, 3 * D_C), F32),
    )
    return pl.pallas_call(
        functools.partial(_inproj_prompt_kernel, tt=tt, nt=nt),
        grid=(n_b, nt),
        in_specs=[row(d), _const_spec(g1.shape), _layer_spec(wmain, layer), _const_spec(wsmall.shape),
                  _const_spec(caw.shape), _const_spec(cbw.shape), _const_spec(lng.shape), _const_spec(lnb.shape),
                  _const_spec(ccw.shape), _const_spec(alog.shape), _const_spec(dtb.shape)],
        out_specs=(row(D_A + D_B), row(D_C), row(D_C), row(D_C), row(D_C), row(LANES),
                   st(W_A - 1, D_A), st(W_B - 1, D_B), st(W_C - 1, 3 * D_C)),
        out_shape=out_shape,
        scratch_shapes=[pltpu.VMEM((PAD_A + tt, D_A), F32), pltpu.VMEM((PAD_B + tt, D_B), F32),
                        pltpu.VMEM((PAD_C + tt, 3 * D_C), F32), pltpu.VMEM((7, PAD_B + tt - 8, D_B), F32)],
        compiler_params=pltpu.CompilerParams(dimension_semantics=("arbitrary", "arbitrary"),
                                             vmem_limit_bytes=VMEM_LIMIT),
        name="inproj_prompt",
    )(x2d, g1, wmain, wsmall, caw, cbw, lng, lnb, ccw, alog, dtb)


def _dot_nt(a, b):
    return lax.dot_general(a, b, (((1,), (1,)), ((), ())), preferred_element_type=F32)


def _dot_tn(a, b):
    return lax.dot_general(a, b, (((0,), (0,)), ((), ())), preferred_element_type=F32)


def _bdot(a, b):
    return jnp.dot(a.astype(BF16), b.astype(BF16), preferred_element_type=F32)


def _gated_norm(o, on_g, gate):
    return o * lax.rsqrt(jnp.mean(o * o, axis=-1, keepdims=True) + EPS) * on_g * _silu(gate)


def _delta_prompt_kernel(q_ref, k_ref, v_ref, gb_ref, gate_ref, ong_ref, yc_ref, sfin_ref, s_scr, *, tq, nt):
    t = pl.program_id(1)

    @pl.when(t == 0)
    def _():
        s_scr[...] = jnp.zeros(s_scr.shape, F32)

    c = CHUNK
    n_chunks = tq // c
    n_double = c.bit_length() - 2
    inst = [(ch, h) for ch in range(n_chunks) for h in range(H_C)]
    rows = lambda ch: slice(ch * c, (ch + 1) * c)
    cols = lambda h: slice(h * DK, (h + 1) * DK)
    ri = lax.broadcasted_iota(jnp.int32, (c, c), 0)
    ci = lax.broadcasted_iota(jnp.int32, (c, c), 1)
    gb = gb_ref[...]

    q = [q_ref[rows(ch), cols(h)] for ch, h in inst]
    k = [k_ref[rows(ch), cols(h)] for ch, h in inst]
    v = [v_ref[rows(ch), cols(h)] for ch, h in inst]
    g_col = [gb[rows(ch), h:h + 1] for ch, h in inst]
    b_col = [gb[rows(ch), H_C + h:H_C + h + 1] for ch, h in inst]
    g_row = [jnp.sum(jnp.where(ri == ci, g, 0.0), axis=0, keepdims=True) for g in g_col]
    diff = [gc - gr for gc, gr in zip(g_col, g_row)]
    dec_strict = [jnp.exp(jnp.where(ri > ci, d, -jnp.inf)) for d in diff]
    dec_incl = [jnp.exp(jnp.where(ri >= ci, d, -jnp.inf)) for d in diff]
    e_g = [jnp.exp(g) for g in g_col]
    g_last = [g[c - 1:c, :] for g in g_col]

    qkk = [_dot_nt(jnp.concatenate([qi, ki], axis=0).astype(BF16), ki.astype(BF16)) for qi, ki in zip(q, k)]
    qk = [m[:c] * d for m, d in zip(qkk, dec_incl)]
    p = [-(b * m[c:] * d) for b, m, d in zip(b_col, qkk, dec_strict)]
    sol = [jnp.concatenate([vi * b, ki * (b * eg)], axis=1) for vi, ki, b, eg in zip(v, k, b_col, e_g)]
    for level in range(n_double + 1):
        sol = [s + _bdot(pi, s) for pi, s in zip(p, sol)]
        if level < n_double:
            p = [_bdot(pi, pi) for pi in p]
    kd = [ki * jnp.exp(gl - g) for ki, gl, g in zip(k, g_last, g_col)]
    qk_uw = [_bdot(m, s) for m, s in zip(qk, sol)]
    kd_uw = [_dot_tn(x.astype(BF16), s.astype(BF16)) for x, s in zip(kd, sol)]
    lhs = [jnp.concatenate([qi * eg - a[:, DV:], -b[:, DV:]], axis=0).astype(BF16)
           for qi, eg, a, b in zip(q, e_g, qk_uw, kd_uw)]
    decay = [jnp.exp(gl) for gl in g_last]

    state = [s_scr[h] for h in range(H_C)]
    for ch in range(n_chunks):
        base = ch * H_C
        r = [jnp.dot(lhs[base + h], state[h].astype(BF16), preferred_element_type=F32) for h in range(H_C)]
        for h in range(H_C):
            o = r[h][:c] + qk_uw[base + h][:, :DV]
            yc_ref[rows(ch), cols(h)] = _gated_norm(o, ong_ref[...], gate_ref[rows(ch), cols(h)])
        state = [decay[base + h] * state[h] + r[h][c:] + kd_uw[base + h][:, :DV] for h in range(H_C)]
    for h in range(H_C):
        s_scr[h] = state[h]

    @pl.when(t == nt - 1)
    def _():
        sfin_ref[0] = s_scr[...]


def _delta_prompt(q, k, v, gb, gate, ong, n_b, t_len):
    tq = min(TIME_TILE, t_len)
    nt = t_len // tq
    row = lambda w: pl.BlockSpec((tq, w), lambda b, t: (b * nt + t, 0))
    return pl.pallas_call(
        functools.partial(_delta_prompt_kernel, tq=tq, nt=nt),
        grid=(n_b, nt),
        in_specs=[row(D_C), row(D_C), row(D_C), row(LANES), row(D_C), _const_spec(ong.shape)],
        out_specs=(row(D_C), pl.BlockSpec((1, H_C, DK, DV), lambda b, t: (b, 0, 0, 0))),
        out_shape=(jax.ShapeDtypeStruct((n_b * t_len, D_C), F32),
                   jax.ShapeDtypeStruct((n_b, H_C, DK, DV), F32)),
        scratch_shapes=[pltpu.VMEM((H_C, DK, DV), F32)],
        compiler_params=pltpu.CompilerParams(dimension_semantics=("arbitrary", "arbitrary"),
                                             vmem_limit_bytes=VMEM_LIMIT),
        name="delta_prompt",
    )(q, k, v, gb, gate, ong)


def _inproj_sample_kernel(x_ref, g1_ref, wmain_ref, wsmall_ref, caw_ref, cbw_ref, lng_ref, lnb_ref,
                          ccw_ref, alog_ref, dtb_ref, sta_ref, stb_ref, stq_ref,
                          yab_ref, q_ref, k_ref, v_ref, gate_ref, gb_ref, nsta_ref, nstb_ref, nstq_ref):
    z_a, z_b, z_qkv, z_gate, z_s = _in_projection(x_ref[...], g1_ref, wmain_ref, wsmall_ref)

    def conv_step(state_ref, new_state_ref, new, w_ref, width):
        acc = new * w_ref[width - 1:width, :]
        for j in range(width - 1):
            acc = acc + state_ref[j] * w_ref[j:j + 1, :]
        for j in range(width - 2):
            new_state_ref[j] = state_ref[j + 1]
        new_state_ref[width - 2] = new
        return acc

    u_a = z_a[:, 256:512] * z_a[:, 512:768]
    yab_ref[:, 0:D_A] = z_a[:, 0:256] * conv_step(sta_ref, nsta_ref, u_a, caw_ref, W_A)

    u_b = z_b[:, 0:256] * _sigmoid(z_b[:, 256:512])
    yab_ref[:, D_A:D_A + D_B] = _layer_norm_silu(conv_step(stb_ref, nstb_ref, u_b, cbw_ref, W_B),
                                                 lng_ref[...], lnb_ref[...])

    qkv = _silu(conv_step(stq_ref, nstq_ref, z_qkv, ccw_ref, W_C))
    q_ref[...] = _l2norm_heads(qkv[:, 0:D_C], DK ** -0.5)
    k_ref[...] = _l2norm_heads(qkv[:, D_C:2 * D_C], 1.0)
    v_ref[...] = qkv[:, 2 * D_C:3 * D_C]
    gate_ref[...] = z_gate
    g, beta = _decay_and_beta(z_s, alog_ref, dtb_ref)
    gb_ref[...] = g + beta


def _inproj_sample(x2d, g1, wmain, layer, wsmall, caw, cbw, lng, lnb, ccw, alog, dtb, st_a, st_b, st_q):
    n = x2d.shape[0]
    args = (x2d, g1, wmain, wsmall, caw, cbw, lng, lnb, ccw, alog, dtb, st_a, st_b, st_q)
    out_shapes = tuple((n, w) for w in (D_A + D_B, D_C, D_C, D_C, D_C, LANES)) + (st_a.shape, st_b.shape, st_q.shape)
    return pl.pallas_call(
        _inproj_sample_kernel,
        grid=(1,),
        in_specs=[_layer_spec(a, layer) if a is wmain else _const_spec(a.shape) for a in args],
        out_specs=tuple(_const_spec(s) for s in out_shapes),
        out_shape=tuple(jax.ShapeDtypeStruct(s, F32) for s in out_shapes),
        compiler_params=pltpu.CompilerParams(dimension_semantics=("arbitrary",), vmem_limit_bytes=VMEM_LIMIT),
        name="inproj_sample",
    )(*args)


def _delta_sample_kernel(q_ref, k_ref, v_ref, gb_ref, gate_ref, ong_ref, s_ref, *rest, nb, layer):
    if layer:
        earlier_ref, yc_ref, stack_ref = rest
        stack_ref[0:layer] = earlier_ref[...]
    else:
        yc_ref, stack_ref = rest
    snew_ref = stack_ref.at[layer]
    q = q_ref[...]
    k = k_ref[...]
    rows = [k[:, h * DK:(h + 1) * DK] for h in range(H_C)] + [q[:, h * DK:(h + 1) * DK] for h in range(H_C)]
    rows.append(jnp.zeros((LANES - 2 * H_C * nb, DK), F32))
    kq_t = jnp.concatenate(rows, axis=0).T
    gb = gb_ref[...]
    inst = [(i, h) for i in range(nb) for h in range(H_C)]
    cols = lambda h: slice(h * DK, (h + 1) * DK)
    k_bc = [jnp.broadcast_to(kq_t[:, h * nb + i:h * nb + i + 1], (DK, DV)) for i, h in inst]
    q_bc = [jnp.broadcast_to(kq_t[:, (H_C + h) * nb + i:(H_C + h) * nb + i + 1], (DK, DV)) for i, h in inst]
    k_s = [jnp.sum(s_ref[i, h] * kb, axis=0, keepdims=True) for (i, h), kb in zip(inst, k_bc)]
    q_s = [jnp.sum(s_ref[i, h] * qb, axis=0, keepdims=True) for (i, h), qb in zip(inst, q_bc)]
    e_g = [jnp.exp(gb[i:i + 1, h:h + 1]) for i, h in inst]
    v_new = [gb[i:i + 1, H_C + h:H_C + h + 1] * (v_ref[i:i + 1, cols(h)] - eg * ks)
             for (i, h), eg, ks in zip(inst, e_g, k_s)]
    qk = [jnp.sum(q[i:i + 1, cols(h)] * k[i:i + 1, cols(h)], axis=-1, keepdims=True) for i, h in inst]
    for n, (i, h) in enumerate(inst):
        snew_ref[i, h] = e_g[n] * s_ref[i, h] + k_bc[n] * v_new[n]
    for n, (i, h) in enumerate(inst):
        o = e_g[n] * q_s[n] + qk[n] * v_new[n]
        yc_ref[i:i + 1, cols(h)] = _gated_norm(o, ong_ref[...], gate_ref[i:i + 1, cols(h)])


def _delta_sample(q, k, v, gb, gate, ong, state, layer, earlier):
    n = q.shape[0]
    nb = 8
    row = lambda w: pl.BlockSpec((nb, w), lambda i: (i, 0))
    st_in = pl.BlockSpec((None, nb, H_C, DK, DV), lambda i: (layer, i, 0, 0, 0))
    stack = lambda depth: pl.BlockSpec((depth, nb, H_C, DK, DV), lambda i: (0, i, 0, 0, 0))
    carried = [] if earlier is None else [earlier]
    return pl.pallas_call(
        functools.partial(_delta_sample_kernel, nb=nb, layer=layer),
        grid=(n // nb,),
        in_specs=[row(D_C), row(D_C), row(D_C), row(LANES), row(D_C), _const_spec(ong.shape), st_in]
                 + [stack(layer) for _ in carried],
        out_specs=(row(D_C), stack(layer + 1)),
        out_shape=(jax.ShapeDtypeStruct((n, D_C), F32),
                   jax.ShapeDtypeStruct((layer + 1,) + state.shape[1:], F32)),
        compiler_params=pltpu.CompilerParams(dimension_semantics=("arbitrary",), vmem_limit_bytes=VMEM_LIMIT),
        name="delta_sample",
    )(q, k, v, gb, gate, ong, state, *carried)


def _outproj_router_kernel(x_ref, yab_ref, yc_ref, xs_ref, yabs_ref, ycs_ref, woa_ref, woc_ref, g2_ref, wrt_ref,
                           x1_ref, h2_ref, route_ref, routet_ref, cnt_ref, *, n_prompt_tiles):
    is_prompt = pl.program_id(0) < n_prompt_tiles
    tm = x_ref.shape[0]

    def pick(prompt_ref, sample_ref):
        sample = sample_ref[...]
        sample = jnp.concatenate([sample, jnp.zeros((tm - sample.shape[0], sample.shape[1]), F32)], axis=0)
        return jnp.where(is_prompt, prompt_ref[...], sample)

    x1 = (pick(x_ref, xs_ref)
          + jnp.dot(pick(yab_ref, yabs_ref).astype(BF16), woa_ref[...], preferred_element_type=F32)
          + jnp.dot(pick(yc_ref, ycs_ref).astype(BF16), woc_ref[...], preferred_element_type=F32))
    x1_ref[...] = x1
    h2 = _rms(x1, g2_ref[...])
    h2_ref[...] = h2.astype(BF16)
    logits = _dot_3pass(h2, h2.astype(BF16), wrt_ref[...]).T[0:ROUTER_ROWS, :]
    row = lax.broadcasted_iota(jnp.int32, logits.shape, 0)
    neg = -jnp.inf
    gl = jnp.where(row < N_GROUPS, logits, neg)
    g_max = jnp.max(gl, axis=0, keepdims=True)
    g_idx = jnp.min(jnp.where(gl == g_max, row, LANES), axis=0, keepdims=True)
    g_p = 1.0 / jnp.sum(jnp.exp(gl - g_max), axis=0, keepdims=True)
    lo = EXPERT_LANE0 + g_idx * EXP_PER_GROUP
    in_group = (row >= lo) & (row < lo + EXP_PER_GROUP)
    el = jnp.where(in_group, logits, neg)
    e_max = jnp.max(el, axis=0, keepdims=True)
    pe = jnp.exp(el - e_max)
    e_prob = pe / jnp.sum(pe, axis=0, keepdims=True)
    p1 = jnp.max(e_prob, axis=0, keepdims=True)
    i1 = jnp.min(jnp.where(e_prob == p1, row, LANES), axis=0, keepdims=True)
    rest = jnp.where(in_group & (row != i1), e_prob, -1.0)
    p2 = jnp.max(rest, axis=0, keepdims=True)
    i2 = jnp.min(jnp.where(rest == p2, row, LANES), axis=0, keepdims=True)
    denom = p1 + p2
    w1 = g_p * (p1 / denom)
    w2 = g_p * (p2 / denom)
    e1 = i1 - EXPERT_LANE0
    e2 = i2 - EXPERT_LANE0
    oh1 = row == e1
    oh2 = row == e2
    picked = jnp.where(oh1 | oh2, 1.0, 0.0).astype(BF16)
    cnt = _dot_nt(jnp.ones((8, tm), BF16), picked)
    cnt_ref[0] = jnp.concatenate([cnt, jnp.zeros((8, LANES - ROUTER_ROWS), F32)], axis=1)
    ki = lax.broadcasted_iota(jnp.int32, (ROUTER_ROWS, ROUTER_ROWS), 0)
    kj = lax.broadcasted_iota(jnp.int32, (ROUTER_ROWS, ROUTER_ROWS), 1)
    cnt_col = jnp.sum(picked.astype(F32), axis=1, keepdims=True)
    pieces = jnp.broadcast_to(jnp.ceil(cnt_col * (1.0 / PIECE)), (ROUTER_ROWS, LANES)).astype(BF16)
    run_start = PIECE * jnp.dot(jnp.where(kj < ki, 1.0, 0.0).astype(BF16), pieces,
                                preferred_element_type=F32)[:, 0:1]
    ri = lax.broadcasted_iota(jnp.int32, (tm, tm), 0)
    ci = lax.broadcasted_iota(jnp.int32, (tm, tm), 1)
    base = jnp.dot(picked, jnp.where(ri < ci, 1.0, 0.0).astype(BF16), preferred_element_type=F32) + run_start
    pos1 = jnp.sum(jnp.where(oh1, base, 0.0), axis=0, keepdims=True)
    pos2 = jnp.sum(jnp.where(oh2, base, 0.0), axis=0, keepdims=True)
    routet = jnp.concatenate([e1.astype(F32), e2.astype(F32), w1, w2, pos1, pos2, jnp.zeros((2, tm), F32)], axis=0)
    routet_ref[0] = routet
    route_ref[...] = jnp.concatenate([routet, jnp.zeros((LANES - 8, tm), F32)], axis=0).T


def _outproj_router(prompt, sample, woa, woc, g2, wrt):
    x2d = prompt[0]
    n, d = x2d.shape
    tm = min(TOKEN_TILE, n)
    n_prompt_tiles = n // tm
    n_all = n + tm
    assert sample[0].shape[0] <= tm and sample[0].shape[0] % 8 == 0
    p_row = lambda w: pl.BlockSpec((tm, w), lambda i: (jnp.minimum(i, n_prompt_tiles - 1), 0))
    row = lambda w: pl.BlockSpec((tm, w), lambda i: (i, 0))
    return pl.pallas_call(
        functools.partial(_outproj_router_kernel, n_prompt_tiles=n_prompt_tiles),
        grid=(n_prompt_tiles + 1,),
        in_specs=[p_row(d), p_row(D_A + D_B), p_row(D_C)] + [_const_spec(a.shape) for a in sample]
                 + [_const_spec(woa.shape), _const_spec(woc.shape), _const_spec(g2.shape), _const_spec(wrt.shape)],
        out_specs=(row(d), row(d), row(LANES), pl.BlockSpec((1, 8, tm), lambda i: (i, 0, 0)),
                   pl.BlockSpec((1, 8, LANES), lambda i: (i, 0, 0))),
        out_shape=(jax.ShapeDtypeStruct((n_all, d), F32), jax.ShapeDtypeStruct((n_all, d), BF16),
                   jax.ShapeDtypeStruct((n_all, LANES), F32),
                   jax.ShapeDtypeStruct((n_all // tm, 8, tm), F32),
                   jax.ShapeDtypeStruct((n_all // tm, 8, LANES), F32)),
        compiler_params=pltpu.CompilerParams(dimension_semantics=("arbitrary",), vmem_limit_bytes=VMEM_LIMIT),
        name="outproj_router",
    )(*prompt, *sample, woa, woc, g2, wrt)


def _route_plan(cnt, n_tok, tt):
    n_tiles = n_tok // tt
    cnt = cnt[:, 0, :N_EXPERTS].astype(jnp.int32)
    pc = (cnt + PIECE - 1) // PIECE * PIECE
    local_end = jnp.cumsum(pc, axis=1)
    local_off = local_end - pc
    seg_len = jnp.sum(pc, axis=0)
    seg_pad = (seg_len + ROW_TILE - 1) // ROW_TILE * ROW_TILE
    seg_end = jnp.cumsum(seg_pad)
    seg_start = seg_end - seg_pad
    run_off = seg_start[None, :] + jnp.cumsum(pc, axis=0) - pc
    piece_row = jnp.arange(_local_rows(tt) // PIECE, dtype=jnp.int32) * PIECE
    in_run = ((local_off[:, None, :] <= piece_row[None, :, None])
              & (piece_row[None, :, None] < local_end[:, None, :]))
    piece_dst = jnp.sum(jnp.where(in_run, (run_off - local_off)[:, None, :], 0), axis=-1) + piece_row[None, :]
    n_row_tiles = _max_sorted_rows(n_tok, tt) // ROW_TILE
    tile_row0 = jnp.arange(n_row_tiles, dtype=jnp.int32) * ROW_TILE
    tile_expert = jnp.minimum(jnp.sum((seg_end[None, :] <= tile_row0[:, None]).astype(jnp.int32), axis=1),
                              N_EXPERTS - 1)
    used = seg_pad > 0
    order = jnp.cumsum(used.astype(jnp.int32)) - used.astype(jnp.int32)
    ids = jnp.arange(N_EXPERTS, dtype=jnp.int32)
    later = jnp.where(used[None, :] & (ids[None, :] > ids[:, None]), ids[None, :], N_EXPERTS)
    next_used = jnp.min(later, axis=1)
    return dict(
        piece_dst=piece_dst.reshape(-1), tile_pieces=jnp.sum(pc, axis=1) // PIECE,
        fill_off=seg_start + seg_len, fill_pieces=(seg_pad - seg_len) // PIECE,
        tile_expert=tile_expert, n_active=(seg_end[-1] // ROW_TILE).reshape(1),
        expert_order=order, next_used=next_used,
    )


def _max_sorted_rows(n_tok, tt):
    n_tiles = n_tok // tt
    rows = 2 * n_tok + n_tiles * N_EXPERTS * (PIECE - 1) + N_EXPERTS * (ROW_TILE - 1)
    return (rows + ROW_TILE - 1) // ROW_TILE * ROW_TILE


def _local_rows(tt):
    return 2 * tt + N_EXPERTS * PIECE


def _piece_copy(src, dst, sem):
    return pltpu.make_async_copy(src, dst, sem)


def _for_each_piece(n, body):
    def four(q, c):
        for u in range(4):
            body(q * 4 + u)
        return c
    lax.fori_loop(0, lax.shift_right_logical(n, 2), four, 0)

    def one(p, c):
        body(p)
        return c
    lax.fori_loop(n & ~3, n, one, 0)


def _wait_pieces(n, copy_of_rows):
    for bit in (64, 32, 16, 8, 4, 2, 1):
        @pl.when((n & bit) != 0)
        def _(bit=bit):
            copy_of_rows(bit * PIECE).wait()


def _dispatch_kernel(piece_dst, tile_pieces, fill_off, fill_pieces, n_active,
                     h2_ref, routet_ref, xs_hbm, lbuf, zbuf, sem, *, tt, n_tiles, n_row_tiles):
    i = pl.program_id(0)
    slot = i % 2
    lrows = _local_rows(tt)
    max_pieces = lrows // PIECE

    assert max_pieces < 128

    def wait_tile(tile, s):
        _wait_pieces(tile_pieces[tile],
                     lambda rows: _piece_copy(lbuf.at[s, pl.ds(0, rows)], xs_hbm.at[pl.ds(0, rows)], sem.at[s]))

    @pl.when(i >= 2)
    def _():
        wait_tile(i - 2, slot)

    pos = routet_ref[0]
    h2 = h2_ref[...]
    for r0 in range(0, lrows, SORT_CHUNK):
        row = (lax.broadcasted_iota(jnp.int32, (SORT_CHUNK, tt), 0) + r0).astype(F32)
        perm = jnp.where((row == pos[4:5, :]) | (row == pos[5:6, :]), 1.0, 0.0).astype(BF16)
        lbuf[slot, r0:r0 + SORT_CHUNK] = jnp.dot(perm, h2, preferred_element_type=F32).astype(BF16)

    def send(p):
        dst = piece_dst[i * max_pieces + p]
        _piece_copy(lbuf.at[slot, pl.ds(pl.multiple_of(p * PIECE, PIECE), PIECE)],
                    xs_hbm.at[pl.ds(pl.multiple_of(dst, PIECE), PIECE)], sem.at[slot]).start()
    _for_each_piece(tile_pieces[i], send)

    @pl.when(i == n_tiles - 1)
    def _():
        zbuf[...] = jnp.zeros(zbuf.shape, BF16)
        zpiece = zbuf.at[pl.ds(0, PIECE)]
        for e in range(N_EXPERTS):
            def body(p, c, e=e):
                _piece_copy(zpiece, xs_hbm.at[pl.ds(pl.multiple_of(fill_off[e] + p * PIECE, PIECE), PIECE)],
                            sem.at[2]).start()
                return c
            lax.fori_loop(0, fill_pieces[e], body, 0)

        def tail_body(j, c):
            _piece_copy(zbuf, xs_hbm.at[pl.ds(pl.multiple_of(j * ROW_TILE, ROW_TILE), ROW_TILE)], sem.at[3]).start()
            return c
        lax.fori_loop(n_active[0], n_row_tiles, tail_body, 0)
        for e in range(N_EXPERTS):
            def body(p, c):
                _piece_copy(zpiece, xs_hbm.at[pl.ds(0, PIECE)], sem.at[2]).wait()
                return c
            lax.fori_loop(0, fill_pieces[e], body, 0)

        def tail_wait(j, c):
            _piece_copy(zbuf, xs_hbm.at[pl.ds(0, ROW_TILE)], sem.at[3]).wait()
            return c
        lax.fori_loop(n_active[0], n_row_tiles, tail_wait, 0)
        if n_tiles > 1:
            wait_tile(i - 1, 1 - slot)
        wait_tile(i, slot)


def _dispatch(h2, routet, plan, tt):
    n, d = h2.shape
    n_tiles = n // tt
    n_rows = _max_sorted_rows(n, tt)
    kernel_fn = functools.partial(_dispatch_kernel, tt=tt, n_tiles=n_tiles, n_row_tiles=n_rows // ROW_TILE)
    grid_spec = pltpu.PrefetchScalarGridSpec(
        num_scalar_prefetch=5,
        grid=(n_tiles,),
        in_specs=[pl.BlockSpec((tt, d), lambda i, *_: (i, 0)),
                  pl.BlockSpec((1, 8, tt), lambda i, *_: (i, 0, 0))],
        out_specs=pl.BlockSpec(memory_space=pl.ANY),
        scratch_shapes=[pltpu.VMEM((2, _local_rows(tt), d), BF16), pltpu.VMEM((ROW_TILE, d), BF16),
                        pltpu.SemaphoreType.DMA((4,))],
    )
    return pl.pallas_call(
        kernel_fn,
        grid_spec=grid_spec,
        out_shape=jax.ShapeDtypeStruct((n_rows, d), BF16),
        compiler_params=pltpu.CompilerParams(dimension_semantics=("arbitrary",), vmem_limit_bytes=VMEM_LIMIT),
        name="moe_dispatch",
    )(plan["piece_dst"], plan["tile_pieces"], plan["fill_off"], plan["fill_pieces"], plan["n_active"],
      h2, routet)


def _experts_kernel(tile_expert, n_active, expert_order, next_used,
                    xs_hbm, wg_hbm, wu_hbm, wd_hbm, ys_hbm,
                    xbuf, ybuf, wbuf_g, wbuf_u, wbuf_d, wg_b, wu_b, wd_b, xsem, ysem, sem,
                    *, layer, n_row_tiles):
    n_act = n_active[0]

    def tile_rows(j):
        return pl.ds(pl.multiple_of(j * ROW_TILE, ROW_TILE), ROW_TILE)

    def x_copy(j, s):
        return _piece_copy(xs_hbm.at[tile_rows(j)], xbuf.at[s], xsem.at[s])

    def y_copy(j, s):
        return _piece_copy(ybuf.at[s], ys_hbm.at[tile_rows(j)], ysem.at[s])

    def weight_copies(e, s):
        return (_piece_copy(wg_hbm.at[layer, e], wbuf_g.at[s], sem.at[s, 0]),
                _piece_copy(wu_hbm.at[layer, e], wbuf_u.at[s], sem.at[s, 1]),
                _piece_copy(wd_hbm.at[layer, e], wbuf_d.at[s], sem.at[s, 2]))

    first = tile_expert[0]
    for c in weight_copies(first, expert_order[first] % 2):
        c.start()
    x_copy(0, 0).start()

    def body(j, carry):
        s = j % 2
        expert = tile_expert[j]
        previous = tile_expert[jnp.maximum(j - 1, 0)]
        half = expert_order[expert] % 2
        x_copy(j, s).wait()

        @pl.when(j + 1 < n_act)
        def _():
            x_copy(j + 1, 1 - s).start()

        @pl.when((j == 0) | (expert != previous))
        def _():
            for c in weight_copies(expert, half):
                c.wait()
            wg_b[...] = wbuf_g[half].astype(BF16)
            wu_b[...] = wbuf_u[half].astype(BF16)
            wd_b[...] = wbuf_d[half].astype(BF16)
            upcoming = next_used[expert]

            @pl.when(upcoming < N_EXPERTS)
            def _():
                for c in weight_copies(upcoming, 1 - half):
                    c.start()

        @pl.when(j >= 2)
        def _():
            y_copy(j - 2, s).wait()

        x = xbuf[s]
        gate = jnp.dot(x, wg_b[...], preferred_element_type=F32)
        up = jnp.dot(x, wu_b[...], preferred_element_type=F32)
        act = (_silu(gate) * up).astype(BF16)
        ybuf[s] = jnp.dot(act, wd_b[...], preferred_element_type=F32).astype(BF16)
        y_copy(j, s).start()
        return carry

    lax.fori_loop(0, n_act, body, 0)

    @pl.when(n_act >= 2)
    def _():
        y_copy(n_act - 2, n_act % 2).wait()
    y_copy(n_act - 1, (n_act - 1) % 2).wait()

    ybuf[0] = jnp.zeros(ybuf.shape[1:], BF16)

    def zero_start(j, c):
        y_copy(j, 0).start()
        return c
    lax.fori_loop(n_act, n_row_tiles, zero_start, 0)

    def zero_wait(j, c):
        y_copy(j, 0).wait()
        return c
    lax.fori_loop(n_act, n_row_tiles, zero_wait, 0)


def _experts(xs, wg, wu, wd, layer, plan):
    rows, d = xs.shape
    d_e = wg.shape[-1]
    any_space = pl.BlockSpec(memory_space=pl.ANY)
    grid_spec = pltpu.PrefetchScalarGridSpec(
        num_scalar_prefetch=4,
        grid=(1,),
        in_specs=[any_space, any_space, any_space, any_space],
        out_specs=any_space,
        scratch_shapes=[pltpu.VMEM((2, ROW_TILE, d), BF16), pltpu.VMEM((2, ROW_TILE, d), BF16),
                        pltpu.VMEM((2, d, d_e), F32), pltpu.VMEM((2, d, d_e), F32), pltpu.VMEM((2, d_e, d), F32),
                        pltpu.VMEM((d, d_e), BF16), pltpu.VMEM((d, d_e), BF16), pltpu.VMEM((d_e, d), BF16),
                        pltpu.SemaphoreType.DMA((2,)), pltpu.SemaphoreType.DMA((2,)),
                        pltpu.SemaphoreType.DMA((2, 3))],
    )
    return pl.pallas_call(
        functools.partial(_experts_kernel, layer=layer, n_row_tiles=rows // ROW_TILE),
        grid_spec=grid_spec,
        out_shape=jax.ShapeDtypeStruct((rows, d), BF16),
        compiler_params=pltpu.CompilerParams(dimension_semantics=("arbitrary",), vmem_limit_bytes=VMEM_LIMIT),
        name="moe_experts",
    )(plan["tile_expert"], plan["n_active"], plan["expert_order"], plan["next_used"], xs, wg, wu, wd)


def _combine_kernel(piece_src, tile_pieces, x1_ref, route_ref, fg_ref, ys_hbm, out_ref, outs_ref, lbuf, sem,
                    *, tt, n_tiles, final_norm):
    i = pl.program_id(0)
    slot = i % 2
    lrows = _local_rows(tt)
    max_pieces = lrows // PIECE

    assert max_pieces < 128

    def fetch_tile(tile, s):
        def fetch(p):
            src = piece_src[tile * max_pieces + p]
            _piece_copy(ys_hbm.at[pl.ds(pl.multiple_of(src, PIECE), PIECE)],
                        lbuf.at[s, pl.ds(pl.multiple_of(p * PIECE, PIECE), PIECE)], sem.at[s]).start()
        _for_each_piece(tile_pieces[tile], fetch)

    @pl.when(i == 0)
    def _():
        lbuf[...] = jnp.zeros(lbuf.shape, BF16)
        fetch_tile(0, 0)

    @pl.when(i + 1 < n_tiles)
    def _():
        fetch_tile(i + 1, 1 - slot)

    _wait_pieces(tile_pieces[i], lambda rows: _piece_copy(ys_hbm.at[pl.ds(0, rows)],
                                                         lbuf.at[slot, pl.ds(0, rows)], sem.at[slot]))

    route = route_ref[...]
    out = x1_ref[...]
    for r0 in range(0, lrows, SORT_CHUNK):
        col = (lax.broadcasted_iota(jnp.int32, (tt, SORT_CHUNK), 1) + r0).astype(F32)
        weights = (jnp.where(col == route[:, 4:5], route[:, 2:3], 0.0)
                   + jnp.where(col == route[:, 5:6], route[:, 3:4], 0.0)).astype(BF16)
        out = out + jnp.dot(weights, lbuf[slot, r0:r0 + SORT_CHUNK], preferred_element_type=F32)
    out = _rms(out, fg_ref[...]) if final_norm else out

    @pl.when(i < n_tiles - 1)
    def _():
        out_ref[...] = out

    @pl.when(i == n_tiles - 1)
    def _():
        outs_ref[...] = out[0:outs_ref.shape[0]]


def _combine(x1, route, ys, fg, plan, tt, n_sample, final_norm):
    n, d = x1.shape
    n_tiles = n // tt
    kernel_fn = functools.partial(_combine_kernel, tt=tt, n_tiles=n_tiles, final_norm=final_norm)
    grid_spec = pltpu.PrefetchScalarGridSpec(
        num_scalar_prefetch=2,
        grid=(n_tiles,),
        in_specs=[pl.BlockSpec((tt, d), lambda i, *_: (i, 0)),
                  pl.BlockSpec((tt, LANES), lambda i, *_: (i, 0)),
                  pl.BlockSpec(fg.shape, lambda i, *_: (0, 0)),
                  pl.BlockSpec(memory_space=pl.ANY)],
        out_specs=(pl.BlockSpec((tt, d), lambda i, *_: (jnp.minimum(i, n_tiles - 2), 0)),
                   pl.BlockSpec((n_sample, d), lambda i, *_: (0, 0))),
        scratch_shapes=[pltpu.VMEM((2, _local_rows(tt), d), BF16), pltpu.SemaphoreType.DMA((2,))],
    )
    return pl.pallas_call(
        kernel_fn,
        grid_spec=grid_spec,
        out_shape=(jax.ShapeDtypeStruct((n - tt, d), F32), jax.ShapeDtypeStruct((n_sample, d), F32)),
        compiler_params=pltpu.CompilerParams(dimension_semantics=("arbitrary",), vmem_limit_bytes=VMEM_LIMIT),
        name="moe_combine",
    )(plan["piece_dst"], plan["tile_pieces"], x1, route, fg, ys)


def _routed_moe(x1, h2, route, routet, cnt, wg, wu, wd, layer, fg, n_sample, final_norm):
    n = x1.shape[0]
    tt = n // cnt.shape[0]
    plan = _route_plan(cnt, n, tt)
    xs = _dispatch(h2, routet, plan, tt)
    ys = _experts(xs, wg, wu, wd, layer, plan)
    return _combine(x1, route, ys, fg, plan, tt, n_sample, final_norm)


def _pad_lanes(a):
    a = a.reshape((1, -1)) if a.ndim == 1 else a
    return jnp.pad(a, ((0, 0), (0, LANES - a.shape[1])))


def kernel(x_prompt, x_sample, state_conv_a, state_conv_b, state_conv_qkv, state_delta, norm1_g, w_in, conv_a_w,
           conv_b_w, ln_b_g, ln_b_b, conv_c_w, a_log, dt_bias, o_norm_g, w_out, norm2_g, w_group, w_router,
           w_gate, w_up, w_down, final_g):
    n_b, t_len, d = x_prompt.shape
    n_s = x_sample.shape[0]
    depth = w_in.shape[0]
    n_main = 3 * D_A + 2 * D_B + 3 * D_C + D_C
    xp = x_prompt.reshape(n_b * t_len, d)
    xs = x_sample.reshape(n_s, d)
    fg = final_g.reshape(1, d)
    outs = {k: [] for k in ("pa", "pb", "pq", "pd", "sa", "sb", "sq")}
    sample_states = None
    wmain = w_in.astype(BF16)
    for l in range(depth):
        g1 = norm1_g[l].reshape(1, d)
        wsmall = _pad_lanes(jnp.concatenate([w_in[l, :, n_main + H_C:], w_in[l, :, n_main:n_main + H_C]], axis=1))
        caw, cbw, ccw = conv_a_w[l], conv_b_w[l], conv_c_w[l]
        lng, lnb = ln_b_g[l].reshape(1, D_B), ln_b_b[l].reshape(1, D_B)
        alog, dtb = _pad_lanes(a_log[l]), _pad_lanes(dt_bias[l])
        ong = o_norm_g[l].reshape(1, DV)
        woa = w_out[l][:D_A + D_B].astype(BF16)
        woc = w_out[l][D_A + D_B:].astype(BF16)
        g2 = norm2_g[l].reshape(1, d)
        wrt = _pad_lanes(jnp.concatenate(
            [w_group[l], w_router[l].transpose(1, 0, 2).reshape(d, N_EXPERTS)], axis=1))
        last = l == depth - 1

        yab, q, k, v, gate, gb, st_a, st_b, st_q = _inproj_prompt(
            xp, n_b, t_len, g1, wmain, l, wsmall, caw, cbw, lng, lnb, ccw, alog, dtb)
        yc, s_fin = _delta_prompt(q, k, v, gb, gate, ong, n_b, t_len)
        outs["pa"].append(st_a); outs["pb"].append(st_b); outs["pq"].append(st_q); outs["pd"].append(s_fin)

        hist_a = state_conv_a[l].transpose(1, 0, 2)
        hist_b = state_conv_b[l].transpose(1, 0, 2)
        hist_q = state_conv_qkv[l].transpose(1, 0, 2)
        yab_s, q, k, v, gate, gb, hist_a, hist_b, hist_q = _inproj_sample(
            xs, g1, wmain, l, wsmall, caw, cbw, lng, lnb, ccw, alog, dtb, hist_a, hist_b, hist_q)
        yc_s, sample_states = _delta_sample(q, k, v, gb, gate, ong, state_delta, l, sample_states)

        x1, h2, route, routet, cnt = _outproj_router((xp, yab, yc), (xs, yab_s, yc_s), woa, woc, g2, wrt)
        xp, xs = _routed_moe(x1, h2, route, routet, cnt, w_gate, w_up, w_down, l, fg, n_s, last)
        outs["sa"].append(hist_a.transpose(1, 0, 2))
        outs["sb"].append(hist_b.transpose(1, 0, 2))
        outs["sq"].append(hist_q.transpose(1, 0, 2))

    stack = lambda key: jnp.stack(outs[key])
    return (xp.reshape(n_b, t_len, d), xs.reshape(n_s, 1, d),
            stack("pa"), stack("pb"), stack("pq"), stack("pd"),
            stack("sa"), stack("sb"), stack("sq"), sample_states)
```
